```python
import math
import jax, jax.numpy as jnp
from jax import lax
import numpy as np

D_MODEL = 2048
BATCH = 4
SEQ = 2048
DEPTH = 4
DEC_BATCH = 32
DEC_SEQ = 4
PAST_LEN = 16384
PAGE_SIZE = 128

N_MIXERS = 3
N_FOX = (DEPTH + 2) // 3
N_SWA = (DEPTH + 1) // 3
N_LRU = DEPTH // 3
FOX_HEADS = 16
FOX_KV_HEADS = 4
FOX_HEAD_DIM = D_MODEL // FOX_HEADS
SWA_HEADS = 32
SWA_KV_HEADS = 4
SWA_HEAD_DIM = D_MODEL // SWA_HEADS
WINDOW = 128
Q_BLOCK = 128
ROPE_THETA = 500000.0
ROT_DIM = SWA_HEAD_DIM // 4
D_RNN = D_MODEL
LRU_BLOCKS = 8
LRU_BLOCK = D_RNN // LRU_BLOCKS
CONV_W = 4
LRU_C = 8.0
D_FF = 4 * D_MODEL
RMS_EPS = 1e-6

kernel_name = 'fox_swa_rglru_hybrid_step'


def rmsnorm(x, g):
    xf = x.astype(jnp.float32)
    y = xf * lax.rsqrt(jnp.mean(jnp.square(xf), axis=-1, keepdims=True) + RMS_EPS)
    return (y * g.astype(jnp.float32)).astype(x.dtype)


def split_qkv(qkv, n_heads, n_kv, hd):
    b, t, _ = qkv.shape
    q = qkv[..., :n_heads * hd].reshape(b, t, n_heads, hd)
    k = qkv[..., n_heads * hd:(n_heads + n_kv) * hd].reshape(b, t, n_kv, hd)
    v = qkv[..., (n_heads + n_kv) * hd:].reshape(b, t, n_kv, hd)
    return q, k, v


def partial_rope(x, pos):
    half = ROT_DIM // 2
    inv_freq = ROPE_THETA ** (-jnp.arange(half, dtype=jnp.float32) / half)
    ang = pos.astype(jnp.float32)[:, None] * inv_freq[None, :]
    cos = jnp.cos(ang)[:, None, :]
    sin = jnp.sin(ang)[:, None, :]
    xr = x[..., :ROT_DIM].astype(jnp.float32)
    x1, x2 = xr[..., :half], xr[..., half:]
    rot = jnp.concatenate([x1 * cos - x2 * sin, x2 * cos + x1 * sin], axis=-1)
    return jnp.concatenate([rot.astype(x.dtype), x[..., ROT_DIM:]], axis=-1)


def fox_attend(q, k, v, cq, ck, q_pos, k_pos):
    b, t, h, d = q.shape
    s = k.shape[1]
    g = h // FOX_KV_HEADS
    qg = q.reshape(b, t, FOX_KV_HEADS, g, d)
    logits = jnp.einsum('btkgd,bskd->bkgts', qg, k, preferred_element_type=jnp.float32) * (d ** -0.5)
    cq_ = cq.reshape(b, t, FOX_KV_HEADS, g).transpose(0, 2, 3, 1)[..., :, None]
    ck_ = ck.reshape(b, s, FOX_KV_HEADS, g).transpose(0, 2, 3, 1)[..., None, :]
    causal = q_pos[:, None] >= k_pos[None, :]
    logits = jnp.where(causal, logits + (cq_ - ck_), -jnp.inf)
    p = jax.nn.softmax(logits, axis=-1)
    out = jnp.einsum('bkgts,bskd->btkgd', p.astype(v.dtype), v)
    return out.reshape(b, t, h * d)


def fox_project(xn, w_qkv, w_f, b_f):
    q, k, v = split_qkv(xn @ w_qkv, FOX_HEADS, FOX_KV_HEADS, FOX_HEAD_DIM)
    logf = jax.nn.log_sigmoid((xn @ w_f + b_f).astype(jnp.float32))
    return q, k, v, logf


def fox_prompt(xn, w_qkv, w_f, b_f, w_o):
    b, s, _ = xn.shape
    q, k, v, logf = fox_project(xn, w_qkv, w_f, b_f)
    c = lax.cumsum(logf, axis=1)
    pos = jnp.arange(s)
    nb = s // Q_BLOCK
    qb = q.reshape(b, nb, Q_BLOCK, FOX_HEADS, FOX_HEAD_DIM).swapaxes(0, 1)
    cqb = c.reshape(b, nb, Q_BLOCK, FOX_HEADS).swapaxes(0, 1)
    pb = pos.reshape(nb, Q_BLOCK)
    out = lax.map(lambda blk: fox_attend(blk[0], k, v, blk[1], c, blk[2], pos), (qb, cqb, pb))
    out = out.swapaxes(0, 1).reshape(b, s, FOX_HEADS * FOX_HEAD_DIM)
    return out @ w_o, k, v, logf


def fox_sample(xn, cache_k, cache_v, cache_logf, layer, page_table, w_qkv, w_f, b_f, w_o):
    b, t, _ = xn.shape
    past = page_table.shape[1] * PAGE_SIZE
    q, k, v, logf = fox_project(xn, w_qkv, w_f, b_f)
    k_past = cache_k[layer, page_table].reshape(b, past, FOX_KV_HEADS, FOX_HEAD_DIM)
    v_past = cache_v[layer, page_table].reshape(b, past, FOX_KV_HEADS, FOX_HEAD_DIM)
    lf_past = cache_logf[layer, page_table].reshape(b, past, FOX_HEADS).astype(jnp.float32)
    suffix = lax.cumsum(lf_past, axis=1, reverse=True)
    r_past = jnp.concatenate([suffix[:, 1:], jnp.zeros_like(suffix[:, :1])], axis=1)
    c_new = lax.cumsum(logf, axis=1)
    ck = jnp.concatenate([-r_past, c_new], axis=1)
    k_all = jnp.concatenate([k_past, k], axis=1)
    v_all = jnp.concatenate([v_past, v], axis=1)
    q_pos = past + jnp.arange(t)
    k_pos = jnp.arange(past + t)
    out = fox_attend(q, k_all, v_all, c_new, ck, q_pos, k_pos)
    return out @ w_o, k, v, logf


def swa_attend(q, k, v, sinks, q_pos, k_pos):
    b, t, h, d = q.shape
    g = h // SWA_KV_HEADS
    qg = q.reshape(b, t, SWA_KV_HEADS, g, d)
    logits = jnp.einsum('btkgd,bskd->bkgts', qg, k, preferred_element_type=jnp.float32) * (d ** -0.5)
    diff = q_pos[:, None] - k_pos[None, :]
    band = (diff >= 0) & (diff < WINDOW) & (k_pos >= 0)[None, :]
    logits = jnp.where(band, logits, -jnp.inf)
    sink = sinks.astype(jnp.float32).reshape(SWA_KV_HEADS, g)[None, :, :, None, None]
    m = jnp.maximum(jnp.max(logits, axis=-1, keepdims=True), sink)
    e = jnp.exp(logits - m)
    p = e / (jnp.sum(e, axis=-1, keepdims=True) + jnp.exp(sink - m))
    out = jnp.einsum('bkgts,bskd->btkgd', p.astype(v.dtype), v)
    return out.reshape(b, t, h * d)


def swa_prompt(xn, w_qkv, b_qkv, sinks, w_o, b_o):
    b, s, _ = xn.shape
    q, k, v = split_qkv(xn @ w_qkv + b_qkv, SWA_HEADS, SWA_KV_HEADS, SWA_HEAD_DIM)
    pos = jnp.arange(s)
    q = partial_rope(q, pos)
    k = partial_rope(k, pos)
    nb = s // Q_BLOCK

    def band_blocks(z):
        prev = jnp.pad(z, ((0, 0), (Q_BLOCK, 0), (0, 0), (0, 0)))[:, :s]
        shp = (b, nb, Q_BLOCK) + z.shape[2:]
        return jnp.concatenate([prev.reshape(shp), z.reshape(shp)], axis=2)

    qb = q.reshape(b, nb, Q_BLOCK, SWA_HEADS, SWA_HEAD_DIM)
    qpos = pos.reshape(nb, Q_BLOCK)
    kpos = (jnp.arange(nb)[:, None] - 1) * Q_BLOCK + jnp.arange(2 * Q_BLOCK)[None, :]
    out = jax.vmap(swa_attend, in_axes=(1, 1, 1, None, 0, 0), out_axes=1)(
        qb, band_blocks(k), band_blocks(v), sinks, qpos, kpos)
    out = out.reshape(b, s, SWA_HEADS * SWA_HEAD_DIM)
    keep = min(WINDOW, s)
    return out @ w_o + b_o, k[:, s - keep:], v[:, s - keep:]


def swa_sample(xn, buf_k, buf_v, past, w_qkv, b_qkv, sinks, w_o, b_o):
    b, t, _ = xn.shape
    nbuf = buf_k.shape[1]
    q, k, v = split_qkv(xn @ w_qkv + b_qkv, SWA_HEADS, SWA_KV_HEADS, SWA_HEAD_DIM)
    pos = past + jnp.arange(t)
    q = partial_rope(q, pos)
    k = partial_rope(k, pos)
    k_all = jnp.concatenate([buf_k, k], axis=1)
    v_all = jnp.concatenate([buf_v, v], axis=1)
    k_pos = past - nbuf + jnp.arange(nbuf + t)
    out = swa_attend(q, k_all, v_all, sinks, pos, k_pos)
    return out @ w_o + b_o, k_all[:, t:], v_all[:, t:]


def rglru_block(xn, h0, conv_buf, w_gate, b_gate, w_in, b_in, conv_w, conv_b,
                w_a, b_a, w_x, b_x, lam, w_out, b_out):
    b, t, _ = xn.shape
    gate = jax.nn.gelu(xn @ w_gate + b_gate)
    u = xn @ w_in + b_in
    u_ext = jnp.concatenate([conv_buf.astype(u.dtype), u], axis=1)
    xc = conv_b
    for i in range(CONV_W):
        xc = xc + u_ext[:, i:i + t] * conv_w[i]
    xb = xc.reshape(b, t, LRU_BLOCKS, LRU_BLOCK)
    r = jax.nn.sigmoid(jnp.einsum('btnc,ncd->btnd', xb, w_a).reshape(b, t, D_RNN) + b_a)
    ig = jax.nn.sigmoid(jnp.einsum('btnc,ncd->btnd', xb, w_x).reshape(b, t, D_RNN) + b_x)
    log_a = LRU_C * r.astype(jnp.float32) * jax.nn.log_sigmoid(lam.astype(jnp.float32))
    a = jnp.exp(log_a)
    drive = jnp.sqrt(-jnp.expm1(2.0 * log_a)) * (ig * xc).astype(jnp.float32)

    def step(h, inp):
        a_t, d_t = inp
        h = a_t * h + d_t
        return h, h

    h_last, hs = lax.scan(step, h0.astype(jnp.float32), (a.swapaxes(0, 1), drive.swapaxes(0, 1)))
    y = hs.swapaxes(0, 1).astype(xn.dtype) * gate
    return y @ w_out + b_out, h_last.astype(xn.dtype), u_ext[:, t:]


def sq_relu_mlp(xn, w_up, w_down):
    return jnp.square(jax.nn.relu(xn @ w_up)) @ w_down


def setup_inputs(seed: int = 0) -> dict:
    key = jax.random.key(seed)
    ks = iter(jax.random.split(key, 48))

    def nrm(shape, scale):
        return scale * jax.random.normal(next(ks), shape, jnp.float32)

    n_pages = PAST_LEN // PAGE_SIZE
    n_used = DEC_BATCH * n_pages
    n_pool = n_used + n_used // 4
    swa_buf = min(WINDOW, PAST_LEN)
    fox_qkv = (FOX_HEADS + 2 * FOX_KV_HEADS) * FOX_HEAD_DIM
    swa_qkv = (SWA_HEADS + 2 * SWA_KV_HEADS) * SWA_HEAD_DIM
    x_prompt = nrm((BATCH, SEQ, D_MODEL), 1.0)
    x_sample = nrm((DEC_BATCH, DEC_SEQ, D_MODEL), 1.0)
    cache_fox_k = nrm((N_FOX, n_pool, PAGE_SIZE, FOX_KV_HEADS, FOX_HEAD_DIM), 1.0)
    cache_fox_v = nrm((N_FOX, n_pool, PAGE_SIZE, FOX_KV_HEADS, FOX_HEAD_DIM), 1.0)
    cache_fox_logf = jax.nn.log_sigmoid(3.0 + nrm((N_FOX, n_pool, PAGE_SIZE, FOX_HEADS), 1.0))
    cache_swa_k = nrm((N_SWA, DEC_BATCH, swa_buf, SWA_KV_HEADS, SWA_HEAD_DIM), 1.0)
    cache_swa_v = nrm((N_SWA, DEC_BATCH, swa_buf, SWA_KV_HEADS, SWA_HEAD_DIM), 1.0)
    state_lru_h = nrm((N_LRU, DEC_BATCH, D_RNN), 0.5)
    state_lru_conv = nrm((N_LRU, DEC_BATCH, CONV_W - 1, D_RNN), 1.0)
    page_table = jax.random.permutation(next(ks), n_pool)[:n_used].reshape(DEC_BATCH, n_pages).astype(jnp.int32)
    u = jax.random.uniform(next(ks), (N_LRU, D_RNN), jnp.float32, 0.9, 0.999)
    p = u ** (1.0 / LRU_C)
    lru_lambda = jnp.log(p) - jnp.log1p(-p)
    return {
        'x_prompt': x_prompt,
        'x_sample': x_sample,
        'cache_fox_k': cache_fox_k,
        'cache_fox_v': cache_fox_v,
        'cache_fox_logf': cache_fox_logf,
        'cache_swa_k': cache_swa_k,
        'cache_swa_v': cache_swa_v,
        'state_lru_h': state_lru_h,
        'state_lru_conv': state_lru_conv,
        'page_table': page_table,
        'norm_mix_pre': 1.0 + nrm((DEPTH, D_MODEL), 0.05),
        'norm_mix_post': 1.0 + nrm((DEPTH, D_MODEL), 0.05),
        'norm_mlp_pre': 1.0 + nrm((DEPTH, D_MODEL), 0.05),
        'norm_mlp_post': 1.0 + nrm((DEPTH, D_MODEL), 0.05),
        'mlp_w_up': nrm((DEPTH, D_MODEL, D_FF), D_MODEL ** -0.5),
        'mlp_w_down': nrm((DEPTH, D_FF, D_MODEL), D_FF ** -0.5),
        'fox_w_qkv': nrm((N_FOX, D_MODEL, fox_qkv), D_MODEL ** -0.5),
        'fox_w_f': nrm((N_FOX, D_MODEL, FOX_HEADS), 0.5 * D_MODEL ** -0.5),
        'fox_b_f': jax.random.uniform(next(ks), (N_FOX, FOX_HEADS), jnp.float32, 1.0, 6.0),
        'fox_w_o': nrm((N_FOX, FOX_HEADS * FOX_HEAD_DIM, D_MODEL), (FOX_HEADS * FOX_HEAD_DIM) ** -0.5),
        'swa_w_qkv': nrm((N_SWA, D_MODEL, swa_qkv), D_MODEL ** -0.5),
        'swa_b_qkv': nrm((N_SWA, swa_qkv), 0.02),
        'swa_sinks': nrm((N_SWA, SWA_HEADS), 0.5),
        'swa_w_o': nrm((N_SWA, SWA_HEADS * SWA_HEAD_DIM, D_MODEL), (SWA_HEADS * SWA_HEAD_DIM) ** -0.5),
        'swa_b_o': nrm((N_SWA, D_MODEL), 0.02),
        'lru_w_gate': nrm((N_LRU, D_MODEL, D_RNN), D_MODEL ** -0.5),
        'lru_b_gate': nrm((N_LRU, D_RNN), 0.02),
        'lru_w_in': nrm((N_LRU, D_MODEL, D_RNN), D_MODEL ** -0.5),
        'lru_b_in': nrm((N_LRU, D_RNN), 0.02),
        'lru_conv_w': nrm((N_LRU, CONV_W, D_RNN), CONV_W ** -0.5),
        'lru_conv_b': nrm((N_LRU, D_RNN), 0.02),
        'lru_w_a': nrm((N_LRU, LRU_BLOCKS, LRU_BLOCK, LRU_BLOCK), LRU_BLOCK ** -0.5),
        'lru_b_a': nrm((N_LRU, D_RNN), 0.02),
        'lru_w_x': nrm((N_LRU, LRU_BLOCKS, LRU_BLOCK, LRU_BLOCK), LRU_BLOCK ** -0.5),
        'lru_b_x': nrm((N_LRU, D_RNN), 0.02),
        'lru_lambda': lru_lambda,
        'lru_w_out': nrm((N_LRU, D_RNN, D_MODEL), D_RNN ** -0.5),
        'lru_b_out': nrm((N_LRU, D_MODEL), 0.02),
    }


def reference(x_prompt, x_sample, cache_fox_k, cache_fox_v, cache_fox_logf, cache_swa_k, cache_swa_v,
              state_lru_h, state_lru_conv, page_table,
              norm_mix_pre, norm_mix_post, norm_mlp_pre, norm_mlp_post, mlp_w_up, mlp_w_down,
              fox_w_qkv, fox_w_f, fox_b_f, fox_w_o,
              swa_w_qkv, swa_b_qkv, swa_sinks, swa_w_o, swa_b_o,
              lru_w_gate, lru_b_gate, lru_w_in, lru_b_in, lru_conv_w, lru_conv_b,
              lru_w_a, lru_b_a, lru_w_x, lru_b_x, lru_lambda, lru_w_out, lru_b_out):
    xp, xs = x_prompt, x_sample
    bp = xp.shape[0]
    past = page_table.shape[1] * PAGE_SIZE
    fkp, fvp, flp, fks, fvs, fls = [], [], [], [], [], []
    skp, svp, sks, svs = [], [], [], []
    lhp, lcp, lhs, lcs = [], [], [], []
    for i in range(DEPTH):
        kind, j = i % N_MIXERS, i // N_MIXERS
        hp = rmsnorm(xp, norm_mix_pre[i])
        hs = rmsnorm(xs, norm_mix_pre[i])
        if kind == 0:
            mp, k_, v_, l_ = fox_prompt(hp, fox_w_qkv[j], fox_w_f[j], fox_b_f[j], fox_w_o[j])
            fkp.append(k_); fvp.append(v_); flp.append(l_)
            ms, k_, v_, l_ = fox_sample(hs, cache_fox_k, cache_fox_v, cache_fox_logf, j, page_table,
                                        fox_w_qkv[j], fox_w_f[j], fox_b_f[j], fox_w_o[j])
            fks.append(k_); fvs.append(v_); fls.append(l_)
        elif kind == 1:
            mp, k_, v_ = swa_prompt(hp, swa_w_qkv[j], swa_b_qkv[j], swa_sinks[j], swa_w_o[j], swa_b_o[j])
            skp.append(k_); svp.append(v_)
            ms, k_, v_ = swa_sample(hs, cache_swa_k[j], cache_swa_v[j], past,
                                    swa_w_qkv[j], swa_b_qkv[j], swa_sinks[j], swa_w_o[j], swa_b_o[j])
            sks.append(k_); svs.append(v_)
        else:
            lw = (lru_w_gate[j], lru_b_gate[j], lru_w_in[j], lru_b_in[j], lru_conv_w[j], lru_conv_b[j],
                  lru_w_a[j], lru_b_a[j], lru_w_x[j], lru_b_x[j], lru_lambda[j], lru_w_out[j], lru_b_out[j])
            h0 = jnp.zeros((bp, D_RNN), xp.dtype)
            c0 = jnp.zeros((bp, CONV_W - 1, D_RNN), xp.dtype)
            mp, h_, c_ = rglru_block(hp, h0, c0, *lw)
            lhp.append(h_); lcp.append(c_)
            ms, h_, c_ = rglru_block(hs, state_lru_h[j], state_lru_conv[j], *lw)
            lhs.append(h_); lcs.append(c_)
        xp = xp + rmsnorm(mp, norm_mix_post[i])
        xs = xs + rmsnorm(ms, norm_mix_post[i])
        xp = xp + rmsnorm(sq_relu_mlp(rmsnorm(xp, norm_mlp_pre[i]), mlp_w_up[i], mlp_w_down[i]), norm_mlp_post[i])
        xs = xs + rmsnorm(sq_relu_mlp(rmsnorm(xs, norm_mlp_pre[i]), mlp_w_up[i], mlp_w_down[i]), norm_mlp_post[i])
    return (xp, xs,
            jnp.stack(fkp), jnp.stack(fvp), jnp.stack(flp),
            jnp.stack(fks), jnp.stack(fvs), jnp.stack(fls),
            jnp.stack(skp), jnp.stack(svp), jnp.stack(sks), jnp.stack(svs),
            jnp.stack(lhp), jnp.stack(lcp), jnp.stack(lhs), jnp.stack(lcs))
```

```python
import functools
import math

import jax
import jax.numpy as jnp
from jax import lax
from jax.experimental import pallas as pl
from jax.experimental.pallas import tpu as pltpu

F32 = jnp.float32
BF16 = jnp.bfloat16

RMS_EPS = 1e-6
WINDOW = 128
ROPE_THETA = 500000.0
LRU_C = 8.0

V7X_VMEM_LIMIT_BYTES = 56 * 1024 * 1024
LANES = 128

NT_DIMS = (((1,), (1,)), ((), ()))


def _params(*sem):
    return pltpu.CompilerParams(dimension_semantics=sem, vmem_limit_bytes=V7X_VMEM_LIMIT_BYTES)


def _rms(x, g):
    ms = jnp.mean(x * x, axis=-1, keepdims=True)
    return x * lax.rsqrt(ms + RMS_EPS) * g


def _expm1(x):
    u = jnp.exp(x)
    um1 = u - 1.0
    safe = jnp.where(u == 1.0, 1.0, jnp.log(u))
    return jnp.where(u == 1.0, x, jnp.where(um1 == -1.0, -1.0, um1 * x / safe))


def _split3(x):
    hi = x.astype(BF16)
    r1 = x - hi.astype(F32)
    mid = r1.astype(BF16)
    lo = (r1 - mid.astype(F32)).astype(BF16)
    return hi, mid, lo


def _norm_mm_kernel(*refs, n_gelu_tiles, has_bias, has_gate):
    it = iter(refs)
    x_ref, g_ref, w_ref = next(it), next(it), next(it)
    b_ref = next(it) if has_bias else None
    wf_ref, bf_ref = (next(it), next(it)) if has_gate else (None, None)
    o_ref = next(it)
    lf_ref = next(it) if has_gate else None
    xn_ref = next(it)
    j = pl.program_id(1)

    @pl.when(j == 0)
    def _():
        xn = _rms(x_ref[...], g_ref[...]).astype(BF16)
        xn_ref[...] = xn
        if has_gate:
            z = jnp.dot(xn, wf_ref[...], preferred_element_type=F32) + bf_ref[...]
            lf_ref[...] = jax.nn.log_sigmoid(z)

    acc = jnp.dot(xn_ref[...], w_ref[...], preferred_element_type=F32)
    if has_bias:
        acc = acc + b_ref[...]
    if n_gelu_tiles == 0:
        o_ref[...] = acc.astype(o_ref.dtype)
    else:
        @pl.when(j < n_gelu_tiles)
        def _():
            o_ref[...] = jax.nn.gelu(acc).astype(o_ref.dtype)

        @pl.when(j >= n_gelu_tiles)
        def _():
            o_ref[...] = acc.astype(o_ref.dtype)


def norm_matmul(x, g, w, b=None, *, gate=None, n_gelu_cols=0, tm, tn):
    m, d = x.shape
    n = w.shape[1]
    assert m % tm == 0 and n % tn == 0 and n_gelu_cols % tn == 0
    has_bias, has_gate = b is not None, gate is not None
    args = [x, g.reshape(1, d), w]
    in_specs = [
        pl.BlockSpec((tm, d), lambda i, j: (i, 0)),
        pl.BlockSpec((1, d), lambda i, j: (0, 0)),
        pl.BlockSpec((d, tn), lambda i, j: (0, j)),
    ]
    if has_bias:
        args.append(b.reshape(1, n))
        in_specs.append(pl.BlockSpec((1, tn), lambda i, j: (0, j)))
    out_shape = [jax.ShapeDtypeStruct((m, n), F32)]
    out_specs = [pl.BlockSpec((tm, tn), lambda i, j: (i, j))]
    if has_gate:
        w_f, b_f = gate
        h = w_f.shape[1]
        args += [w_f, b_f.reshape(1, h)]
        in_specs += [pl.BlockSpec((d, h), lambda i, j: (0, 0)), pl.BlockSpec((1, h), lambda i, j: (0, 0))]
        out_shape.append(jax.ShapeDtypeStruct((m, h), F32))
        out_specs.append(pl.BlockSpec((tm, h), lambda i, j: (i, 0)))
    outs = pl.pallas_call(
        functools.partial(_norm_mm_kernel, n_gelu_tiles=n_gelu_cols // tn, has_bias=has_bias, has_gate=has_gate),
        grid=(m // tm, n // tn),
        in_specs=in_specs,
        out_specs=out_specs,
        out_shape=out_shape,
        scratch_shapes=[pltpu.VMEM((tm, d), BF16)],
        compiler_params=_params("parallel", "arbitrary"),
        name="norm_matmul",
    )(*args)
    return outs if has_gate else outs[0]


def _mm_norm_res_kernel(*refs, has_bias):
    it = iter(refs)
    a_ref, w_ref = next(it), next(it)
    b_ref = next(it) if has_bias else None
    g_ref, x_ref, o_ref = next(it), next(it), next(it)
    mix = jnp.dot(a_ref[...], w_ref[...], preferred_element_type=F32)
    if has_bias:
        mix = mix + b_ref[...]
    o_ref[...] = x_ref[...] + _rms(mix, g_ref[...])


def matmul_norm_residual(a, w, b, g, x, *, tm):
    m, k = a.shape
    d = w.shape[1]
    assert m % tm == 0
    has_bias = b is not None
    args = [a, w]
    in_specs = [pl.BlockSpec((tm, k), lambda i: (i, 0)), pl.BlockSpec((k, d), lambda i: (0, 0))]
    if has_bias:
        args.append(b.reshape(1, d))
        in_specs.append(pl.BlockSpec((1, d), lambda i: (0, 0)))
    args += [g.reshape(1, d), x]
    in_specs += [pl.BlockSpec((1, d), lambda i: (0, 0)), pl.BlockSpec((tm, d), lambda i: (i, 0))]
    return pl.pallas_call(
        functools.partial(_mm_norm_res_kernel, has_bias=has_bias),
        grid=(m // tm,),
        in_specs=in_specs,
        out_specs=pl.BlockSpec((tm, d), lambda i: (i, 0)),
        out_shape=jax.ShapeDtypeStruct((m, d), F32),
        compiler_params=_params("parallel"),
        name="matmul_norm_residual",
    )(*args)


def _mlp_kernel(x_ref, gpre_ref, wup_ref, wdn_ref, gpost_ref, o_ref, xn_ref, acc_ref):
    f = pl.program_id(1)

    @pl.when(f == 0)
    def _():
        xn_ref[...] = _rms(x_ref[...], gpre_ref[...]).astype(BF16)
        acc_ref[...] = jnp.zeros_like(acc_ref)

    h = jnp.dot(xn_ref[...], wup_ref[...], preferred_element_type=F32)
    h = jnp.square(jnp.maximum(h, 0.0)).astype(BF16)
    acc_ref[...] += jnp.dot(h, wdn_ref[...], preferred_element_type=F32)

    @pl.when(f == pl.num_programs(1) - 1)
    def _():
        o_ref[...] = x_ref[...] + _rms(acc_ref[...], gpost_ref[...])


def mlp_sublayer(x, g_pre, w_up, w_down, g_post, *, tm, tf):
    m, d = x.shape
    dff = w_up.shape[1]
    assert m % tm == 0 and dff % tf == 0
    return pl.pallas_call(
        _mlp_kernel,
        grid=(m // tm, dff // tf),
        in_specs=[
            pl.BlockSpec((tm, d), lambda i, f: (i, 0)),
            pl.BlockSpec((1, d), lambda i, f: (0, 0)),
            pl.BlockSpec((d, tf), lambda i, f: (0, f)),
            pl.BlockSpec((tf, d), lambda i, f: (f, 0)),
            pl.BlockSpec((1, d), lambda i, f: (0, 0)),
        ],
        out_specs=pl.BlockSpec((tm, d), lambda i, f: (i, 0)),
        out_shape=jax.ShapeDtypeStruct((m, d), F32),
        scratch_shapes=[pltpu.VMEM((tm, d), BF16), pltpu.VMEM((tm, d), F32)],
        compiler_params=_params("parallel", "arbitrary"),
        name="mlp_sublayer",
    )(x, g_pre.reshape(1, d), w_up, w_down, g_post.reshape(1, d))


def _fox_cumsum_kernel(lf_ref, ccol_ref, crow_ref, *, chunk, n_kv, n_g):
    s, h = lf_ref.shape[1], lf_ref.shape[2]
    row = lax.broadcasted_iota(jnp.int32, (chunk, chunk), 0)
    col = lax.broadcasted_iota(jnp.int32, (chunk, chunk), 1)
    lower = (row >= col).astype(BF16)
    carry = jnp.zeros((1, h), F32)
    for c in range(s // chunk):
        x = lf_ref[0, c * chunk:(c + 1) * chunk, :]
        hi, mid, lo = _split3(x)
        cs = (jnp.dot(lower, hi, preferred_element_type=F32)
              + jnp.dot(lower, mid, preferred_element_type=F32)
              + jnp.dot(lower, lo, preferred_element_type=F32)) + carry
        carry = cs[chunk - 1:chunk, :]
        cs_t = jnp.concatenate([cs, jnp.zeros((chunk, LANES - h), F32)], axis=1).T
        for kv in range(n_kv):
            ccol_ref[0, kv, c * chunk:(c + 1) * chunk, :] = cs[:, kv * n_g:(kv + 1) * n_g]
            crow_ref[0, kv, c] = cs_t[kv * n_g:(kv + 1) * n_g, :]


def fox_cumsum(logf, *, n_kv, chunk):
    b, s, h = logf.shape
    n_g = h // n_kv
    return pl.pallas_call(
        functools.partial(_fox_cumsum_kernel, chunk=chunk, n_kv=n_kv, n_g=n_g),
        grid=(b,),
        in_specs=[pl.BlockSpec((1, s, h), lambda i: (i, 0, 0))],
        out_specs=[
            pl.BlockSpec((1, n_kv, s, n_g), lambda i: (i, 0, 0, 0)),
            pl.BlockSpec((1, n_kv, s // chunk, n_g, chunk), lambda i: (i, 0, 0, 0, 0)),
        ],
        out_shape=[
            jax.ShapeDtypeStruct((b, n_kv, s, n_g), F32),
            jax.ShapeDtypeStruct((b, n_kv, s // chunk, n_g, chunk), F32),
        ],
        compiler_params=_params("parallel"),
        name="fox_cumsum",
    )(logf)


def _fox_flash_kernel(q_ref, k_ref, v_ref, cq_ref, ck_ref, o_ref, m_ref, l_ref, acc_ref, *, blk, n_g, hd):
    qi = pl.program_id(2)
    scale = hd ** -0.5
    qs = [q_ref[:, g * hd:(g + 1) * hd].astype(BF16) for g in range(n_g)]
    cqs = [cq_ref[0, 0, :, g:g + 1] for g in range(n_g)]
    m_ref[...] = jnp.full_like(m_ref, -jnp.inf)
    l_ref[...] = jnp.zeros_like(l_ref)
    acc_ref[...] = jnp.zeros_like(acc_ref)

    def step(ki, masked):
        start = pl.multiple_of(ki * blk, blk)
        k = k_ref[pl.ds(start, blk), :].astype(BF16)
        v = v_ref[pl.ds(start, blk), :].astype(BF16)
        ck = ck_ref[0, 0, ki]
        if masked:
            row = lax.broadcasted_iota(jnp.int32, (blk, blk), 0)
            col = lax.broadcasted_iota(jnp.int32, (blk, blk), 1)
            causal = row >= col
        for g in range(n_g):
            s = lax.dot_general(qs[g], k, NT_DIMS, preferred_element_type=F32) * scale
            s = s + (cqs[g] - ck[g:g + 1, :])
            if masked:
                s = jnp.where(causal, s, -jnp.inf)
            m_old = m_ref[g]
            m_new = jnp.maximum(m_old, jnp.max(s, axis=1, keepdims=True))
            alpha = jnp.exp(m_old - m_new)
            p = jnp.exp(s - m_new)
            l_ref[g] = alpha * l_ref[g] + jnp.sum(p, axis=1, keepdims=True)
            acc_ref[g] = alpha * acc_ref[g] + jnp.dot(p.astype(BF16), v, preferred_element_type=F32)
            m_ref[g] = m_new

    def body(ki, carry):
        step(ki, False)
        return carry

    lax.fori_loop(0, qi, body, 0)
    step(qi, True)
    for g in range(n_g):
        o_ref[:, g * hd:(g + 1) * hd] = (acc_ref[g] / l_ref[g]).astype(o_ref.dtype)


def fox_flash(qkv, ccol, crow, *, batch, seq, n_heads, n_kv, hd, blk):
    n_g = n_heads // n_kv
    nq = seq // blk
    gw = n_g * hd
    return pl.pallas_call(
        functools.partial(_fox_flash_kernel, blk=blk, n_g=n_g, hd=hd),
        grid=(batch, n_kv, nq),
        in_specs=[
            pl.BlockSpec((blk, gw), lambda b, kv, qi: (b * nq + qi, kv)),
            pl.BlockSpec((seq, hd), lambda b, kv, qi: (b, n_heads + kv)),
            pl.BlockSpec((seq, hd), lambda b, kv, qi: (b, n_heads + n_kv + kv)),
            pl.BlockSpec((1, 1, blk, n_g), lambda b, kv, qi: (b, kv, qi, 0)),
            pl.BlockSpec((1, 1, nq, n_g, blk), lambda b, kv, qi: (b, kv, 0, 0, 0)),
        ],
        out_specs=pl.BlockSpec((blk, gw), lambda b, kv, qi: (b * nq + qi, kv)),
        out_shape=jax.ShapeDtypeStruct((batch * seq, n_heads * hd), BF16),
        scratch_shapes=[
            pltpu.VMEM((n_g, blk, 1), F32),
            pltpu.VMEM((n_g, blk, 1), F32),
            pltpu.VMEM((n_g, blk, hd), F32),
        ],
        compiler_params=_params("parallel", "parallel", "arbitrary"),
        name="fox_flash",
    )(qkv, qkv, qkv, ccol, crow)


def _fox_decode_kernel(pt_ref, q_ref, knew_ref, vnew_ref, lfnew_ref, *rest, pages, n_kv, n_heads, hd, n_t):
    del pt_ref
    k_refs, v_refs, lf_refs = rest[:pages], rest[pages:2 * pages], rest[2 * pages:3 * pages]
    o_ref, qbd_ref, cq_ref, m_ref, l_ref, acc_ref, carry_ref, pad_ref, kpad_ref, vpad_ref = rest[3 * pages:]
    c = pl.program_id(1)
    scale = hd ** -0.5
    page = k_refs[0].shape[2]
    n_rows = n_t * n_heads
    n_g = n_heads // n_kv
    width = n_kv * hd
    jj = lax.broadcasted_iota(jnp.int32, (page, page), 0)
    ss = lax.broadcasted_iota(jnp.int32, (page, page), 1)

    def lane_sums(x_t, tri):
        hi, mid, lo = _split3(x_t)
        out = jnp.dot(jnp.concatenate([hi, mid, lo], axis=0), tri, preferred_element_type=F32)
        return out[:n_heads] + out[n_heads:2 * n_heads] + out[2 * n_heads:]

    def rows4(x):
        return jnp.concatenate([x] * n_t, axis=0)

    @pl.when(c == 0)
    def _():
        q = q_ref[0]
        qt = jnp.concatenate([q] * n_kv, axis=1)
        row_kv = (lax.broadcasted_iota(jnp.int32, (n_rows, width), 0) % n_heads) // n_g
        lane_kv = lax.broadcasted_iota(jnp.int32, (n_rows, width), 1) // hd
        qbd = jnp.where(row_kv == lane_kv, qt, 0.0).astype(BF16)
        qbd_ref[...] = qbd
        pad_ref[...] = jnp.zeros_like(pad_ref)
        pad_ref[0:n_t, 0:n_heads] = lfnew_ref[0]
        lf_t = pad_ref[...].T[:n_heads, :]
        c_new = lane_sums(lf_t, (jj <= ss).astype(BF16))
        cq = jnp.concatenate([c_new[:, t:t + 1] for t in range(n_t)], axis=0)
        cq_ref[...] = cq
        kpad_ref[...] = jnp.zeros_like(kpad_ref)
        vpad_ref[...] = jnp.zeros_like(vpad_ref)
        kpad_ref[0:n_t, :] = knew_ref[0]
        vpad_ref[0:n_t, :] = vnew_ref[0]
        s = lax.dot_general(qbd, kpad_ref[...].astype(BF16), NT_DIMS, preferred_element_type=F32) * scale
        s = s + (cq - rows4(c_new))
        tok = lax.broadcasted_iota(jnp.int32, (n_rows, page), 0) // n_heads
        key = lax.broadcasted_iota(jnp.int32, (n_rows, page), 1)
        s = jnp.where(key <= tok, s, -jnp.inf)
        m = jnp.max(s, axis=1, keepdims=True)
        p = jnp.exp(s - m)
        m_ref[...] = m
        l_ref[...] = jnp.sum(p, axis=1, keepdims=True)
        acc_ref[...] = jnp.dot(p.astype(BF16), vpad_ref[...].astype(BF16), preferred_element_type=F32)
        carry_ref[...] = jnp.zeros_like(carry_ref)

    qbd = qbd_ref[...]
    cq = cq_ref[...]
    strict = (jj > ss).astype(BF16)
    carry = carry_ref[...]
    logits = []
    for i in range(pages):
        lf = lf_refs[i][0, 0]
        lf_t = jnp.concatenate([lf, jnp.zeros((page, LANES - n_heads), F32)], axis=1).T[:n_heads, :]
        r = lane_sums(lf_t, strict) + carry
        carry = carry + jnp.sum(lf_t, axis=1, keepdims=True)
        s = lax.dot_general(qbd, k_refs[i][0, 0].astype(BF16), NT_DIMS, preferred_element_type=F32) * scale
        logits.append(s + (cq + rows4(r)))
    carry_ref[...] = carry
    m_old = m_ref[...]
    m_new = m_old
    for s in logits:
        m_new = jnp.maximum(m_new, jnp.max(s, axis=1, keepdims=True))
    alpha = jnp.exp(m_old - m_new)
    l_new = alpha * l_ref[...]
    acc = alpha * acc_ref[...]
    for i, s in enumerate(logits):
        p = jnp.exp(s - m_new)
        l_new = l_new + jnp.sum(p, axis=1, keepdims=True)
        acc = acc + jnp.dot(p.astype(BF16), v_refs[i][0, 0].astype(BF16), preferred_element_type=F32)
    m_ref[...] = m_new
    l_ref[...] = l_new
    acc_ref[...] = acc

    @pl.when(c == pl.num_programs(1) - 1)
    def _():
        out = acc / l_new
        row_kv = (lax.broadcasted_iota(jnp.int32, (n_rows, hd), 0) % n_heads) // n_g
        res = jnp.zeros((n_rows, hd), F32)
        for kv in range(n_kv):
            res = jnp.where(row_kv == kv, out[:, kv * hd:(kv + 1) * hd], res)
        o_ref[0] = res


def fox_decode(q64, qkv_s, lf_s, cache_k, cache_v, cache_lf, page_table, layer, *, n_heads, n_kv, hd, pages):
    bsz, n_t = qkv_s.shape[0], qkv_s.shape[1]
    n_pages = page_table.shape[1]
    page = cache_k.shape[2]
    width = n_kv * hd
    n_rows = n_t * n_heads
    assert n_pages % pages == 0 and page == LANES
    n_chunks = n_pages // pages

    def page_map(i):
        return lambda b, c, pt: (layer, pt[b, n_pages - 1 - (c * pages + i)], 0, 0)

    kv_col = n_heads * hd // width
    in_specs = [
        pl.BlockSpec((1, n_rows, hd), lambda b, c, pt: (b, 0, 0)),
        pl.BlockSpec((1, n_t, width), lambda b, c, pt: (b, 0, kv_col)),
        pl.BlockSpec((1, n_t, width), lambda b, c, pt: (b, 0, kv_col + 1)),
        pl.BlockSpec((1, n_t, n_heads), lambda b, c, pt: (b, 0, 0)),
    ]
    in_specs += [pl.BlockSpec((1, 1, page, width), page_map(i)) for i in range(pages)]
    in_specs += [pl.BlockSpec((1, 1, page, width), page_map(i)) for i in range(pages)]
    in_specs += [pl.BlockSpec((1, 1, page, n_heads), page_map(i)) for i in range(pages)]
    grid_spec = pltpu.PrefetchScalarGridSpec(
        num_scalar_prefetch=1,
        grid=(bsz, n_chunks),
        in_specs=in_specs,
        out_specs=pl.BlockSpec((1, n_rows, hd), lambda b, c, pt: (b, 0, 0)),
        scratch_shapes=[
            pltpu.VMEM((n_rows, width), BF16),
            pltpu.VMEM((n_rows, 1), F32),
            pltpu.VMEM((n_rows, 1), F32),
            pltpu.VMEM((n_rows, 1), F32),
            pltpu.VMEM((n_rows, width), F32),
            pltpu.VMEM((n_heads, 1), F32),
            pltpu.VMEM((page, LANES), F32),
            pltpu.VMEM((page, width), F32),
            pltpu.VMEM((page, width), F32),
        ],
    )
    return pl.pallas_call(
        functools.partial(_fox_decode_kernel, pages=pages, n_kv=n_kv, n_heads=n_heads, hd=hd, n_t=n_t),
        grid_spec=grid_spec,
        out_shape=jax.ShapeDtypeStruct((bsz, n_rows, hd), F32),
        compiler_params=_params("parallel", "arbitrary"),
        name="fox_decode",
    )(page_table, q64, qkv_s, qkv_s, lf_s, *([cache_k] * pages), *([cache_v] * pages), *([cache_lf] * pages))


def _rope_tables(pos, hd, reps):
    rot = hd // 4
    half = rot // 2
    inv_freq = ROPE_THETA ** (-jnp.arange(half, dtype=F32) / half)
    ang = pos.astype(F32)[:, None] * inv_freq[None, :]
    cos, sin = jnp.cos(ang), jnp.sin(ang)
    n = pos.shape[0]
    ones = jnp.ones((n, hd - rot), F32)
    zeros_h = jnp.zeros((n, half), F32)
    zeros_r = jnp.zeros((n, hd - rot), F32)
    c = jnp.concatenate([cos, cos, ones], axis=1)
    s_up = jnp.concatenate([-sin, zeros_h, zeros_r], axis=1)
    s_dn = jnp.concatenate([zeros_h, sin, zeros_r], axis=1)
    return tuple(jnp.tile(t, (1, reps)) for t in (c, s_up, s_dn))


def _rope(x, c, s_up, s_dn, half):
    w = x.shape[1]
    pieces = []
    for j in range(w // LANES):
        xj = x[:, j * LANES:(j + 1) * LANES]
        up = pltpu.roll(xj, LANES - half, 1)
        dn = pltpu.roll(xj, half, 1)
        pieces.append(xj * c + up * s_up + dn * s_dn)
    return pieces[0] if len(pieces) == 1 else jnp.concatenate(pieces, axis=1)


def _swa_prompt_kernel(q_ref, kc_ref, kp_ref, vc_ref, vp_ref, cc_ref, su_ref, sd_ref, pc_ref, pu_ref, pd_ref,
                       sink_ref, o_ref, klast_ref, *, n_heads, n_kv, hd, blk):
    qb = pl.program_id(1)
    scale = hd ** -0.5
    half = hd // 8
    n_g = n_heads // n_kv
    q = _rope(q_ref[...], cc_ref[...], su_ref[...], sd_ref[...], half)
    kc = _rope(kc_ref[...], cc_ref[...], su_ref[...], sd_ref[...], half)
    kp = _rope(kp_ref[...], pc_ref[...], pu_ref[...], pd_ref[...], half)
    k = jnp.concatenate([kp, kc], axis=0).astype(BF16)
    v = jnp.concatenate([vp_ref[...], vc_ref[...]], axis=0).astype(BF16)
    qbf = q.astype(BF16)
    t = lax.broadcasted_iota(jnp.int32, (blk, 2 * blk), 0)
    j = lax.broadcasted_iota(jnp.int32, (blk, 2 * blk), 1)
    first_key = jnp.where(qb > 0, 0, blk)
    band = (j > t) & (j <= t + blk) & (j >= first_key)
    for h in range(n_heads):
        kv = h // n_g
        s = lax.dot_general(qbf[:, h * hd:(h + 1) * hd], k[:, kv * hd:(kv + 1) * hd], NT_DIMS,
                            preferred_element_type=F32) * scale
        s = jnp.where(band, s, -jnp.inf)
        sink = sink_ref[0:1, h:h + 1]
        m = jnp.maximum(jnp.max(s, axis=1, keepdims=True), sink)
        e = jnp.exp(s - m)
        p = e / (jnp.sum(e, axis=1, keepdims=True) + jnp.exp(sink - m))
        o_ref[:, h * hd:(h + 1) * hd] = jnp.dot(p.astype(BF16), v[:, kv * hd:(kv + 1) * hd],
                                                  preferred_element_type=F32).astype(o_ref.dtype)

    @pl.when(qb == pl.num_programs(1) - 1)
    def _():
        klast_ref[0] = kc


def swa_prompt(qkv, sinks, *, batch, seq, n_heads, n_kv, hd, blk):
    assert blk == WINDOW and (n_kv * hd) % LANES == 0
    nb = seq // blk
    qw, kw = n_heads * hd, n_kv * hd
    kcol = qw // kw
    pos = jnp.arange(seq)
    reps = LANES // hd
    cur = _rope_tables(pos, hd, reps)
    prev = _rope_tables(pos - blk, hd, reps)

    def cur_row(b, qb):
        return b * nb + qb

    def prev_row(b, qb):
        return b * nb + jnp.maximum(qb - 1, 0)

    tab_cur = pl.BlockSpec((blk, LANES), lambda b, qb: (qb, 0))
    tab_prev = pl.BlockSpec((blk, LANES), lambda b, qb: (qb, 0))
    return pl.pallas_call(
        functools.partial(_swa_prompt_kernel, n_heads=n_heads, n_kv=n_kv, hd=hd, blk=blk),
        grid=(batch, nb),
        in_specs=[
            pl.BlockSpec((blk, qw), lambda b, qb: (cur_row(b, qb), 0)),
            pl.BlockSpec((blk, kw), lambda b, qb: (cur_row(b, qb), kcol)),
            pl.BlockSpec((blk, kw), lambda b, qb: (prev_row(b, qb), kcol)),
            pl.BlockSpec((blk, kw), lambda b, qb: (cur_row(b, qb), kcol + 1)),
            pl.BlockSpec((blk, kw), lambda b, qb: (prev_row(b, qb), kcol + 1)),
            tab_cur, tab_cur, tab_cur, tab_prev, tab_prev, tab_prev,
            pl.BlockSpec((1, n_heads), lambda b, qb: (0, 0)),
        ],
        out_specs=[
            pl.BlockSpec((blk, qw), lambda b, qb: (cur_row(b, qb), 0)),
            pl.BlockSpec((1, blk, kw), lambda b, qb: (b, 0, 0)),
        ],
        out_shape=[
            jax.ShapeDtypeStruct((batch * seq, qw), BF16),
            jax.ShapeDtypeStruct((batch, blk, kw), F32),
        ],
        compiler_params=_params("parallel", "arbitrary"),
        name="swa_prompt",
    )(qkv, qkv, qkv, qkv, qkv, *cur, *prev, sinks.reshape(1, n_heads))


def _swa_sample_kernel(q_ref, knew_ref, vnew_ref, bk_ref, bv_ref, qc_ref, qu_ref, qd_ref, kc_ref, ku_ref, kd_ref,
                       sink_ref, o_ref, ok_ref, ov_ref, kall_ref, vall_ref, *, n_heads, n_kv, hd, n_t):
    scale = hd ** -0.5
    half = hd // 8
    n_g = n_heads // n_kv
    n_rows = n_t * n_heads
    nbuf = bk_ref.shape[1]
    width = n_kv * hd
    n_all = kall_ref.shape[0]
    knew = _rope(knew_ref[0], kc_ref[...], ku_ref[...], kd_ref[...], half)
    kall_ref[...] = jnp.zeros_like(kall_ref)
    vall_ref[...] = jnp.zeros_like(vall_ref)
    kall_ref[0:nbuf, :] = bk_ref[0]
    vall_ref[0:nbuf, :] = bv_ref[0]
    kall_ref[nbuf:nbuf + n_t, :] = knew
    vall_ref[nbuf:nbuf + n_t, :] = vnew_ref[0]
    ok_ref[0] = kall_ref[n_t:n_t + nbuf, :]
    ov_ref[0] = vall_ref[n_t:n_t + nbuf, :]
    q = q_ref[0]
    up = jnp.concatenate([q[:, half:], q[:, :half]], axis=1)
    dn = jnp.concatenate([q[:, hd - half:], q[:, :hd - half]], axis=1)
    q = q * qc_ref[...] + up * qu_ref[...] + dn * qd_ref[...]
    qt = jnp.concatenate([q] * n_kv, axis=1)
    row_kv = (lax.broadcasted_iota(jnp.int32, (n_rows, width), 0) % n_heads) // n_g
    lane_kv = lax.broadcasted_iota(jnp.int32, (n_rows, width), 1) // hd
    qbd = jnp.where(row_kv == lane_kv, qt, 0.0).astype(BF16)
    s = lax.dot_general(qbd, kall_ref[...].astype(BF16), NT_DIMS, preferred_element_type=F32) * scale
    tok = lax.broadcasted_iota(jnp.int32, (n_rows, n_all), 0) // n_heads
    key = lax.broadcasted_iota(jnp.int32, (n_rows, n_all), 1)
    band = ((key < nbuf) & (tok + nbuf - key < WINDOW)) | ((key >= nbuf) & (key - nbuf <= tok))
    s = jnp.where(band, s, -jnp.inf)
    sink = sink_ref[...]
    m = jnp.maximum(jnp.max(s, axis=1, keepdims=True), sink)
    e = jnp.exp(s - m)
    p = e / (jnp.sum(e, axis=1, keepdims=True) + jnp.exp(sink - m))
    out = jnp.dot(p.astype(BF16), vall_ref[...].astype(BF16), preferred_element_type=F32)
    row_kv = (lax.broadcasted_iota(jnp.int32, (n_rows, hd), 0) % n_heads) // n_g
    res = jnp.zeros((n_rows, hd), F32)
    for kv in range(n_kv):
        res = jnp.where(row_kv == kv, out[:, kv * hd:(kv + 1) * hd], res)
    o_ref[0] = res


def swa_sample(q_rows, qkv_s, buf_k, buf_v, sinks, past, *, n_heads, n_kv, hd):
    bsz, n_t = qkv_s.shape[0], qkv_s.shape[1]
    nbuf = buf_k.shape[1]
    assert nbuf == WINDOW and n_t <= 8
    width = n_kv * hd
    n_rows = n_t * n_heads
    n_all = nbuf + LANES
    pos = past + jnp.arange(n_t)
    qtab = tuple(jnp.repeat(t, n_heads, axis=0) for t in _rope_tables(pos, hd, 1))
    ktab = _rope_tables(pos, hd, LANES // hd)
    sink_rows = jnp.tile(sinks.astype(F32), n_t).reshape(n_rows, 1)
    kcol = n_heads * hd // width
    full2 = lambda shape: pl.BlockSpec(shape, lambda b: (0, 0))
    return pl.pallas_call(
        functools.partial(_swa_sample_kernel, n_heads=n_heads, n_kv=n_kv, hd=hd, n_t=n_t),
        grid=(bsz,),
        in_specs=[
            pl.BlockSpec((1, n_rows, hd), lambda b: (b, 0, 0)),
            pl.BlockSpec((1, n_t, width), lambda b: (b, 0, kcol)),
            pl.BlockSpec((1, n_t, width), lambda b: (b, 0, kcol + 1)),
            pl.BlockSpec((1, nbuf, width), lambda b: (b, 0, 0)),
            pl.BlockSpec((1, nbuf, width), lambda b: (b, 0, 0)),
            full2((n_rows, hd)), full2((n_rows, hd)), full2((n_rows, hd)),
            full2((n_t, LANES)), full2((n_t, LANES)), full2((n_t, LANES)),
            full2((n_rows, 1)),
        ],
        out_specs=[
            pl.BlockSpec((1, n_rows, hd), lambda b: (b, 0, 0)),
            pl.BlockSpec((1, nbuf, width), lambda b: (b, 0, 0)),
            pl.BlockSpec((1, nbuf, width), lambda b: (b, 0, 0)),
        ],
        out_shape=[
            jax.ShapeDtypeStruct((bsz, n_rows, hd), F32),
            jax.ShapeDtypeStruct((bsz, nbuf, width), F32),
            jax.ShapeDtypeStruct((bsz, nbuf, width), F32),
        ],
        scratch_shapes=[pltpu.VMEM((n_all, width), F32), pltpu.VMEM((n_all, width), F32)],
        compiler_params=_params("parallel"),
        name="swa_sample",
    )(q_rows, qkv_s, qkv_s, buf_k, buf_v, *qtab, *ktab, sink_rows)


def _rglru_kernel(gate_ref, u_ref, cb_ref, h0_ref, cw_ref, cbias_ref, wa_ref, ba_ref, wx_ref, bx_ref, lam_ref,
                  y_ref, hlast_ref, cout_ref, uext_ref, a_ref, d_ref, *, n_t, n_seq, chunk):
    rows = n_t * n_seq
    tail = (cw_ref.shape[0] - 1) * n_seq
    head = uext_ref.shape[0] - rows
    uext_ref[head - tail:head, :] = cb_ref[0]
    uext_ref[head:head + rows, :] = u_ref[...]
    cout_ref[0] = uext_ref[head + rows - tail:head + rows, :]
    log_sig_lam = jax.nn.log_sigmoid(lam_ref[...])
    wa = wa_ref[0]
    wx = wx_ref[0]
    for c0 in range(0, rows, chunk):
        n = min(chunk, rows - c0)
        xc = cbias_ref[...]
        for i in range(cw_ref.shape[0]):
            xc = xc + uext_ref[head - tail + i * n_seq + c0:head - tail + i * n_seq + c0 + n, :] * cw_ref[i:i + 1, :]
        xb = xc.astype(BF16)
        r = jax.nn.sigmoid(jnp.dot(xb, wa, preferred_element_type=F32) + ba_ref[...])
        ig = jax.nn.sigmoid(jnp.dot(xb, wx, preferred_element_type=F32) + bx_ref[...])
        log_a = LRU_C * r * log_sig_lam
        a_ref[c0:c0 + n, :] = jnp.exp(log_a)
        d_ref[c0:c0 + n, :] = jnp.sqrt(-_expm1(2.0 * log_a)) * (ig * xc)

    def step(t, h):
        sl = pl.ds(pl.multiple_of(t * n_seq, n_seq), n_seq)
        h = a_ref[sl, :] * h + d_ref[sl, :]
        d_ref[sl, :] = h
        return h

    h_last = lax.fori_loop(0, n_t, step, h0_ref[0], unroll=min(8, n_t))
    hlast_ref[0] = h_last
    for c0 in range(0, rows, chunk):
        n = min(chunk, rows - c0)
        y_ref[c0:c0 + n, :] = (d_ref[c0:c0 + n, :] * gate_ref[c0:c0 + n, :]).astype(y_ref.dtype)


def rglru(gu, row0, conv_buf, h0, conv_w, conv_b, w_a, b_a, w_x, b_x, lam, *, n_groups, n_t, n_seq, d_rnn):
    n_blocks, cb = w_a.shape[0], w_a.shape[1]
    rows = n_t * n_seq
    tail = conv_buf.shape[1]
    head = -(-tail // 8) * 8
    chunk = min(rows, 256)
    vec = lambda a: a.reshape(1, d_rnn)
    col = lambda g, n: (0, n)
    return pl.pallas_call(
        functools.partial(_rglru_kernel, n_t=n_t, n_seq=n_seq, chunk=chunk),
        grid=(n_groups, n_blocks),
        in_specs=[
            pl.BlockSpec((rows, cb), lambda g, n: (row0 + g, n)),
            pl.BlockSpec((rows, cb), lambda g, n: (row0 + g, n_blocks + n)),
            pl.BlockSpec((1, tail, cb), lambda g, n: (g, 0, n)),
            pl.BlockSpec((1, n_seq, cb), lambda g, n: (g, 0, n)),
            pl.BlockSpec((conv_w.shape[0], cb), col),
            pl.BlockSpec((1, cb), col),
            pl.BlockSpec((1, cb, cb), lambda g, n: (n, 0, 0)),
            pl.BlockSpec((1, cb), col),
            pl.BlockSpec((1, cb, cb), lambda g, n: (n, 0, 0)),
            pl.BlockSpec((1, cb), col),
            pl.BlockSpec((1, cb), col),
        ],
        out_specs=[
            pl.BlockSpec((rows, cb), lambda g, n: (g, n)),
            pl.BlockSpec((1, n_seq, cb), lambda g, n: (g, 0, n)),
            pl.BlockSpec((1, tail, cb), lambda g, n: (g, 0, n)),
        ],
        out_shape=[
            jax.ShapeDtypeStruct((n_groups * rows, d_rnn), BF16),
            jax.ShapeDtypeStruct((n_groups, n_seq, d_rnn), F32),
            jax.ShapeDtypeStruct((n_groups, tail, d_rnn), F32),
        ],
        scratch_shapes=[
            pltpu.VMEM((head + rows, cb), F32),
            pltpu.VMEM((rows, cb), F32),
            pltpu.VMEM((rows, cb), F32),
        ],
        compiler_params=_params("parallel", "parallel"),
        name="rglru",
    )(gu, gu, conv_buf, h0, conv_w, vec(conv_b), w_a, vec(b_a), w_x, vec(b_x), vec(lam))


TM = 640
TN = 512
TF = 512
FOX_BLK = 256
FOX_PAGES = 8


def kernel(x_prompt, x_sample, cache_fox_k, cache_fox_v, cache_fox_logf, cache_swa_k, cache_swa_v, state_lru_h, state_lru_conv, page_table, norm_mix_pre, norm_mix_post, norm_mlp_pre, norm_mlp_post, mlp_w_up, mlp_w_down, fox_w_qkv, fox_w_f, fox_b_f, fox_w_o, swa_w_qkv, swa_b_qkv, swa_sinks, swa_w_o, swa_b_o, lru_w_gate, lru_b_gate, lru_w_in, lru_b_in, lru_conv_w, lru_conv_b, lru_w_a, lru_b_a, lru_w_x, lru_b_x, lru_lambda, lru_w_out, lru_b_out):
    bp, seq, d = x_prompt.shape
    bs, n_t, _ = x_sample.shape
    mp, ms = bp * seq, bs * n_t
    depth = norm_mix_pre.shape[0]
    fox_heads = fox_w_f.shape[2]
    fox_kv, fox_hd = cache_fox_k.shape[3], cache_fox_k.shape[4]
    swa_heads = swa_sinks.shape[1]
    swa_kv, swa_hd = cache_swa_k.shape[3], cache_swa_k.shape[4]
    d_rnn = lru_w_gate.shape[2]
    conv_w = lru_conv_w.shape[1]
    past = page_table.shape[1] * cache_fox_k.shape[2]
    fox_qw, fox_kw = fox_heads * fox_hd, fox_kv * fox_hd
    swa_qw, swa_kw = swa_heads * swa_hd, swa_kv * swa_hd

    x = jnp.concatenate([x_prompt.reshape(mp, d), x_sample.reshape(ms, d)], axis=0)
    ck = cache_fox_k.reshape(cache_fox_k.shape[:3] + (fox_kw,))
    cv = cache_fox_v.reshape(cache_fox_v.shape[:3] + (fox_kw,))

    fkp, fvp, flp, fks, fvs, fls = [], [], [], [], [], []
    skp, svp, sks, svs = [], [], [], []
    lhp, lcp, lhs, lcs = [], [], [], []
    for i in range(depth):
        kind, j = i % 3, i // 3
        if kind == 0:
            qkv, logf = norm_matmul(x, norm_mix_pre[i], fox_w_qkv[j].astype(BF16),
                                    gate=(fox_w_f[j].astype(BF16), fox_b_f[j]), tm=TM, tn=TN)
            kp = qkv[:mp, fox_qw:fox_qw + fox_kw]
            vp = qkv[:mp, fox_qw + fox_kw:]
            fkp.append(kp.reshape(bp, seq, fox_kv, fox_hd))
            fvp.append(vp.reshape(bp, seq, fox_kv, fox_hd))
            flp.append(logf[:mp].reshape(bp, seq, fox_heads))
            qkv_s = qkv[mp:].reshape(bs, n_t, -1)
            lf_s = logf[mp:].reshape(bs, n_t, fox_heads)
            fks.append(qkv_s[:, :, fox_qw:fox_qw + fox_kw].reshape(bs, n_t, fox_kv, fox_hd))
            fvs.append(qkv_s[:, :, fox_qw + fox_kw:].reshape(bs, n_t, fox_kv, fox_hd))
            fls.append(lf_s)
            ccol, crow = fox_cumsum(logf[:mp].reshape(bp, seq, fox_heads), n_kv=fox_kv, chunk=FOX_BLK)
            att_p = fox_flash(qkv, ccol, crow, batch=bp, seq=seq, n_heads=fox_heads, n_kv=fox_kv, hd=fox_hd,
                              blk=FOX_BLK)
            q64 = qkv_s[:, :, :fox_qw].reshape(bs, n_t * fox_heads, fox_hd)
            att_s = fox_decode(q64, qkv_s, lf_s, ck, cv, cache_fox_logf, page_table, j,
                               n_heads=fox_heads, n_kv=fox_kv, hd=fox_hd, pages=FOX_PAGES)
            att = jnp.concatenate([att_p, att_s.reshape(ms, fox_qw).astype(BF16)], axis=0)
            x = matmul_norm_residual(att, fox_w_o[j].astype(BF16), None, norm_mix_post[i], x, tm=TM)
        elif kind == 1:
            qkv = norm_matmul(x, norm_mix_pre[i], swa_w_qkv[j].astype(BF16), swa_b_qkv[j], tm=TM, tn=TN)
            att_p, k_last = swa_prompt(qkv, swa_sinks[j], batch=bp, seq=seq, n_heads=swa_heads, n_kv=swa_kv,
                                       hd=swa_hd, blk=WINDOW)
            keep = min(WINDOW, seq)
            skp.append(k_last.reshape(bp, keep, swa_kv, swa_hd))
            svp.append(qkv[:mp, swa_qw + swa_kw:].reshape(bp, seq, swa_kv, swa_hd)[:, seq - keep:])
            qkv_s = qkv[mp:].reshape(bs, n_t, -1)
            q_rows = qkv_s[:, :, :swa_qw].reshape(bs, n_t * swa_heads, swa_hd)
            nbuf = cache_swa_k.shape[2]
            att_s, nk, nv = swa_sample(q_rows, qkv_s, cache_swa_k[j].reshape(bs, nbuf, swa_kw),
                                       cache_swa_v[j].reshape(bs, nbuf, swa_kw), swa_sinks[j], past,
                                       n_heads=swa_heads, n_kv=swa_kv, hd=swa_hd)
            sks.append(nk.reshape(bs, nbuf, swa_kv, swa_hd))
            svs.append(nv.reshape(bs, nbuf, swa_kv, swa_hd))
            att = jnp.concatenate([att_p, att_s.reshape(ms, swa_qw).astype(BF16)], axis=0)
            x = matmul_norm_residual(att, swa_w_o[j].astype(BF16), swa_b_o[j], norm_mix_post[i], x, tm=TM)
        else:
            w_gu = jnp.concatenate([lru_w_gate[j], lru_w_in[j]], axis=1).astype(BF16)
            b_gu = jnp.concatenate([lru_b_gate[j], lru_b_in[j]])
            gu = norm_matmul(x, norm_mix_pre[i], w_gu, b_gu, n_gelu_cols=d_rnn, tm=TM, tn=TN)
            lw = (lru_conv_w[j], lru_conv_b[j], lru_w_a[j].astype(BF16), lru_b_a[j], lru_w_x[j].astype(BF16),
                  lru_b_x[j], lru_lambda[j])
            y_p, h_p, c_p = rglru(gu, 0, jnp.zeros((bp, conv_w - 1, d_rnn), F32), jnp.zeros((bp, 1, d_rnn), F32),
                                  *lw, n_groups=bp, n_t=seq, n_seq=1, d_rnn=d_rnn)
            lhp.append(h_p.reshape(bp, d_rnn))
            lcp.append(c_p)
            gu_s = gu[mp:].reshape(bs, n_t, -1).swapaxes(0, 1).reshape(ms, -1)
            cb_s = state_lru_conv[j].swapaxes(0, 1).reshape(1, (conv_w - 1) * bs, d_rnn)
            y_s, h_s, c_s = rglru(gu_s, 0, cb_s, state_lru_h[j].reshape(1, bs, d_rnn), *lw,
                                  n_groups=1, n_t=n_t, n_seq=bs, d_rnn=d_rnn)
            lhs.append(h_s.reshape(bs, d_rnn))
            lcs.append(c_s.reshape(conv_w - 1, bs, d_rnn).swapaxes(0, 1))
            y_s = y_s.reshape(n_t, bs, d_rnn).swapaxes(0, 1).reshape(ms, d_rnn)
            y = jnp.concatenate([y_p, y_s], axis=0)
            x = matmul_norm_residual(y, lru_w_out[j].astype(BF16), lru_b_out[j], norm_mix_post[i], x, tm=TM)
        x = mlp_sublayer(x, norm_mlp_pre[i], mlp_w_up[i].astype(BF16), mlp_w_down[i].astype(BF16),
                         norm_mlp_post[i], tm=TM, tf=TF)
    return (x[:mp].reshape(bp, seq, d), x[mp:].reshape(bs, n_t, d),
            jnp.stack(fkp), jnp.stack(fvp), jnp.stack(flp),
            jnp.stack(fks), jnp.stack(fvs), jnp.stack(fls),
            jnp.stack(skp), jnp.stack(svp), jnp.stack(sks), jnp.stack(svs),
            jnp.stack(lhp), jnp.stack(lcp), jnp.stack(lhs), jnp.stack(lcs))
```

```python
import functools

import jax
import jax.numpy as jnp
from jax import lax
from jax.experimental import pallas as pl
from jax.experimental.pallas import tpu as pltpu

F32 = jnp.float32
BF16 = jnp.bfloat16

RMS_EPS = 1e-6
WINDOW = 128
ROPE_THETA = 500000.0
LRU_C = 8.0
LOG2E = 1.4426950408889634

V7X_VMEM_LIMIT_BYTES = 56 * 1024 * 1024
LANES = 128

NT_DIMS = (((1,), (1,)), ((), ()))


def _params(*sem):
    return pltpu.CompilerParams(dimension_semantics=sem, vmem_limit_bytes=V7X_VMEM_LIMIT_BYTES)


def _rms(x, g):
    ms = jnp.mean(x * x, axis=-1, keepdims=True)
    return x * lax.rsqrt(ms + RMS_EPS) * g


def _expm1(x):
    u = jnp.exp(x)
    um1 = u - 1.0
    safe = jnp.where(u == 1.0, 1.0, jnp.log(u))
    return jnp.where(u == 1.0, x, jnp.where(um1 == -1.0, -1.0, um1 * x / safe))


def _split3(x):
    hi = x.astype(BF16)
    r1 = x - hi.astype(F32)
    mid = r1.astype(BF16)
    lo = (r1 - mid.astype(F32)).astype(BF16)
    return hi, mid, lo


def _norm_mm_kernel(*refs, n_gelu_tiles, has_bias, has_gate):
    it = iter(refs)
    x_ref, g_ref, w_ref = next(it), next(it), next(it)
    b_ref = next(it) if has_bias else None
    wf_ref, bf_ref = (next(it), next(it)) if has_gate else (None, None)
    o_ref = next(it)
    lf_ref = next(it) if has_gate else None
    xn_ref = next(it)
    j = pl.program_id(1)

    @pl.when(j == 0)
    def _():
        xn = _rms(x_ref[...], g_ref[...]).astype(BF16)
        xn_ref[...] = xn
        if has_gate:
            z = jnp.dot(xn, wf_ref[...], preferred_element_type=F32) + bf_ref[...]
            lf_ref[...] = jax.nn.log_sigmoid(z)

    acc = jnp.dot(xn_ref[...], w_ref[...], preferred_element_type=F32)
    if has_bias:
        acc = acc + b_ref[...]
    if n_gelu_tiles == 0:
        o_ref[...] = acc.astype(o_ref.dtype)
    else:
        @pl.when(j < n_gelu_tiles)
        def _():
            o_ref[...] = jax.nn.gelu(acc).astype(o_ref.dtype)

        @pl.when(j >= n_gelu_tiles)
        def _():
            o_ref[...] = acc.astype(o_ref.dtype)


def norm_matmul(x, g, w, b=None, *, gate=None, n_gelu_cols=0, tm, tn):
    m, d = x.shape
    n = w.shape[1]
    assert m % tm == 0 and n % tn == 0 and n_gelu_cols % tn == 0
    has_bias, has_gate = b is not None, gate is not None
    args = [x, g.reshape(1, d), w]
    in_specs = [
        pl.BlockSpec((tm, d), lambda i, j: (i, 0)),
        pl.BlockSpec((1, d), lambda i, j: (0, 0)),
        pl.BlockSpec((d, tn), lambda i, j: (0, j)),
    ]
    if has_bias:
        args.append(b.reshape(1, n))
        in_specs.append(pl.BlockSpec((1, tn), lambda i, j: (0, j)))
    out_shape = [jax.ShapeDtypeStruct((m, n), F32)]
    out_specs = [pl.BlockSpec((tm, tn), lambda i, j: (i, j))]
    if has_gate:
        w_f, b_f = gate
        h = w_f.shape[1]
        args += [w_f, b_f.reshape(1, h)]
        in_specs += [pl.BlockSpec((d, h), lambda i, j: (0, 0)), pl.BlockSpec((1, h), lambda i, j: (0, 0))]
        out_shape.append(jax.ShapeDtypeStruct((m, h), F32))
        out_specs.append(pl.BlockSpec((tm, h), lambda i, j: (i, 0)))
    outs = pl.pallas_call(
        functools.partial(_norm_mm_kernel, n_gelu_tiles=n_gelu_cols // tn, has_bias=has_bias, has_gate=has_gate),
        grid=(m // tm, n // tn),
        in_specs=in_specs,
        out_specs=out_specs,
        out_shape=out_shape,
        scratch_shapes=[pltpu.VMEM((tm, d), BF16)],
        compiler_params=_params("parallel", "arbitrary"),
        name="norm_matmul",
    )(*args)
    return outs if has_gate else outs[0]


def _mm_norm_res_kernel(*refs, has_bias):
    it = iter(refs)
    a_ref, w_ref = next(it), next(it)
    b_ref = next(it) if has_bias else None
    g_ref, x_ref, o_ref = next(it), next(it), next(it)
    mix = jnp.dot(a_ref[...], w_ref[...], preferred_element_type=F32)
    if has_bias:
        mix = mix + b_ref[...]
    o_ref[...] = x_ref[...] + _rms(mix, g_ref[...])


def matmul_norm_residual(a, w, b, g, x, *, tm):
    m, k = a.shape
    d = w.shape[1]
    assert m % tm == 0
    has_bias = b is not None
    args = [a, w]
    in_specs = [pl.BlockSpec((tm, k), lambda i: (i, 0)), pl.BlockSpec((k, d), lambda i: (0, 0))]
    if has_bias:
        args.append(b.reshape(1, d))
        in_specs.append(pl.BlockSpec((1, d), lambda i: (0, 0)))
    args += [g.reshape(1, d), x]
    in_specs += [pl.BlockSpec((1, d), lambda i: (0, 0)), pl.BlockSpec((tm, d), lambda i: (i, 0))]
    return pl.pallas_call(
        functools.partial(_mm_norm_res_kernel, has_bias=has_bias),
        grid=(m // tm,),
        in_specs=in_specs,
        out_specs=pl.BlockSpec((tm, d), lambda i: (i, 0)),
        out_shape=jax.ShapeDtypeStruct((m, d), F32),
        compiler_params=_params("parallel"),
        name="matmul_norm_residual",
    )(*args)


def _mlp_kernel(x_ref, gpre_ref, wup_ref, wdn_ref, gpost_ref, o_ref, xn_ref, acc_ref):
    f = pl.program_id(1)

    @pl.when(f == 0)
    def _():
        xn_ref[...] = _rms(x_ref[...], gpre_ref[...]).astype(BF16)
        acc_ref[...] = jnp.zeros_like(acc_ref)

    h = jnp.dot(xn_ref[...], wup_ref[...], preferred_element_type=F32)
    h = jnp.square(jnp.maximum(h, 0.0)).astype(BF16)
    acc_ref[...] += jnp.dot(h, wdn_ref[...], preferred_element_type=F32)

    @pl.when(f == pl.num_programs(1) - 1)
    def _():
        o_ref[...] = x_ref[...] + _rms(acc_ref[...], gpost_ref[...])


def mlp_sublayer(x, g_pre, w_up, w_down, g_post, *, tm, tf):
    m, d = x.shape
    dff = w_up.shape[1]
    assert m % tm == 0 and dff % tf == 0
    return pl.pallas_call(
        _mlp_kernel,
        grid=(m // tm, dff // tf),
        in_specs=[
            pl.BlockSpec((tm, d), lambda i, f: (i, 0)),
            pl.BlockSpec((1, d), lambda i, f: (0, 0)),
            pl.BlockSpec((d, tf), lambda i, f: (0, f)),
            pl.BlockSpec((tf, d), lambda i, f: (f, 0)),
            pl.BlockSpec((1, d), lambda i, f: (0, 0)),
        ],
        out_specs=pl.BlockSpec((tm, d), lambda i, f: (i, 0)),
        out_shape=jax.ShapeDtypeStruct((m, d), F32),
        scratch_shapes=[pltpu.VMEM((tm, d), BF16), pltpu.VMEM((tm, d), F32)],
        compiler_params=_params("parallel", "arbitrary"),
        name="mlp_sublayer",
    )(x, g_pre.reshape(1, d), w_up, w_down, g_post.reshape(1, d))


N_PARTS = 3


def _fox_cumsum_kernel(lf_ref, qx_ref, kx_ref, *, chunk, n_kv, n_g):
    s, h = lf_ref.shape[1], lf_ref.shape[2]
    row = lax.broadcasted_iota(jnp.int32, (chunk, chunk), 0)
    col = lax.broadcasted_iota(jnp.int32, (chunk, chunk), 1)
    lower = (row >= col).astype(BF16)

    def placement(n_tiles, lane_of, head_of):
        r = lax.broadcasted_iota(jnp.int32, (h, n_tiles * LANES), 0)
        c = lax.broadcasted_iota(jnp.int32, (h, n_tiles * LANES), 1)
        tile, lane = c // LANES, c % LANES
        mats = []
        for p in range(N_PARTS):
            hit = jnp.zeros((h, n_tiles * LANES), jnp.bool_)
            for g in range(n_g):
                hit = hit | ((r == head_of(tile, g)) & (lane == lane_of(g, p)))
            mats.append(hit)
        return mats, tile, lane

    q_hit, q_tile, q_lane = placement(h, lambda g, p: p, lambda tile, g: jnp.where(tile % n_g == g, tile, -1))
    q_mats = [m.astype(BF16) for m in q_hit]
    q_gl = N_PARTS * (1 + q_tile[0:1] % n_g)
    q_ones = ((q_lane[0:1] >= q_gl) & (q_lane[0:1] < q_gl + N_PARTS)).astype(F32)
    k_hit, _, k_lane = placement(n_kv, lambda g, p: N_PARTS * (1 + g) + p, lambda tile, g: tile * n_g + g)
    k_mats = [(-m.astype(F32)).astype(BF16) for m in k_hit]
    k_ones = (k_lane[0:1] < N_PARTS).astype(F32)

    carry = jnp.zeros((1, h), F32)
    for c in range(s // chunk):
        rows = slice(c * chunk, (c + 1) * chunk)
        hi, mid, lo = _split3(lf_ref[0, rows, :])
        cs = (jnp.dot(lower, hi, preferred_element_type=F32)
              + jnp.dot(lower, mid, preferred_element_type=F32)
              + jnp.dot(lower, lo, preferred_element_type=F32)) + carry
        carry = cs[chunk - 1:chunk, :]
        parts = _split3(cs * LOG2E)
        qx = q_ones + sum(jnp.dot(parts[p], q_mats[p], preferred_element_type=F32) for p in range(N_PARTS))
        kx = k_ones + sum(jnp.dot(parts[p], k_mats[p], preferred_element_type=F32) for p in range(N_PARTS))
        for hh in range(h):
            qx_ref[0, hh, rows, :] = qx[:, hh * LANES:(hh + 1) * LANES].astype(BF16)
        for kv in range(n_kv):
            kx_ref[0, kv, rows, :] = kx[:, kv * LANES:(kv + 1) * LANES].astype(BF16)


def fox_cumsum(logf, *, n_kv, chunk):
    b, s, h = logf.shape
    n_g = h // n_kv
    assert N_PARTS * (1 + n_g) <= LANES
    return pl.pallas_call(
        functools.partial(_fox_cumsum_kernel, chunk=chunk, n_kv=n_kv, n_g=n_g),
        grid=(b,),
        in_specs=[pl.BlockSpec((1, s, h), lambda i: (i, 0, 0))],
        out_specs=[
            pl.BlockSpec((1, h, s, LANES), lambda i: (i, 0, 0, 0)),
            pl.BlockSpec((1, n_kv, s, LANES), lambda i: (i, 0, 0, 0)),
        ],
        out_shape=[
            jax.ShapeDtypeStruct((b, h, s, LANES), BF16),
            jax.ShapeDtypeStruct((b, n_kv, s, LANES), BF16),
        ],
        compiler_params=_params("parallel"),
        name="fox_cumsum",
    )(logf)


def _lane_tiles(x):
    return [x[:, j * LANES:(j + 1) * LANES] for j in range(x.shape[1] // LANES)]


def _fox_flash_kernel(q_ref, k_ref, v_ref, qx_ref, kx_ref, o_ref, ka_ref, vb_ref, s_ref, m_ref, l_ref, acc_ref,
                      *, blk, n_g, hd):
    qi = pl.program_id(2)

    @pl.when(qi == 0)
    def _():
        ka_ref[:, :hd] = k_ref[...].astype(BF16)
        ka_ref[:, hd:] = kx_ref[0, 0]
        vb_ref[...] = v_ref[...].astype(BF16)

    q_scale = hd ** -0.5 * LOG2E
    qa = [jnp.concatenate([(q_ref[:, g * hd:(g + 1) * hd] * q_scale).astype(BF16), qx_ref[0, g]], axis=1)
          for g in range(n_g)]
    m_ref[...] = jnp.full_like(m_ref, -jnp.inf)

    def logits_pass(kc, masked):
        ka = ka_ref[pl.ds(pl.multiple_of(kc * blk, blk), blk), :]
        if masked:
            row = lax.broadcasted_iota(jnp.int32, (blk, blk), 0)
            col = lax.broadcasted_iota(jnp.int32, (blk, blk), 1)
            causal = row >= col
        for g in range(n_g):
            s = lax.dot_general(qa[g], ka, NT_DIMS, preferred_element_type=F32)
            if masked:
                s = jnp.where(causal, s, -jnp.inf)
            s_ref[g, kc] = s
            m = m_ref[g]
            for t in _lane_tiles(s):
                m = jnp.maximum(m, t)
            m_ref[g] = m

    def body1(kc, carry):
        logits_pass(kc, False)
        return carry

    lax.fori_loop(0, qi, body1, 0)
    logits_pass(qi, True)
    for g in range(n_g):
        m_ref[g] = jnp.broadcast_to(jnp.max(m_ref[g], axis=1, keepdims=True), (blk, LANES))
    l_ref[...] = jnp.zeros_like(l_ref)
    acc_ref[...] = jnp.zeros_like(acc_ref)

    def body2(kc, carry):
        vb = vb_ref[pl.ds(pl.multiple_of(kc * blk, blk), blk), :]
        for g in range(n_g):
            m = m_ref[g]
            p = [jnp.exp2(t - m) for t in _lane_tiles(s_ref[g, kc])]
            l_ref[g] += sum(p)
            acc_ref[g] += jnp.dot(jnp.concatenate(p, axis=1).astype(BF16), vb, preferred_element_type=F32)
        return carry

    lax.fori_loop(0, qi + 1, body2, 0)
    for g in range(n_g):
        inv = 1.0 / jnp.sum(l_ref[g], axis=1, keepdims=True)
        o_ref[:, g * hd:(g + 1) * hd] = (acc_ref[g] * inv).astype(o_ref.dtype)


def fox_flash(qkv, qx, kx, *, batch, seq, n_heads, n_kv, hd, blk):
    assert hd == LANES
    n_g = n_heads // n_kv
    nq = seq // blk
    gw = n_g * hd
    return pl.pallas_call(
        functools.partial(_fox_flash_kernel, blk=blk, n_g=n_g, hd=hd),
        grid=(batch, n_kv, nq),
        in_specs=[
            pl.BlockSpec((blk, gw), lambda b, kv, qi: (b * nq + qi, kv)),
            pl.BlockSpec((seq, hd), lambda b, kv, qi: (b, n_heads + kv)),
            pl.BlockSpec((seq, hd), lambda b, kv, qi: (b, n_heads + n_kv + kv)),
            pl.BlockSpec((1, n_g, blk, LANES), lambda b, kv, qi: (b, kv, qi, 0)),
            pl.BlockSpec((1, 1, seq, LANES), lambda b, kv, qi: (b, kv, 0, 0)),
        ],
        out_specs=pl.BlockSpec((blk, gw), lambda b, kv, qi: (b * nq + qi, kv)),
        out_shape=jax.ShapeDtypeStruct((batch * seq, n_heads * hd), BF16),
        scratch_shapes=[
            pltpu.VMEM((seq, hd + LANES), BF16),
            pltpu.VMEM((seq, hd), BF16),
            pltpu.VMEM((n_g, nq, blk, blk), F32),
            pltpu.VMEM((n_g, blk, LANES), F32),
            pltpu.VMEM((n_g, blk, LANES), F32),
            pltpu.VMEM((n_g, blk, hd), F32),
        ],
        compiler_params=_params("parallel", "parallel", "arbitrary"),
        name="fox_flash",
    )(qkv, qkv, qkv, qx, kx)


def _fox_decode_kernel(pt_ref, q_ref, knew_ref, vnew_ref, lfnew_ref, *rest, pages, n_kv, n_heads, hd, n_t):
    del pt_ref
    k_refs, v_refs, lf_refs = rest[:pages], rest[pages:2 * pages], rest[2 * pages:3 * pages]
    o_ref, cq_ref, m_ref, l_ref, acc_ref, carry_ref, kpad_ref, vpad_ref = rest[3 * pages:]
    c = pl.program_id(1)
    scale = hd ** -0.5
    page = lf_refs[0].shape[3]
    n_rows = n_t * n_heads
    n_g = n_heads // n_kv
    jj = lax.broadcasted_iota(jnp.int32, (page, page), 0)
    ss = lax.broadcasted_iota(jnp.int32, (page, page), 1)
    row_kv = (lax.broadcasted_iota(jnp.int32, (n_rows, page), 0) % n_heads) // n_g

    def lane_sums(x_t, tri):
        hi, mid, lo = _split3(x_t)
        out = jnp.dot(jnp.concatenate([hi, mid, lo], axis=0), tri, preferred_element_type=F32)
        return out[:n_heads] + out[n_heads:2 * n_heads] + out[2 * n_heads:]

    def rows4(x):
        return jnp.concatenate([x] * n_t, axis=0)

    def own_kv(parts):
        out = parts[n_kv - 1]
        for kv in range(n_kv - 2, -1, -1):
            out = jnp.where(row_kv == kv, parts[kv], out)
        return out

    def qk(q, keys):
        return own_kv([lax.dot_general(q, keys(kv), NT_DIMS, preferred_element_type=F32) for kv in range(n_kv)])

    def pv(p, values):
        return own_kv([jnp.dot(p, values(kv), preferred_element_type=F32) for kv in range(n_kv)])

    q = q_ref[0].astype(BF16)

    @pl.when(c == 0)
    def _():
        c_new = lane_sums(lfnew_ref[0], (jj <= ss).astype(BF16))
        cq = jnp.concatenate([c_new[:, t:t + 1] for t in range(n_t)], axis=0)
        cq_ref[...] = cq
        kpad_ref[...] = jnp.zeros_like(kpad_ref)
        vpad_ref[...] = jnp.zeros_like(vpad_ref)
        for kv in range(n_kv):
            kpad_ref[kv, 0:n_t, :] = knew_ref[0, :, kv * hd:(kv + 1) * hd]
            vpad_ref[kv, 0:n_t, :] = vnew_ref[0, :, kv * hd:(kv + 1) * hd]
        s = qk(q, lambda kv: kpad_ref[kv].astype(BF16)) * scale
        s = s + (cq - rows4(c_new))
        tok = lax.broadcasted_iota(jnp.int32, (n_rows, page), 0) // n_heads
        key = lax.broadcasted_iota(jnp.int32, (n_rows, page), 1)
        s = jnp.where(key <= tok, s, -jnp.inf)
        m = jnp.max(s, axis=1, keepdims=True)
        p = jnp.exp(s - m)
        m_ref[...] = m
        l_ref[...] = jnp.sum(p, axis=1, keepdims=True)
        acc_ref[...] = pv(p.astype(BF16), lambda kv: vpad_ref[kv].astype(BF16))
        carry_ref[...] = jnp.zeros_like(carry_ref)

    cq = cq_ref[...]
    strict = (jj > ss).astype(BF16)
    carry = carry_ref[...]
    logits = []
    for i in range(pages):
        lf_t = lf_refs[i][0, 0]
        r = lane_sums(lf_t, strict) + carry
        carry = carry + jnp.sum(lf_t, axis=1, keepdims=True)
        s = qk(q, lambda kv: k_refs[i][0, 0, pl.ds(kv, page, stride=n_kv), :].astype(BF16)) * scale
        logits.append(s + (cq + rows4(r)))
    carry_ref[...] = carry
    m_old = m_ref[...]
    m_new = m_old
    for s in logits:
        m_new = jnp.maximum(m_new, jnp.max(s, axis=1, keepdims=True))
    alpha = jnp.exp(m_old - m_new)
    l_new = alpha * l_ref[...]
    acc = alpha * acc_ref[...]
    for i, s in enumerate(logits):
        p = jnp.exp(s - m_new)
        l_new = l_new + jnp.sum(p, axis=1, keepdims=True)
        acc = acc + pv(p.astype(BF16), lambda kv: v_refs[i][0, 0, pl.ds(kv, page, stride=n_kv), :].astype(BF16))
    m_ref[...] = m_new
    l_ref[...] = l_new
    acc_ref[...] = acc

    @pl.when(c == pl.num_programs(1) - 1)
    def _():
        o_ref[0] = acc / l_new


def fox_decode(q_rows, qkv_s, lf_new_t, cache_k, cache_v, cache_lf_t, page_table, layer, *, n_heads, n_kv, hd, pages):
    bsz, n_t = qkv_s.shape[0], qkv_s.shape[1]
    n_pages = page_table.shape[1]
    page = cache_lf_t.shape[3]
    width = n_kv * hd
    n_rows = n_t * n_heads
    assert n_pages % pages == 0 and page == LANES and cache_k.shape[2] == page * n_kv
    n_chunks = n_pages // pages

    def page_map(i):
        return lambda b, c, pt: (layer, pt[b, n_pages - 1 - (c * pages + i)], 0, 0)

    kv_col = n_heads * hd // width
    in_specs = [
        pl.BlockSpec((1, n_rows, hd), lambda b, c, pt: (b, 0, 0)),
        pl.BlockSpec((1, n_t, width), lambda b, c, pt: (b, 0, kv_col)),
        pl.BlockSpec((1, n_t, width), lambda b, c, pt: (b, 0, kv_col + 1)),
        pl.BlockSpec((1, n_heads, page), lambda b, c, pt: (b, 0, 0)),
    ]
    in_specs += [pl.BlockSpec((1, 1, page * n_kv, hd), page_map(i)) for i in range(pages)]
    in_specs += [pl.BlockSpec((1, 1, page * n_kv, hd), page_map(i)) for i in range(pages)]
    in_specs += [pl.BlockSpec((1, 1, n_heads, page), page_map(i)) for i in range(pages)]
    grid_spec = pltpu.PrefetchScalarGridSpec(
        num_scalar_prefetch=1,
        grid=(bsz, n_chunks),
        in_specs=in_specs,
        out_specs=pl.BlockSpec((1, n_rows, hd), lambda b, c, pt: (b, 0, 0)),
        scratch_shapes=[
            pltpu.VMEM((n_rows, 1), F32),
            pltpu.VMEM((n_rows, 1), F32),
            pltpu.VMEM((n_rows, 1), F32),
            pltpu.VMEM((n_rows, hd), F32),
            pltpu.VMEM((n_heads, 1), F32),
            pltpu.VMEM((n_kv, page, hd), F32),
            pltpu.VMEM((n_kv, page, hd), F32),
        ],
    )
    return pl.pallas_call(
        functools.partial(_fox_decode_kernel, pages=pages, n_kv=n_kv, n_heads=n_heads, hd=hd, n_t=n_t),
        grid_spec=grid_spec,
        out_shape=jax.ShapeDtypeStruct((bsz, n_rows, hd), F32),
        compiler_params=_params("parallel", "arbitrary"),
        name="fox_decode",
    )(page_table, q_rows, qkv_s, qkv_s, lf_new_t, *([cache_k] * pages), *([cache_v] * pages), *([cache_lf_t] * pages))


def _rope_tables(pos, hd, reps):
    rot = hd // 4
    half = rot // 2
    inv_freq = ROPE_THETA ** (-jnp.arange(half, dtype=F32) / half)
    ang = pos.astype(F32)[:, None] * inv_freq[None, :]
    cos, sin = jnp.cos(ang), jnp.sin(ang)
    n = pos.shape[0]
    ones = jnp.ones((n, hd - rot), F32)
    zeros_h = jnp.zeros((n, half), F32)
    zeros_r = jnp.zeros((n, hd - rot), F32)
    c = jnp.concatenate([cos, cos, ones], axis=1)
    s_up = jnp.concatenate([-sin, zeros_h, zeros_r], axis=1)
    s_dn = jnp.concatenate([zeros_h, sin, zeros_r], axis=1)
    return tuple(jnp.tile(t, (1, reps)) for t in (c, s_up, s_dn))


def _rope(x, c, s_up, s_dn, half):
    w = x.shape[1]
    pieces = []
    for j in range(w // LANES):
        xj = x[:, j * LANES:(j + 1) * LANES]
        up = pltpu.roll(xj, LANES - half, 1)
        dn = pltpu.roll(xj, half, 1)
        pieces.append(xj * c + up * s_up + dn * s_dn)
    return pieces[0] if len(pieces) == 1 else jnp.concatenate(pieces, axis=1)


def _swa_prompt_kernel(q_ref, kc_ref, kp_ref, vc_ref, vp_ref, cc_ref, su_ref, sd_ref, pc_ref, pu_ref, pd_ref,
                       sink_ref, o_ref, klast_ref, *, n_heads, n_kv, hd, blk):
    qb = pl.program_id(1)
    scale = hd ** -0.5
    half = hd // 8
    n_g = n_heads // n_kv
    per_tile = LANES // hd
    assert per_tile == 2 and n_g % per_tile == 0
    q = _rope(q_ref[...], cc_ref[...], su_ref[...], sd_ref[...], half)
    kc = _rope(kc_ref[...], cc_ref[...], su_ref[...], sd_ref[...], half)
    kp = _rope(kp_ref[...], pc_ref[...], pu_ref[...], pd_ref[...], half)
    k = jnp.concatenate([kp, kc], axis=0)
    v = jnp.concatenate([vp_ref[...], vc_ref[...]], axis=0)
    lane = lax.broadcasted_iota(jnp.int32, (2 * blk, LANES), 1)

    def spread(x, kv):
        tile = x[:, (kv // per_tile) * LANES:(kv // per_tile + 1) * LANES]
        swapped = pltpu.roll(tile, hd, 1)
        low, high = (tile, swapped) if kv % per_tile == 0 else (swapped, tile)
        return jnp.concatenate([jnp.where(lane < hd, low, 0.0), jnp.where(lane >= hd, high, 0.0)],
                               axis=0).astype(BF16)

    t = lax.broadcasted_iota(jnp.int32, (blk, 2 * blk), 0)
    j = lax.broadcasted_iota(jnp.int32, (blk, 2 * blk), 1)
    first_key = jnp.where(qb > 0, 0, blk)
    band = (j > t) & (j <= t + blk) & (j >= first_key)
    out_lane = lax.broadcasted_iota(jnp.int32, (blk, LANES), 1)
    for kv in range(n_kv):
        k2 = spread(k, kv)
        v2 = spread(v, kv)
        for pair in range(n_g // per_tile):
            tile = kv * (n_g // per_tile) + pair
            s = lax.dot_general(q[:, tile * LANES:(tile + 1) * LANES].astype(BF16), k2, NT_DIMS,
                                preferred_element_type=F32) * scale
            es, invs = [], []
            for e in range(per_tile):
                h = per_tile * tile + e
                se = jnp.where(band, s[:, e * 2 * blk:(e + 1) * 2 * blk], -jnp.inf)
                sink = sink_ref[0:1, h:h + 1]
                m = jnp.maximum(jnp.max(se, axis=1, keepdims=True), sink)
                ee = jnp.exp(se - m)
                es.append(ee)
                invs.append(1.0 / (jnp.sum(ee, axis=1, keepdims=True) + jnp.exp(sink - m)))
            out = jnp.dot(jnp.concatenate(es, axis=1).astype(BF16), v2, preferred_element_type=F32)
            out = out * jnp.where(out_lane < hd, invs[0], invs[1])
            o_ref[:, tile * LANES:(tile + 1) * LANES] = out.astype(o_ref.dtype)

    @pl.when(qb == pl.num_programs(1) - 1)
    def _():
        klast_ref[0] = kc


def swa_prompt(qkv, sinks, *, batch, seq, n_heads, n_kv, hd, blk):
    assert blk == WINDOW and (n_kv * hd) % LANES == 0
    nb = seq // blk
    qw, kw = n_heads * hd, n_kv * hd
    kcol = qw // kw
    pos = jnp.arange(seq)
    reps = LANES // hd
    cur = _rope_tables(pos, hd, reps)
    prev = _rope_tables(pos - blk, hd, reps)

    def cur_row(b, qb):
        return b * nb + qb

    def prev_row(b, qb):
        return b * nb + jnp.maximum(qb - 1, 0)

    tab_cur = pl.BlockSpec((blk, LANES), lambda b, qb: (qb, 0))
    tab_prev = pl.BlockSpec((blk, LANES), lambda b, qb: (qb, 0))
    return pl.pallas_call(
        functools.partial(_swa_prompt_kernel, n_heads=n_heads, n_kv=n_kv, hd=hd, blk=blk),
        grid=(batch, nb),
        in_specs=[
            pl.BlockSpec((blk, qw), lambda b, qb: (cur_row(b, qb), 0)),
            pl.BlockSpec((blk, kw), lambda b, qb: (cur_row(b, qb), kcol)),
            pl.BlockSpec((blk, kw), lambda b, qb: (prev_row(b, qb), kcol)),
            pl.BlockSpec((blk, kw), lambda b, qb: (cur_row(b, qb), kcol + 1)),
            pl.BlockSpec((blk, kw), lambda b, qb: (prev_row(b, qb), kcol + 1)),
            tab_cur, tab_cur, tab_cur, tab_prev, tab_prev, tab_prev,
            pl.BlockSpec((1, n_heads), lambda b, qb: (0, 0)),
        ],
        out_specs=[
            pl.BlockSpec((blk, qw), lambda b, qb: (cur_row(b, qb), 0)),
            pl.BlockSpec((1, blk, kw), lambda b, qb: (b, 0, 0)),
        ],
        out_shape=[
            jax.ShapeDtypeStruct((batch * seq, qw), BF16),
            jax.ShapeDtypeStruct((batch, blk, kw), F32),
        ],
        compiler_params=_params("parallel", "arbitrary"),
        name="swa_prompt",
    )(qkv, qkv, qkv, qkv, qkv, *cur, *prev, sinks.reshape(1, n_heads))


def _swa_sample_kernel(q_ref, knew_ref, vnew_ref, bk_ref, bv_ref, qc_ref, qu_ref, qd_ref, kc_ref, ku_ref, kd_ref,
                       sink_ref, o_ref, ok_ref, ov_ref, kall_ref, vall_ref, *, n_heads, n_kv, hd, n_t):
    scale = hd ** -0.5
    half = hd // 8
    n_g = n_heads // n_kv
    n_rows = n_t * n_heads
    nbuf = bk_ref.shape[1]
    width = n_kv * hd
    n_all = kall_ref.shape[0]
    knew = _rope(knew_ref[0], kc_ref[...], ku_ref[...], kd_ref[...], half)
    kall_ref[...] = jnp.zeros_like(kall_ref)
    vall_ref[...] = jnp.zeros_like(vall_ref)
    kall_ref[0:nbuf, :] = bk_ref[0]
    vall_ref[0:nbuf, :] = bv_ref[0]
    kall_ref[nbuf:nbuf + n_t, :] = knew
    vall_ref[nbuf:nbuf + n_t, :] = vnew_ref[0]
    ok_ref[0] = kall_ref[n_t:n_t + nbuf, :]
    ov_ref[0] = vall_ref[n_t:n_t + nbuf, :]
    q = q_ref[0]
    up = jnp.concatenate([q[:, half:], q[:, :half]], axis=1)
    dn = jnp.concatenate([q[:, hd - half:], q[:, :hd - half]], axis=1)
    q = q * qc_ref[...] + up * qu_ref[...] + dn * qd_ref[...]
    qt = jnp.concatenate([q] * n_kv, axis=1)
    row_kv = (lax.broadcasted_iota(jnp.int32, (n_rows, width), 0) % n_heads) // n_g
    lane_kv = lax.broadcasted_iota(jnp.int32, (n_rows, width), 1) // hd
    qbd = jnp.where(row_kv == lane_kv, qt, 0.0).astype(BF16)
    s = lax.dot_general(qbd, kall_ref[...].astype(BF16), NT_DIMS, preferred_element_type=F32) * scale
    tok = lax.broadcasted_iota(jnp.int32, (n_rows, n_all), 0) // n_heads
    key = lax.broadcasted_iota(jnp.int32, (n_rows, n_all), 1)
    band = ((key < nbuf) & (tok + nbuf - key < WINDOW)) | ((key >= nbuf) & (key - nbuf <= tok))
    s = jnp.where(band, s, -jnp.inf)
    sink = sink_ref[...]
    m = jnp.maximum(jnp.max(s, axis=1, keepdims=True), sink)
    e = jnp.exp(s - m)
    p = e / (jnp.sum(e, axis=1, keepdims=True) + jnp.exp(sink - m))
    out = jnp.dot(p.astype(BF16), vall_ref[...].astype(BF16), preferred_element_type=F32)
    row_kv = (lax.broadcasted_iota(jnp.int32, (n_rows, hd), 0) % n_heads) // n_g
    res = jnp.zeros((n_rows, hd), F32)
    for kv in range(n_kv):
        res = jnp.where(row_kv == kv, out[:, kv * hd:(kv + 1) * hd], res)
    o_ref[0] = res


def swa_sample(q_rows, qkv_s, buf_k, buf_v, sinks, past, *, n_heads, n_kv, hd):
    bsz, n_t = qkv_s.shape[0], qkv_s.shape[1]
    nbuf = buf_k.shape[1]
    assert nbuf == WINDOW and n_t <= 8
    width = n_kv * hd
    n_rows = n_t * n_heads
    n_all = nbuf + LANES
    pos = past + jnp.arange(n_t)
    qtab = tuple(jnp.repeat(t, n_heads, axis=0) for t in _rope_tables(pos, hd, 1))
    ktab = _rope_tables(pos, hd, LANES // hd)
    sink_rows = jnp.tile(sinks.astype(F32), n_t).reshape(n_rows, 1)
    kcol = n_heads * hd // width
    full2 = lambda shape: pl.BlockSpec(shape, lambda b: (0, 0))
    return pl.pallas_call(
        functools.partial(_swa_sample_kernel, n_heads=n_heads, n_kv=n_kv, hd=hd, n_t=n_t),
        grid=(bsz,),
        in_specs=[
            pl.BlockSpec((1, n_rows, hd), lambda b: (b, 0, 0)),
            pl.BlockSpec((1, n_t, width), lambda b: (b, 0, kcol)),
            pl.BlockSpec((1, n_t, width), lambda b: (b, 0, kcol + 1)),
            pl.BlockSpec((1, nbuf, width), lambda b: (b, 0, 0)),
            pl.BlockSpec((1, nbuf, width), lambda b: (b, 0, 0)),
            full2((n_rows, hd)), full2((n_rows, hd)), full2((n_rows, hd)),
            full2((n_t, LANES)), full2((n_t, LANES)), full2((n_t, LANES)),
            full2((n_rows, 1)),
        ],
        out_specs=[
            pl.BlockSpec((1, n_rows, hd), lambda b: (b, 0, 0)),
            pl.BlockSpec((1, nbuf, width), lambda b: (b, 0, 0)),
            pl.BlockSpec((1, nbuf, width), lambda b: (b, 0, 0)),
        ],
        out_shape=[
            jax.ShapeDtypeStruct((bsz, n_rows, hd), F32),
            jax.ShapeDtypeStruct((bsz, nbuf, width), F32),
            jax.ShapeDtypeStruct((bsz, nbuf, width), F32),
        ],
        scratch_shapes=[pltpu.VMEM((n_all, width), F32), pltpu.VMEM((n_all, width), F32)],
        compiler_params=_params("parallel"),
        name="swa_sample",
    )(q_rows, qkv_s, qkv_s, buf_k, buf_v, *qtab, *ktab, sink_rows)


def _rglru_kernel(gate_ref, u_ref, cb_ref, h0_ref, cw_ref, cbias_ref, wa_ref, ba_ref, wx_ref, bx_ref, lam_ref,
                  y_ref, hlast_ref, cout_ref, uext_ref, a_ref, d_ref, h_ref, *, n_t, n_seq, chunk):
    tc = pl.program_id(1)
    rows = n_t * n_seq
    tail = (cw_ref.shape[0] - 1) * n_seq
    head = uext_ref.shape[0] - rows
    n_blocks, cb = wa_ref.shape[0], wa_ref.shape[1]

    @pl.when(tc == 0)
    def _():
        uext_ref[head - tail:head, :] = cb_ref[0]
        h_ref[...] = h0_ref[0]

    @pl.when(tc > 0)
    def _():
        uext_ref[head - tail:head, :] = uext_ref[head + rows - tail:head + rows, :]

    uext_ref[head:head + rows, :] = u_ref[...]
    log_sig_lam = jax.nn.log_sigmoid(lam_ref[...])
    for c0 in range(0, rows, chunk):
        n = min(chunk, rows - c0)
        xc = cbias_ref[...]
        for i in range(cw_ref.shape[0]):
            xc = xc + uext_ref[head - tail + i * n_seq + c0:head - tail + i * n_seq + c0 + n, :] * cw_ref[i:i + 1, :]
        xb = xc.astype(BF16)
        r = jnp.concatenate([jnp.dot(xb[:, j * cb:(j + 1) * cb], wa_ref[j], preferred_element_type=F32)
                             for j in range(n_blocks)], axis=1)
        ig = jnp.concatenate([jnp.dot(xb[:, j * cb:(j + 1) * cb], wx_ref[j], preferred_element_type=F32)
                              for j in range(n_blocks)], axis=1)
        r = jax.nn.sigmoid(r + ba_ref[...])
        ig = jax.nn.sigmoid(ig + bx_ref[...])
        log_a = LRU_C * r * log_sig_lam
        a_ref[c0:c0 + n, :] = jnp.exp(log_a)
        d_ref[c0:c0 + n, :] = jnp.sqrt(-_expm1(2.0 * log_a)) * (ig * xc)

    def step(t, h):
        sl = pl.ds(pl.multiple_of(t * n_seq, n_seq), n_seq)
        h = a_ref[sl, :] * h + d_ref[sl, :]
        d_ref[sl, :] = h
        return h

    h_last = lax.fori_loop(0, n_t, step, h_ref[...], unroll=min(8, n_t))
    h_ref[...] = h_last
    for c0 in range(0, rows, chunk):
        n = min(chunk, rows - c0)
        y_ref[c0:c0 + n, :] = (d_ref[c0:c0 + n, :] * gate_ref[c0:c0 + n, :]).astype(y_ref.dtype)

    @pl.when(tc == pl.num_programs(1) - 1)
    def _():
        hlast_ref[0] = h_last
        cout_ref[0] = uext_ref[head + rows - tail:head + rows, :]


def rglru(gu, conv_buf, h0, conv_w, conv_b, w_a, b_a, w_x, b_x, lam, *, n_groups, n_t, n_seq, t_chunk, d_rnn):
    assert n_t % t_chunk == 0
    n_tc = n_t // t_chunk
    rows = t_chunk * n_seq
    tail = conv_buf.shape[1]
    assert tail <= rows
    head = -(-tail // 8) * 8
    chunk = min(rows, 256)
    vec = lambda a: a.reshape(1, d_rnn)
    const2 = lambda g, t: (0, 0)
    const3 = lambda g, t: (0, 0, 0)
    per_group = lambda g, t: (g, 0, 0)
    return pl.pallas_call(
        functools.partial(_rglru_kernel, n_t=t_chunk, n_seq=n_seq, chunk=chunk),
        grid=(n_groups, n_tc),
        in_specs=[
            pl.BlockSpec((rows, d_rnn), lambda g, t: (g * n_tc + t, 0)),
            pl.BlockSpec((rows, d_rnn), lambda g, t: (g * n_tc + t, 1)),
            pl.BlockSpec((1, tail, d_rnn), per_group),
            pl.BlockSpec((1, n_seq, d_rnn), per_group),
            pl.BlockSpec(conv_w.shape, const2),
            pl.BlockSpec((1, d_rnn), const2),
            pl.BlockSpec(w_a.shape, const3),
            pl.BlockSpec((1, d_rnn), const2),
            pl.BlockSpec(w_x.shape, const3),
            pl.BlockSpec((1, d_rnn), const2),
            pl.BlockSpec((1, d_rnn), const2),
        ],
        out_specs=[
            pl.BlockSpec((rows, d_rnn), lambda g, t: (g * n_tc + t, 0)),
            pl.BlockSpec((1, n_seq, d_rnn), per_group),
            pl.BlockSpec((1, tail, d_rnn), per_group),
        ],
        out_shape=[
            jax.ShapeDtypeStruct((n_groups * n_t * n_seq, d_rnn), BF16),
            jax.ShapeDtypeStruct((n_groups, n_seq, d_rnn), F32),
            jax.ShapeDtypeStruct((n_groups, tail, d_rnn), F32),
        ],
        scratch_shapes=[
            pltpu.VMEM((head + rows, d_rnn), F32),
            pltpu.VMEM((rows, d_rnn), F32),
            pltpu.VMEM((rows, d_rnn), F32),
            pltpu.VMEM((n_seq, d_rnn), F32),
        ],
        compiler_params=_params("parallel", "arbitrary"),
        name="rglru",
    )(gu, gu, conv_buf, h0, conv_w, vec(conv_b), w_a, vec(b_a), w_x, vec(b_x), vec(lam))


TM = 640
TN = 512
TF = 512
FOX_BLK = 256
FOX_PAGES = 8
LRU_T_CHUNK = 256


def kernel(x_prompt, x_sample, cache_fox_k, cache_fox_v, cache_fox_logf, cache_swa_k, cache_swa_v, state_lru_h, state_lru_conv, page_table, norm_mix_pre, norm_mix_post, norm_mlp_pre, norm_mlp_post, mlp_w_up, mlp_w_down, fox_w_qkv, fox_w_f, fox_b_f, fox_w_o, swa_w_qkv, swa_b_qkv, swa_sinks, swa_w_o, swa_b_o, lru_w_gate, lru_b_gate, lru_w_in, lru_b_in, lru_conv_w, lru_conv_b, lru_w_a, lru_b_a, lru_w_x, lru_b_x, lru_lambda, lru_w_out, lru_b_out):
    bp, seq, d = x_prompt.shape
    bs, n_t, _ = x_sample.shape
    mp, ms = bp * seq, bs * n_t
    depth = norm_mix_pre.shape[0]
    fox_heads = fox_w_f.shape[2]
    fox_kv, fox_hd = cache_fox_k.shape[3], cache_fox_k.shape[4]
    swa_heads = swa_sinks.shape[1]
    swa_kv, swa_hd = cache_swa_k.shape[3], cache_swa_k.shape[4]
    d_rnn = lru_w_gate.shape[2]
    conv_w = lru_conv_w.shape[1]
    past = page_table.shape[1] * cache_fox_k.shape[2]
    fox_qw, fox_kw = fox_heads * fox_hd, fox_kv * fox_hd
    swa_qw, swa_kw = swa_heads * swa_hd, swa_kv * swa_hd

    x = jnp.concatenate([x_prompt.reshape(mp, d), x_sample.reshape(ms, d)], axis=0)
    n_layers_fox, pool, page = cache_fox_k.shape[:3]
    ck = cache_fox_k.reshape(n_layers_fox, pool, page * fox_kv, fox_hd)
    cv = cache_fox_v.reshape(n_layers_fox, pool, page * fox_kv, fox_hd)
    clf_t = jnp.swapaxes(cache_fox_logf, 2, 3)

    fkp, fvp, flp, fks, fvs, fls = [], [], [], [], [], []
    skp, svp, sks, svs = [], [], [], []
    lhp, lcp, lhs, lcs = [], [], [], []
    for i in range(depth):
        kind, j = i % 3, i // 3
        if kind == 0:
            qkv, logf = norm_matmul(x, norm_mix_pre[i], fox_w_qkv[j].astype(BF16),
                                    gate=(fox_w_f[j].astype(BF16), fox_b_f[j]), tm=TM, tn=TN)
            kp = qkv[:mp, fox_qw:fox_qw + fox_kw]
            vp = qkv[:mp, fox_qw + fox_kw:]
            fkp.append(kp.reshape(bp, seq, fox_kv, fox_hd))
            fvp.append(vp.reshape(bp, seq, fox_kv, fox_hd))
            flp.append(logf[:mp].reshape(bp, seq, fox_heads))
            qkv_s = qkv[mp:].reshape(bs, n_t, -1)
            lf_s = logf[mp:].reshape(bs, n_t, fox_heads)
            fks.append(qkv_s[:, :, fox_qw:fox_qw + fox_kw].reshape(bs, n_t, fox_kv, fox_hd))
            fvs.append(qkv_s[:, :, fox_qw + fox_kw:].reshape(bs, n_t, fox_kv, fox_hd))
            fls.append(lf_s)
            qx, kx = fox_cumsum(logf[:mp].reshape(bp, seq, fox_heads), n_kv=fox_kv, chunk=FOX_BLK)
            att_p = fox_flash(qkv, qx, kx, batch=bp, seq=seq, n_heads=fox_heads, n_kv=fox_kv, hd=fox_hd, blk=FOX_BLK)
            q_rows = qkv_s[:, :, :fox_qw].reshape(bs, n_t * fox_heads, fox_hd)
            lf_new_t = jnp.pad(jnp.swapaxes(lf_s, 1, 2), ((0, 0), (0, 0), (0, page - n_t)))
            att_s = fox_decode(q_rows, qkv_s, lf_new_t, ck, cv, clf_t, page_table, j,
                               n_heads=fox_heads, n_kv=fox_kv, hd=fox_hd, pages=FOX_PAGES)
            att = jnp.concatenate([att_p, att_s.reshape(ms, fox_qw).astype(BF16)], axis=0)
            x = matmul_norm_residual(att, fox_w_o[j].astype(BF16), None, norm_mix_post[i], x, tm=TM)
        elif kind == 1:
            qkv = norm_matmul(x, norm_mix_pre[i], swa_w_qkv[j].astype(BF16), swa_b_qkv[j], tm=TM, tn=TN)
            att_p, k_last = swa_prompt(qkv, swa_sinks[j], batch=bp, seq=seq, n_heads=swa_heads, n_kv=swa_kv,
                                       hd=swa_hd, blk=WINDOW)
            keep = min(WINDOW, seq)
            skp.append(k_last.reshape(bp, keep, swa_kv, swa_hd))
            svp.append(qkv[:mp, swa_qw + swa_kw:].reshape(bp, seq, swa_kv, swa_hd)[:, seq - keep:])
            qkv_s = qkv[mp:].reshape(bs, n_t, -1)
            q_rows = qkv_s[:, :, :swa_qw].reshape(bs, n_t * swa_heads, swa_hd)
            nbuf = cache_swa_k.shape[2]
            att_s, nk, nv = swa_sample(q_rows, qkv_s, cache_swa_k[j].reshape(bs, nbuf, swa_kw),
                                       cache_swa_v[j].reshape(bs, nbuf, swa_kw), swa_sinks[j], past,
                                       n_heads=swa_heads, n_kv=swa_kv, hd=swa_hd)
            sks.append(nk.reshape(bs, nbuf, swa_kv, swa_hd))
            svs.append(nv.reshape(bs, nbuf, swa_kv, swa_hd))
            att = jnp.concatenate([att_p, att_s.reshape(ms, swa_qw).astype(BF16)], axis=0)
            x = matmul_norm_residual(att, swa_w_o[j].astype(BF16), swa_b_o[j], norm_mix_post[i], x, tm=TM)
        else:
            w_gu = jnp.concatenate([lru_w_gate[j], lru_w_in[j]], axis=1).astype(BF16)
            b_gu = jnp.concatenate([lru_b_gate[j], lru_b_in[j]])
            gu = norm_matmul(x, norm_mix_pre[i], w_gu, b_gu, n_gelu_cols=d_rnn, tm=TM, tn=TN)
            lw = (lru_conv_w[j], lru_conv_b[j], lru_w_a[j].astype(BF16), lru_b_a[j], lru_w_x[j].astype(BF16),
                  lru_b_x[j], lru_lambda[j])
            y_p, h_p, c_p = rglru(gu, jnp.zeros((bp, conv_w - 1, d_rnn), F32), jnp.zeros((bp, 1, d_rnn), F32),
                                  *lw, n_groups=bp, n_t=seq, n_seq=1, t_chunk=LRU_T_CHUNK, d_rnn=d_rnn)
            lhp.append(h_p.reshape(bp, d_rnn))
            lcp.append(c_p)
            gu_s = gu[mp:].reshape(bs, n_t, -1).swapaxes(0, 1).reshape(ms, -1)
            cb_s = state_lru_conv[j].swapaxes(0, 1).reshape(1, (conv_w - 1) * bs, d_rnn)
            y_s, h_s, c_s = rglru(gu_s, cb_s, state_lru_h[j].reshape(1, bs, d_rnn), *lw,
                                  n_groups=1, n_t=n_t, n_seq=bs, t_chunk=n_t, d_rnn=d_rnn)
            lhs.append(h_s.reshape(bs, d_rnn))
            lcs.append(c_s.reshape(conv_w - 1, bs, d_rnn).swapaxes(0, 1))
            y_s = y_s.reshape(n_t, bs, d_rnn).swapaxes(0, 1).reshape(ms, d_rnn)
            y = jnp.concatenate([y_p, y_s], axis=0)
            x = matmul_norm_residual(y, lru_w_out[j].astype(BF16), lru_b_out[j], norm_mix_post[i], x, tm=TM)
        x = mlp_sublayer(x, norm_mlp_pre[i], mlp_w_up[i].astype(BF16), mlp_w_down[i].astype(BF16),
                         norm_mlp_post[i], tm=TM, tf=TF)
    return (x[:mp].reshape(bp, seq, d), x[mp:].reshape(bs, n_t, d),
            jnp.stack(fkp), jnp.stack(fvp), jnp.stack(flp),
            jnp.stack(fks), jnp.stack(fvs), jnp.stack(fls),
            jnp.stack(skp), jnp.stack(svp), jnp.stack(sks), jnp.stack(svs),
            jnp.stack(lhp), jnp.stack(lcp), jnp.stack(lhs), jnp.stack(lcs))
```

```python
import functools

import jax
import jax.numpy as jnp
from jax import lax
from jax.experimental import pallas as pl
from jax.experimental.pallas import tpu as pltpu

F32 = jnp.float32
BF16 = jnp.bfloat16

RMS_EPS = 1e-6
WINDOW = 128
ROPE_THETA = 500000.0
LRU_C = 8.0
LOG2E = 1.4426950408889634

V7X_VMEM_LIMIT_BYTES = 56 * 1024 * 1024
LANES = 128

NT_DIMS = (((1,), (1,)), ((), ()))


def _params(*sem):
    return pltpu.CompilerParams(dimension_semantics=sem, vmem_limit_bytes=V7X_VMEM_LIMIT_BYTES)


def _rms(x, g):
    ms = jnp.mean(x * x, axis=-1, keepdims=True)
    return x * lax.rsqrt(ms + RMS_EPS) * g


def _expm1(x):
    u = jnp.exp(x)
    um1 = u - 1.0
    safe = jnp.where(u == 1.0, 1.0, jnp.log(u))
    return jnp.where(u == 1.0, x, jnp.where(um1 == -1.0, -1.0, um1 * x / safe))


def _split3(x):
    hi = x.astype(BF16)
    r1 = x - hi.astype(F32)
    mid = r1.astype(BF16)
    lo = (r1 - mid.astype(F32)).astype(BF16)
    return hi, mid, lo


def _norm_mm_kernel(*refs, n_gelu_tiles, has_bias, has_gate):
    it = iter(refs)
    x_ref, g_ref, w_ref = next(it), next(it), next(it)
    b_ref = next(it) if has_bias else None
    wf_ref, bf_ref = (next(it), next(it)) if has_gate else (None, None)
    o_ref = next(it)
    lf_ref = next(it) if has_gate else None
    xn_ref = next(it)
    j = pl.program_id(1)

    @pl.when(j == 0)
    def _():
        xn = _rms(x_ref[...], g_ref[...]).astype(BF16)
        xn_ref[...] = xn
        if has_gate:
            z = jnp.dot(xn, wf_ref[...], preferred_element_type=F32) + bf_ref[...]
            lf_ref[...] = jax.nn.log_sigmoid(z)

    acc = jnp.dot(xn_ref[...], w_ref[...], preferred_element_type=F32)
    if has_bias:
        acc = acc + b_ref[...]
    if n_gelu_tiles == 0:
        o_ref[...] = acc.astype(o_ref.dtype)
    else:
        @pl.when(j < n_gelu_tiles)
        def _():
            o_ref[...] = jax.nn.gelu(acc).astype(o_ref.dtype)

        @pl.when(j >= n_gelu_tiles)
        def _():
            o_ref[...] = acc.astype(o_ref.dtype)


def norm_matmul(x, g, w, b=None, *, gate=None, n_gelu_cols=0, tm, tn):
    m, d = x.shape
    n = w.shape[1]
    assert m % tm == 0 and n % tn == 0 and n_gelu_cols % tn == 0
    has_bias, has_gate = b is not None, gate is not None
    args = [x, g.reshape(1, d), w]
    in_specs = [
        pl.BlockSpec((tm, d), lambda i, j: (i, 0)),
        pl.BlockSpec((1, d), lambda i, j: (0, 0)),
        pl.BlockSpec((d, tn), lambda i, j: (0, j)),
    ]
    if has_bias:
        args.append(b.reshape(1, n))
        in_specs.append(pl.BlockSpec((1, tn), lambda i, j: (0, j)))
    out_shape = [jax.ShapeDtypeStruct((m, n), F32)]
    out_specs = [pl.BlockSpec((tm, tn), lambda i, j: (i, j))]
    if has_gate:
        w_f, b_f = gate
        h = w_f.shape[1]
        args += [w_f, b_f.reshape(1, h)]
        in_specs += [pl.BlockSpec((d, h), lambda i, j: (0, 0)), pl.BlockSpec((1, h), lambda i, j: (0, 0))]
        out_shape.append(jax.ShapeDtypeStruct((m, h), F32))
        out_specs.append(pl.BlockSpec((tm, h), lambda i, j: (i, 0)))
    outs = pl.pallas_call(
        functools.partial(_norm_mm_kernel, n_gelu_tiles=n_gelu_cols // tn, has_bias=has_bias, has_gate=has_gate),
        grid=(m // tm, n // tn),
        in_specs=in_specs,
        out_specs=out_specs,
        out_shape=out_shape,
        scratch_shapes=[pltpu.VMEM((tm, d), BF16)],
        compiler_params=_params("parallel", "arbitrary"),
        name="norm_matmul",
    )(*args)
    return outs if has_gate else outs[0]


def _mm_norm_res_kernel(*refs, has_bias):
    it = iter(refs)
    a_ref, w_ref = next(it), next(it)
    b_ref = next(it) if has_bias else None
    g_ref, x_ref, o_ref = next(it), next(it), next(it)
    mix = jnp.dot(a_ref[...], w_ref[...], preferred_element_type=F32)
    if has_bias:
        mix = mix + b_ref[...]
    o_ref[...] = x_ref[...] + _rms(mix, g_ref[...])


def matmul_norm_residual(a, w, b, g, x, *, tm):
    m, k = a.shape
    d = w.shape[1]
    assert m % tm == 0
    has_bias = b is not None
    args = [a, w]
    in_specs = [pl.BlockSpec((tm, k), lambda i: (i, 0)), pl.BlockSpec((k, d), lambda i: (0, 0))]
    if has_bias:
        args.append(b.reshape(1, d))
        in_specs.append(pl.BlockSpec((1, d), lambda i: (0, 0)))
    args += [g.reshape(1, d), x]
    in_specs += [pl.BlockSpec((1, d), lambda i: (0, 0)), pl.BlockSpec((tm, d), lambda i: (i, 0))]
    return pl.pallas_call(
        functools.partial(_mm_norm_res_kernel, has_bias=has_bias),
        grid=(m // tm,),
        in_specs=in_specs,
        out_specs=pl.BlockSpec((tm, d), lambda i: (i, 0)),
        out_shape=jax.ShapeDtypeStruct((m, d), F32),
        compiler_params=_params("parallel"),
        name="matmul_norm_residual",
    )(*args)


def _mlp_kernel(x_ref, gpre_ref, wup_ref, wdn_ref, gpost_ref, o_ref, xn_ref, acc_ref):
    f = pl.program_id(1)

    @pl.when(f == 0)
    def _():
        xn_ref[...] = _rms(x_ref[...], gpre_ref[...]).astype(BF16)
        acc_ref[...] = jnp.zeros_like(acc_ref)

    h = jnp.dot(xn_ref[...], wup_ref[...], preferred_element_type=F32)
    h = jnp.square(jnp.maximum(h, 0.0)).astype(BF16)
    acc_ref[...] += jnp.dot(h, wdn_ref[...], preferred_element_type=F32)

    @pl.when(f == pl.num_programs(1) - 1)
    def _():
        o_ref[...] = x_ref[...] + _rms(acc_ref[...], gpost_ref[...])


def mlp_sublayer(x, g_pre, w_up, w_down, g_post, *, tm, tf):
    m, d = x.shape
    dff = w_up.shape[1]
    assert m % tm == 0 and dff % tf == 0
    return pl.pallas_call(
        _mlp_kernel,
        grid=(m // tm, dff // tf),
        in_specs=[
            pl.BlockSpec((tm, d), lambda i, f: (i, 0)),
            pl.BlockSpec((1, d), lambda i, f: (0, 0)),
            pl.BlockSpec((d, tf), lambda i, f: (0, f)),
            pl.BlockSpec((tf, d), lambda i, f: (f, 0)),
            pl.BlockSpec((1, d), lambda i, f: (0, 0)),
        ],
        out_specs=pl.BlockSpec((tm, d), lambda i, f: (i, 0)),
        out_shape=jax.ShapeDtypeStruct((m, d), F32),
        scratch_shapes=[pltpu.VMEM((tm, d), BF16), pltpu.VMEM((tm, d), F32)],
        compiler_params=_params("parallel", "arbitrary"),
        name="mlp_sublayer",
    )(x, g_pre.reshape(1, d), w_up, w_down, g_post.reshape(1, d))


N_PARTS = 3


def _fox_cumsum_kernel(lf_ref, qx_ref, kx_ref, *, chunk, n_kv, n_g):
    s, h = lf_ref.shape[1], lf_ref.shape[2]
    row = lax.broadcasted_iota(jnp.int32, (chunk, chunk), 0)
    col = lax.broadcasted_iota(jnp.int32, (chunk, chunk), 1)
    lower = (row >= col).astype(BF16)

    def placement(n_tiles, lane_of, head_of):
        r = lax.broadcasted_iota(jnp.int32, (h, n_tiles * LANES), 0)
        c = lax.broadcasted_iota(jnp.int32, (h, n_tiles * LANES), 1)
        tile, lane = c // LANES, c % LANES
        mats = []
        for p in range(N_PARTS):
            hit = jnp.zeros((h, n_tiles * LANES), jnp.bool_)
            for g in range(n_g):
                hit = hit | ((r == head_of(tile, g)) & (lane == lane_of(g, p)))
            mats.append(hit)
        return mats, tile, lane

    q_hit, q_tile, q_lane = placement(h, lambda g, p: p, lambda tile, g: jnp.where(tile % n_g == g, tile, -1))
    q_mats = [m.astype(BF16) for m in q_hit]
    q_gl = N_PARTS * (1 + q_tile[0:1] % n_g)
    q_ones = ((q_lane[0:1] >= q_gl) & (q_lane[0:1] < q_gl + N_PARTS)).astype(F32)
    k_hit, _, k_lane = placement(n_kv, lambda g, p: N_PARTS * (1 + g) + p, lambda tile, g: tile * n_g + g)
    k_mats = [(-m.astype(F32)).astype(BF16) for m in k_hit]
    k_ones = (k_lane[0:1] < N_PARTS).astype(F32)

    carry = jnp.zeros((1, h), F32)
    for c in range(s // chunk):
        rows = slice(c * chunk, (c + 1) * chunk)
        hi, mid, lo = _split3(lf_ref[0, rows, :])
        cs = (jnp.dot(lower, hi, preferred_element_type=F32)
              + jnp.dot(lower, mid, preferred_element_type=F32)
              + jnp.dot(lower, lo, preferred_element_type=F32)) + carry
        carry = cs[chunk - 1:chunk, :]
        parts = _split3(cs * LOG2E)
        qx = q_ones + sum(jnp.dot(parts[p], q_mats[p], preferred_element_type=F32) for p in range(N_PARTS))
        kx = k_ones + sum(jnp.dot(parts[p], k_mats[p], preferred_element_type=F32) for p in range(N_PARTS))
        for hh in range(h):
            qx_ref[0, hh, rows, :] = qx[:, hh * LANES:(hh + 1) * LANES].astype(BF16)
        for kv in range(n_kv):
            kx_ref[0, kv, rows, :] = kx[:, kv * LANES:(kv + 1) * LANES].astype(BF16)


def fox_cumsum(logf, *, n_kv, chunk):
    b, s, h = logf.shape
    n_g = h // n_kv
    assert N_PARTS * (1 + n_g) <= LANES
    return pl.pallas_call(
        functools.partial(_fox_cumsum_kernel, chunk=chunk, n_kv=n_kv, n_g=n_g),
        grid=(b,),
        in_specs=[pl.BlockSpec((1, s, h), lambda i: (i, 0, 0))],
        out_specs=[
            pl.BlockSpec((1, h, s, LANES), lambda i: (i, 0, 0, 0)),
            pl.BlockSpec((1, n_kv, s, LANES), lambda i: (i, 0, 0, 0)),
        ],
        out_shape=[
            jax.ShapeDtypeStruct((b, h, s, LANES), BF16),
            jax.ShapeDtypeStruct((b, n_kv, s, LANES), BF16),
        ],
        compiler_params=_params("parallel"),
        name="fox_cumsum",
    )(logf)


def _lane_tiles(x):
    return [x[:, j * LANES:(j + 1) * LANES] for j in range(x.shape[1] // LANES)]


def _fox_flash_kernel(q_ref, k_ref, v_ref, qx_ref, kx_ref, o_ref, ka_ref, vb_ref, s_ref, m_ref, l_ref, acc_ref,
                      *, tq, blk, n_g, hd):
    qi = pl.program_id(2)
    per_q = tq // blk

    @pl.when(qi == 0)
    def _():
        ka_ref[:, :hd] = k_ref[...].astype(BF16)
        ka_ref[:, hd:] = kx_ref[0, 0]
        vb_ref[...] = v_ref[...].astype(BF16)

    q_scale = hd ** -0.5 * LOG2E
    qa = [jnp.concatenate([(q_ref[:, g * hd:(g + 1) * hd] * q_scale).astype(BF16), qx_ref[0, g]], axis=1)
          for g in range(n_g)]
    m_ref[...] = jnp.full_like(m_ref, -jnp.inf)

    def logits_pass(kc, diag):
        ka = ka_ref[pl.ds(pl.multiple_of(kc * blk, blk), blk), :]
        if diag is not None:
            row = lax.broadcasted_iota(jnp.int32, (tq, blk), 0)
            col = lax.broadcasted_iota(jnp.int32, (tq, blk), 1) + diag * blk
            causal = row >= col
        for g in range(n_g):
            s = lax.dot_general(qa[g], ka, NT_DIMS, preferred_element_type=F32)
            if diag is not None:
                s = jnp.where(causal, s, -jnp.inf)
            s_ref[g, kc] = s
            m = m_ref[g]
            for t in _lane_tiles(s):
                m = jnp.maximum(m, t)
            m_ref[g] = m

    def body1(kc, carry):
        logits_pass(kc, None)
        return carry

    lax.fori_loop(0, per_q * qi, body1, 0)
    for e in range(per_q):
        logits_pass(per_q * qi + e, e)
    for g in range(n_g):
        m_ref[g] = jnp.broadcast_to(jnp.max(m_ref[g], axis=1, keepdims=True), (tq, LANES))
    l_ref[...] = jnp.zeros_like(l_ref)
    acc_ref[...] = jnp.zeros_like(acc_ref)

    def body2(kc, carry):
        vb = vb_ref[pl.ds(pl.multiple_of(kc * blk, blk), blk), :]
        for g in range(n_g):
            m = m_ref[g]
            p = [jnp.exp2(t - m) for t in _lane_tiles(s_ref[g, kc])]
            l_ref[g] += sum(p)
            acc_ref[g] += jnp.dot(jnp.concatenate(p, axis=1).astype(BF16), vb, preferred_element_type=F32)
        return carry

    lax.fori_loop(0, per_q * (qi + 1), body2, 0)
    for g in range(n_g):
        inv = 1.0 / jnp.sum(l_ref[g], axis=1, keepdims=True)
        o_ref[:, g * hd:(g + 1) * hd] = (acc_ref[g] * inv).astype(o_ref.dtype)


def fox_flash(qkv, qx, kx, *, batch, seq, n_heads, n_kv, hd, tq, blk):
    assert hd == LANES and tq % blk == 0 and seq % tq == 0
    n_g = n_heads // n_kv
    nq = seq // tq
    gw = n_g * hd
    return pl.pallas_call(
        functools.partial(_fox_flash_kernel, tq=tq, blk=blk, n_g=n_g, hd=hd),
        grid=(batch, n_kv, nq),
        in_specs=[
            pl.BlockSpec((tq, gw), lambda b, kv, qi: (b * nq + qi, kv)),
            pl.BlockSpec((seq, hd), lambda b, kv, qi: (b, n_heads + kv)),
            pl.BlockSpec((seq, hd), lambda b, kv, qi: (b, n_heads + n_kv + kv)),
            pl.BlockSpec((1, n_g, tq, LANES), lambda b, kv, qi: (b, kv, qi, 0)),
            pl.BlockSpec((1, 1, seq, LANES), lambda b, kv, qi: (b, kv, 0, 0)),
        ],
        out_specs=pl.BlockSpec((tq, gw), lambda b, kv, qi: (b * nq + qi, kv)),
        out_shape=jax.ShapeDtypeStruct((batch * seq, n_heads * hd), BF16),
        scratch_shapes=[
            pltpu.VMEM((seq, hd + LANES), BF16),
            pltpu.VMEM((seq, hd), BF16),
            pltpu.VMEM((n_g, seq // blk, tq, blk), F32),
            pltpu.VMEM((n_g, tq, LANES), F32),
            pltpu.VMEM((n_g, tq, LANES), F32),
            pltpu.VMEM((n_g, tq, hd), F32),
        ],
        compiler_params=_params("parallel", "parallel", "arbitrary"),
        name="fox_flash",
    )(qkv, qkv, qkv, qx, kx)


def _fox_decode_kernel(pt_ref, q_ref, knew_ref, vnew_ref, lfnew_ref, *rest, pages, n_kv, n_heads, hd, n_t):
    del pt_ref
    k_refs, v_refs, lf_refs = rest[:pages], rest[pages:2 * pages], rest[2 * pages:3 * pages]
    o_ref, cq_ref, m_ref, l_ref, acc_ref, carry_ref, kpad_ref, vpad_ref = rest[3 * pages:]
    c = pl.program_id(1)
    scale = hd ** -0.5
    page = lf_refs[0].shape[3]
    n_g = n_heads // n_kv
    grp = n_t * n_g
    jj = lax.broadcasted_iota(jnp.int32, (page, page), 0)
    ss = lax.broadcasted_iota(jnp.int32, (page, page), 1)

    def lane_sums(xs, tri):
        stacked = jnp.concatenate([part for x in xs for part in _split3(x)], axis=0)
        out = jnp.dot(stacked, tri, preferred_element_type=F32)
        n = N_PARTS * n_heads
        return [out[i * n:i * n + n_heads] + out[i * n + n_heads:i * n + 2 * n_heads]
                + out[i * n + 2 * n_heads:(i + 1) * n] for i in range(len(xs))]

    def group_rows(x, kv):
        return jnp.concatenate([x[kv * n_g:(kv + 1) * n_g]] * n_t, axis=0)

    q = [q_ref[0, kv].astype(BF16) for kv in range(n_kv)]

    @pl.when(c == 0)
    def _():
        c_new = lane_sums([lfnew_ref[0]], (jj <= ss).astype(BF16))[0]
        kpad_ref[...] = jnp.zeros_like(kpad_ref)
        vpad_ref[...] = jnp.zeros_like(vpad_ref)
        tok = lax.broadcasted_iota(jnp.int32, (grp, page), 0) // n_g
        key = lax.broadcasted_iota(jnp.int32, (grp, page), 1)
        for kv in range(n_kv):
            cq = jnp.concatenate([c_new[kv * n_g:(kv + 1) * n_g, t:t + 1] for t in range(n_t)], axis=0)
            cq_ref[kv] = cq
            kpad_ref[kv, 0:n_t, :] = knew_ref[0, :, kv * hd:(kv + 1) * hd]
            vpad_ref[kv, 0:n_t, :] = vnew_ref[0, :, kv * hd:(kv + 1) * hd]
            s = lax.dot_general(q[kv], kpad_ref[kv].astype(BF16), NT_DIMS, preferred_element_type=F32) * scale
            s = s + (cq - group_rows(c_new, kv))
            s = jnp.where(key <= tok, s, -jnp.inf)
            m = jnp.max(s, axis=1, keepdims=True)
            p = jnp.exp(s - m)
            m_ref[kv] = m
            l_ref[kv] = jnp.sum(p, axis=1, keepdims=True)
            acc_ref[kv] = jnp.dot(p.astype(BF16), vpad_ref[kv].astype(BF16), preferred_element_type=F32)
        carry_ref[...] = jnp.zeros_like(carry_ref)

    lfs = [lf_refs[i][0, 0] for i in range(pages)]
    local = lane_sums(lfs, (jj > ss).astype(BF16))
    carry = carry_ref[...]
    rs = []
    for i in range(pages):
        rs.append(local[i] + carry)
        carry = carry + (local[i][:, 0:1] + lfs[i][:, 0:1])
    carry_ref[...] = carry
    raw = [[lax.dot_general(q[kv], k_refs[i][0, 0, pl.ds(kv, page, stride=n_kv), :].astype(BF16), NT_DIMS,
                            preferred_element_type=F32) for i in range(pages)] for kv in range(n_kv)]
    probs, alphas = [], []
    for kv in range(n_kv):
        cq = cq_ref[kv]
        logits = [raw[kv][i] * scale + (cq + group_rows(rs[i], kv)) for i in range(pages)]
        m_old = m_ref[kv]
        tile_max = logits[0]
        for s in logits[1:]:
            tile_max = jnp.maximum(tile_max, s)
        m_new = jnp.maximum(m_old, jnp.max(tile_max, axis=1, keepdims=True))
        alpha = jnp.exp(m_old - m_new)
        ps = [jnp.exp(s - m_new) for s in logits]
        l_ref[kv] = alpha * l_ref[kv] + jnp.sum(sum(ps), axis=1, keepdims=True)
        m_ref[kv] = m_new
        probs.append([p.astype(BF16) for p in ps])
        alphas.append(alpha)
    for kv in range(n_kv):
        pv = sum(jnp.dot(probs[kv][i], v_refs[i][0, 0, pl.ds(kv, page, stride=n_kv), :].astype(BF16),
                         preferred_element_type=F32) for i in range(pages))
        acc_ref[kv] = alphas[kv] * acc_ref[kv] + pv

    @pl.when(c == pl.num_programs(1) - 1)
    def _():
        for kv in range(n_kv):
            o_ref[0, kv] = acc_ref[kv] / l_ref[kv]


def fox_decode(q_rows, qkv_s, lf_new_t, cache_k, cache_v, cache_lf_t, page_table, layer, *, n_heads, n_kv, hd, pages):
    bsz, n_t = qkv_s.shape[0], qkv_s.shape[1]
    n_pages = page_table.shape[1]
    page = cache_lf_t.shape[3]
    width = n_kv * hd
    grp = n_t * (n_heads // n_kv)
    assert n_pages % pages == 0 and page == LANES and cache_k.shape[2] == page * n_kv
    n_chunks = n_pages // pages

    def page_map(i):
        return lambda b, c, pt: (layer, pt[b, n_pages - 1 - (c * pages + i)], 0, 0)

    kv_col = n_heads * hd // width
    in_specs = [
        pl.BlockSpec((1, n_kv, grp, hd), lambda b, c, pt: (b, 0, 0, 0)),
        pl.BlockSpec((1, n_t, width), lambda b, c, pt: (b, 0, kv_col)),
        pl.BlockSpec((1, n_t, width), lambda b, c, pt: (b, 0, kv_col + 1)),
        pl.BlockSpec((1, n_heads, page), lambda b, c, pt: (b, 0, 0)),
    ]
    in_specs += [pl.BlockSpec((1, 1, page * n_kv, hd), page_map(i)) for i in range(pages)]
    in_specs += [pl.BlockSpec((1, 1, page * n_kv, hd), page_map(i)) for i in range(pages)]
    in_specs += [pl.BlockSpec((1, 1, n_heads, page), page_map(i)) for i in range(pages)]
    grid_spec = pltpu.PrefetchScalarGridSpec(
        num_scalar_prefetch=1,
        grid=(bsz, n_chunks),
        in_specs=in_specs,
        out_specs=pl.BlockSpec((1, n_kv, grp, hd), lambda b, c, pt: (b, 0, 0, 0)),
        scratch_shapes=[
            pltpu.VMEM((n_kv, grp, 1), F32),
            pltpu.VMEM((n_kv, grp, 1), F32),
            pltpu.VMEM((n_kv, grp, 1), F32),
            pltpu.VMEM((n_kv, grp, hd), F32),
            pltpu.VMEM((n_heads, 1), F32),
            pltpu.VMEM((n_kv, page, hd), F32),
            pltpu.VMEM((n_kv, page, hd), F32),
        ],
    )
    return pl.pallas_call(
        functools.partial(_fox_decode_kernel, pages=pages, n_kv=n_kv, n_heads=n_heads, hd=hd, n_t=n_t),
        grid_spec=grid_spec,
        out_shape=jax.ShapeDtypeStruct((bsz, n_kv, grp, hd), F32),
        compiler_params=_params("parallel", "arbitrary"),
        name="fox_decode",
    )(page_table, q_rows, qkv_s, qkv_s, lf_new_t, *([cache_k] * pages), *([cache_v] * pages), *([cache_lf_t] * pages))


def _rope_tables(pos, hd, reps):
    rot = hd // 4
    half = rot // 2
    inv_freq = ROPE_THETA ** (-jnp.arange(half, dtype=F32) / half)
    ang = pos.astype(F32)[:, None] * inv_freq[None, :]
    cos, sin = jnp.cos(ang), jnp.sin(ang)
    n = pos.shape[0]
    ones = jnp.ones((n, hd - rot), F32)
    zeros_h = jnp.zeros((n, half), F32)
    zeros_r = jnp.zeros((n, hd - rot), F32)
    c = jnp.concatenate([cos, cos, ones], axis=1)
    s_up = jnp.concatenate([-sin, zeros_h, zeros_r], axis=1)
    s_dn = jnp.concatenate([zeros_h, sin, zeros_r], axis=1)
    return tuple(jnp.tile(t, (1, reps)) for t in (c, s_up, s_dn))


def _rope(x, c, s_up, s_dn, half):
    w = x.shape[1]
    pieces = []
    for j in range(w // LANES):
        xj = x[:, j * LANES:(j + 1) * LANES]
        up = pltpu.roll(xj, LANES - half, 1)
        dn = pltpu.roll(xj, half, 1)
        pieces.append(xj * c + up * s_up + dn * s_dn)
    return pieces[0] if len(pieces) == 1 else jnp.concatenate(pieces, axis=1)


def _swa_prompt_kernel(q_ref, kc_ref, kp_ref, vc_ref, vp_ref, cc_ref, su_ref, sd_ref, pc_ref, pu_ref, pd_ref,
                       sink_ref, o_ref, klast_ref, *, n_heads, n_kv, hd, blk):
    qb = pl.program_id(1)
    scale = hd ** -0.5
    half = hd // 8
    n_g = n_heads // n_kv
    per_tile = LANES // hd
    assert per_tile == 2 and n_g % per_tile == 0
    q = _rope(q_ref[...], cc_ref[...], su_ref[...], sd_ref[...], half)
    kc = _rope(kc_ref[...], cc_ref[...], su_ref[...], sd_ref[...], half)
    kp = _rope(kp_ref[...], pc_ref[...], pu_ref[...], pd_ref[...], half)
    k = jnp.concatenate([kp, kc], axis=0)
    v = jnp.concatenate([vp_ref[...], vc_ref[...]], axis=0)
    lane = lax.broadcasted_iota(jnp.int32, (2 * blk, LANES), 1)

    def spread(x, kv):
        tile = x[:, (kv // per_tile) * LANES:(kv // per_tile + 1) * LANES]
        swapped = pltpu.roll(tile, hd, 1)
        low, high = (tile, swapped) if kv % per_tile == 0 else (swapped, tile)
        return jnp.concatenate([jnp.where(lane < hd, low, 0.0), jnp.where(lane >= hd, high, 0.0)],
                               axis=0).astype(BF16)

    t = lax.broadcasted_iota(jnp.int32, (blk, 2 * blk), 0)
    j = lax.broadcasted_iota(jnp.int32, (blk, 2 * blk), 1)
    first_key = jnp.where(qb > 0, 0, blk)
    band = (j > t) & (j <= t + blk) & (j >= first_key)
    out_lane = lax.broadcasted_iota(jnp.int32, (blk, LANES), 1)
    pairs = n_g // per_tile
    n_tiles = n_kv * pairs
    k2 = [spread(k, kv) for kv in range(n_kv)]
    raw = [lax.dot_general(q[:, tile * LANES:(tile + 1) * LANES].astype(BF16), k2[tile // pairs], NT_DIMS,
                           preferred_element_type=F32) for tile in range(n_tiles)]
    weights, scales = [], []
    for tile in range(n_tiles):
        s = raw[tile] * scale
        es, invs = [], []
        for e in range(per_tile):
            h = per_tile * tile + e
            se = jnp.where(band, s[:, e * 2 * blk:(e + 1) * 2 * blk], -jnp.inf)
            sink = sink_ref[0:1, h:h + 1]
            m = jnp.maximum(jnp.max(se, axis=1, keepdims=True), sink)
            ee = jnp.exp(se - m)
            es.append(ee)
            invs.append(1.0 / (jnp.sum(ee, axis=1, keepdims=True) + jnp.exp(sink - m)))
        weights.append(jnp.concatenate(es, axis=1).astype(BF16))
        scales.append(jnp.where(out_lane < hd, invs[0], invs[1]))
    v2 = [spread(v, kv) for kv in range(n_kv)]
    for tile in range(n_tiles):
        out = jnp.dot(weights[tile], v2[tile // pairs], preferred_element_type=F32) * scales[tile]
        o_ref[:, tile * LANES:(tile + 1) * LANES] = out.astype(o_ref.dtype)

    @pl.when(qb == pl.num_programs(1) - 1)
    def _():
        klast_ref[0] = kc


def swa_prompt(qkv, sinks, *, batch, seq, n_heads, n_kv, hd, blk):
    assert blk == WINDOW and (n_kv * hd) % LANES == 0
    nb = seq // blk
    qw, kw = n_heads * hd, n_kv * hd
    kcol = qw // kw
    pos = jnp.arange(seq)
    reps = LANES // hd
    cur = _rope_tables(pos, hd, reps)
    prev = _rope_tables(pos - blk, hd, reps)

    def cur_row(b, qb):
        return b * nb + qb

    def prev_row(b, qb):
        return b * nb + jnp.maximum(qb - 1, 0)

    tab_cur = pl.BlockSpec((blk, LANES), lambda b, qb: (qb, 0))
    tab_prev = pl.BlockSpec((blk, LANES), lambda b, qb: (qb, 0))
    return pl.pallas_call(
        functools.partial(_swa_prompt_kernel, n_heads=n_heads, n_kv=n_kv, hd=hd, blk=blk),
        grid=(batch, nb),
        in_specs=[
            pl.BlockSpec((blk, qw), lambda b, qb: (cur_row(b, qb), 0)),
            pl.BlockSpec((blk, kw), lambda b, qb: (cur_row(b, qb), kcol)),
            pl.BlockSpec((blk, kw), lambda b, qb: (prev_row(b, qb), kcol)),
            pl.BlockSpec((blk, kw), lambda b, qb: (cur_row(b, qb), kcol + 1)),
            pl.BlockSpec((blk, kw), lambda b, qb: (prev_row(b, qb), kcol + 1)),
            tab_cur, tab_cur, tab_cur, tab_prev, tab_prev, tab_prev,
            pl.BlockSpec((1, n_heads), lambda b, qb: (0, 0)),
        ],
        out_specs=[
            pl.BlockSpec((blk, qw), lambda b, qb: (cur_row(b, qb), 0)),
            pl.BlockSpec((1, blk, kw), lambda b, qb: (b, 0, 0)),
        ],
        out_shape=[
            jax.ShapeDtypeStruct((batch * seq, qw), BF16),
            jax.ShapeDtypeStruct((batch, blk, kw), F32),
        ],
        compiler_params=_params("parallel", "arbitrary"),
        name="swa_prompt",
    )(qkv, qkv, qkv, qkv, qkv, *cur, *prev, sinks.reshape(1, n_heads))


def _swa_sample_kernel(q_ref, knew_ref, vnew_ref, bk_ref, bv_ref, qc_ref, qu_ref, qd_ref, kc_ref, ku_ref, kd_ref,
                       sink_ref, o_ref, ok_ref, ov_ref, kall_ref, vall_ref, *, n_heads, n_kv, hd, n_t):
    scale = hd ** -0.5
    half = hd // 8
    n_g = n_heads // n_kv
    n_rows = n_t * n_heads
    nbuf = bk_ref.shape[1]
    width = n_kv * hd
    n_all = kall_ref.shape[0]
    knew = _rope(knew_ref[0], kc_ref[...], ku_ref[...], kd_ref[...], half)
    kall_ref[...] = jnp.zeros_like(kall_ref)
    vall_ref[...] = jnp.zeros_like(vall_ref)
    kall_ref[0:nbuf, :] = bk_ref[0]
    vall_ref[0:nbuf, :] = bv_ref[0]
    kall_ref[nbuf:nbuf + n_t, :] = knew
    vall_ref[nbuf:nbuf + n_t, :] = vnew_ref[0]
    ok_ref[0] = kall_ref[n_t:n_t + nbuf, :]
    ov_ref[0] = vall_ref[n_t:n_t + nbuf, :]
    q = q_ref[0]
    up = jnp.concatenate([q[:, half:], q[:, :half]], axis=1)
    dn = jnp.concatenate([q[:, hd - half:], q[:, :hd - half]], axis=1)
    q = q * qc_ref[...] + up * qu_ref[...] + dn * qd_ref[...]
    qt = jnp.concatenate([q] * n_kv, axis=1)
    row_kv = (lax.broadcasted_iota(jnp.int32, (n_rows, width), 0) % n_heads) // n_g
    lane_kv = lax.broadcasted_iota(jnp.int32, (n_rows, width), 1) // hd
    qbd = jnp.where(row_kv == lane_kv, qt, 0.0).astype(BF16)
    s = lax.dot_general(qbd, kall_ref[...].astype(BF16), NT_DIMS, preferred_element_type=F32) * scale
    tok = lax.broadcasted_iota(jnp.int32, (n_rows, n_all), 0) // n_heads
    key = lax.broadcasted_iota(jnp.int32, (n_rows, n_all), 1)
    band = ((key < nbuf) & (tok + nbuf - key < WINDOW)) | ((key >= nbuf) & (key - nbuf <= tok))
    s = jnp.where(band, s, -jnp.inf)
    sink = sink_ref[...]
    m = jnp.maximum(jnp.max(s, axis=1, keepdims=True), sink)
    e = jnp.exp(s - m)
    p = e / (jnp.sum(e, axis=1, keepdims=True) + jnp.exp(sink - m))
    out = jnp.dot(p.astype(BF16), vall_ref[...].astype(BF16), preferred_element_type=F32)
    row_kv = (lax.broadcasted_iota(jnp.int32, (n_rows, hd), 0) % n_heads) // n_g
    res = jnp.zeros((n_rows, hd), F32)
    for kv in range(n_kv):
        res = jnp.where(row_kv == kv, out[:, kv * hd:(kv + 1) * hd], res)
    o_ref[0] = res


def swa_sample(q_rows, qkv_s, buf_k, buf_v, sinks, past, *, n_heads, n_kv, hd):
    bsz, n_t = qkv_s.shape[0], qkv_s.shape[1]
    nbuf = buf_k.shape[1]
    assert nbuf == WINDOW and n_t <= 8
    width = n_kv * hd
    n_rows = n_t * n_heads
    n_all = nbuf + LANES
    pos = past + jnp.arange(n_t)
    qtab = tuple(jnp.repeat(t, n_heads, axis=0) for t in _rope_tables(pos, hd, 1))
    ktab = _rope_tables(pos, hd, LANES // hd)
    sink_rows = jnp.tile(sinks.astype(F32), n_t).reshape(n_rows, 1)
    kcol = n_heads * hd // width
    full2 = lambda shape: pl.BlockSpec(shape, lambda b: (0, 0))
    return pl.pallas_call(
        functools.partial(_swa_sample_kernel, n_heads=n_heads, n_kv=n_kv, hd=hd, n_t=n_t),
        grid=(bsz,),
        in_specs=[
            pl.BlockSpec((1, n_rows, hd), lambda b: (b, 0, 0)),
            pl.BlockSpec((1, n_t, width), lambda b: (b, 0, kcol)),
            pl.BlockSpec((1, n_t, width), lambda b: (b, 0, kcol + 1)),
            pl.BlockSpec((1, nbuf, width), lambda b: (b, 0, 0)),
            pl.BlockSpec((1, nbuf, width), lambda b: (b, 0, 0)),
            full2((n_rows, hd)), full2((n_rows, hd)), full2((n_rows, hd)),
            full2((n_t, LANES)), full2((n_t, LANES)), full2((n_t, LANES)),
            full2((n_rows, 1)),
        ],
        out_specs=[
            pl.BlockSpec((1, n_rows, hd), lambda b: (b, 0, 0)),
            pl.BlockSpec((1, nbuf, width), lambda b: (b, 0, 0)),
            pl.BlockSpec((1, nbuf, width), lambda b: (b, 0, 0)),
        ],
        out_shape=[
            jax.ShapeDtypeStruct((bsz, n_rows, hd), F32),
            jax.ShapeDtypeStruct((bsz, nbuf, width), F32),
            jax.ShapeDtypeStruct((bsz, nbuf, width), F32),
        ],
        scratch_shapes=[pltpu.VMEM((n_all, width), F32), pltpu.VMEM((n_all, width), F32)],
        compiler_params=_params("parallel"),
        name="swa_sample",
    )(q_rows, qkv_s, qkv_s, buf_k, buf_v, *qtab, *ktab, sink_rows)


def _rglru_kernel(gate_ref, u_ref, cb_ref, h0_ref, cw_ref, cbias_ref, wa_ref, ba_ref, wx_ref, bx_ref, lam_ref,
                  y_ref, hlast_ref, cout_ref, uext_ref, a_ref, d_ref, h_ref, *, n_t, n_seq, chunk):
    tc = pl.program_id(1)
    rows = n_t * n_seq
    tail = (cw_ref.shape[0] - 1) * n_seq
    head = uext_ref.shape[0] - rows
    n_blocks, cb = wa_ref.shape[0], wa_ref.shape[1]

    @pl.when(tc == 0)
    def _():
        uext_ref[head - tail:head, :] = cb_ref[0]
        h_ref[...] = h0_ref[0]

    @pl.when(tc > 0)
    def _():
        uext_ref[head - tail:head, :] = uext_ref[head + rows - tail:head + rows, :]

    uext_ref[head:head + rows, :] = u_ref[...]
    log_sig_lam = jax.nn.log_sigmoid(lam_ref[...])
    for c0 in range(0, rows, chunk):
        n = min(chunk, rows - c0)
        xc = cbias_ref[...]
        for i in range(cw_ref.shape[0]):
            xc = xc + uext_ref[head - tail + i * n_seq + c0:head - tail + i * n_seq + c0 + n, :] * cw_ref[i:i + 1, :]
        xb = xc.astype(BF16)
        r = jnp.concatenate([jnp.dot(xb[:, j * cb:(j + 1) * cb], wa_ref[j], preferred_element_type=F32)
                             for j in range(n_blocks)], axis=1)
        ig = jnp.concatenate([jnp.dot(xb[:, j * cb:(j + 1) * cb], wx_ref[j], preferred_element_type=F32)
                              for j in range(n_blocks)], axis=1)
        r = jax.nn.sigmoid(r + ba_ref[...])
        ig = jax.nn.sigmoid(ig + bx_ref[...])
        log_a = LRU_C * r * log_sig_lam
        a_ref[c0:c0 + n, :] = jnp.exp(log_a)
        d_ref[c0:c0 + n, :] = jnp.sqrt(-_expm1(2.0 * log_a)) * (ig * xc)

    def step(t, h):
        sl = pl.ds(pl.multiple_of(t * n_seq, n_seq), n_seq)
        h = a_ref[sl, :] * h + d_ref[sl, :]
        d_ref[sl, :] = h
        return h

    h_last = lax.fori_loop(0, n_t, step, h_ref[...], unroll=min(8, n_t))
    h_ref[...] = h_last
    for c0 in range(0, rows, chunk):
        n = min(chunk, rows - c0)
        y_ref[c0:c0 + n, :] = (d_ref[c0:c0 + n, :] * gate_ref[c0:c0 + n, :]).astype(y_ref.dtype)

    @pl.when(tc == pl.num_programs(1) - 1)
    def _():
        hlast_ref[0] = h_last
        cout_ref[0] = uext_ref[head + rows - tail:head + rows, :]


def rglru(gu, conv_buf, h0, conv_w, conv_b, w_a, b_a, w_x, b_x, lam, *, n_groups, n_t, n_seq, t_chunk, d_rnn):
    assert n_t % t_chunk == 0
    n_tc = n_t // t_chunk
    rows = t_chunk * n_seq
    tail = conv_buf.shape[1]
    assert tail <= rows
    head = -(-tail // 8) * 8
    chunk = min(rows, 256)
    vec = lambda a: a.reshape(1, d_rnn)
    const2 = lambda g, t: (0, 0)
    const3 = lambda g, t: (0, 0, 0)
    per_group = lambda g, t: (g, 0, 0)
    return pl.pallas_call(
        functools.partial(_rglru_kernel, n_t=t_chunk, n_seq=n_seq, chunk=chunk),
        grid=(n_groups, n_tc),
        in_specs=[
            pl.BlockSpec((rows, d_rnn), lambda g, t: (g * n_tc + t, 0)),
            pl.BlockSpec((rows, d_rnn), lambda g, t: (g * n_tc + t, 1)),
            pl.BlockSpec((1, tail, d_rnn), per_group),
            pl.BlockSpec((1, n_seq, d_rnn), per_group),
            pl.BlockSpec(conv_w.shape, const2),
            pl.BlockSpec((1, d_rnn), const2),
            pl.BlockSpec(w_a.shape, const3),
            pl.BlockSpec((1, d_rnn), const2),
            pl.BlockSpec(w_x.shape, const3),
            pl.BlockSpec((1, d_rnn), const2),
            pl.BlockSpec((1, d_rnn), const2),
        ],
        out_specs=[
            pl.BlockSpec((rows, d_rnn), lambda g, t: (g * n_tc + t, 0)),
            pl.BlockSpec((1, n_seq, d_rnn), per_group),
            pl.BlockSpec((1, tail, d_rnn), per_group),
        ],
        out_shape=[
            jax.ShapeDtypeStruct((n_groups * n_t * n_seq, d_rnn), BF16),
            jax.ShapeDtypeStruct((n_groups, n_seq, d_rnn), F32),
            jax.ShapeDtypeStruct((n_groups, tail, d_rnn), F32),
        ],
        scratch_shapes=[
            pltpu.VMEM((head + rows, d_rnn), F32),
            pltpu.VMEM((rows, d_rnn), F32),
            pltpu.VMEM((rows, d_rnn), F32),
            pltpu.VMEM((n_seq, d_rnn), F32),
        ],
        compiler_params=_params("parallel", "arbitrary"),
        name="rglru",
    )(gu, gu, conv_buf, h0, conv_w, vec(conv_b), w_a, vec(b_a), w_x, vec(b_x), vec(lam))


TM = 640
TN = 512
TF = 512
FOX_TQ = 512
FOX_BLK = 256
FOX_PAGES = 16
LRU_T_CHUNK = 256


def kernel(x_prompt, x_sample, cache_fox_k, cache_fox_v, cache_fox_logf, cache_swa_k, cache_swa_v, state_lru_h, state_lru_conv, page_table, norm_mix_pre, norm_mix_post, norm_mlp_pre, norm_mlp_post, mlp_w_up, mlp_w_down, fox_w_qkv, fox_w_f, fox_b_f, fox_w_o, swa_w_qkv, swa_b_qkv, swa_sinks, swa_w_o, swa_b_o, lru_w_gate, lru_b_gate, lru_w_in, lru_b_in, lru_conv_w, lru_conv_b, lru_w_a, lru_b_a, lru_w_x, lru_b_x, lru_lambda, lru_w_out, lru_b_out):
    bp, seq, d = x_prompt.shape
    bs, n_t, _ = x_sample.shape
    mp, ms = bp * seq, bs * n_t
    depth = norm_mix_pre.shape[0]
    fox_heads = fox_w_f.shape[2]
    fox_kv, fox_hd = cache_fox_k.shape[3], cache_fox_k.shape[4]
    swa_heads = swa_sinks.shape[1]
    swa_kv, swa_hd = cache_swa_k.shape[3], cache_swa_k.shape[4]
    d_rnn = lru_w_gate.shape[2]
    conv_w = lru_conv_w.shape[1]
    past = page_table.shape[1] * cache_fox_k.shape[2]
    fox_qw, fox_kw = fox_heads * fox_hd, fox_kv * fox_hd
    swa_qw, swa_kw = swa_heads * swa_hd, swa_kv * swa_hd

    x = jnp.concatenate([x_prompt.reshape(mp, d), x_sample.reshape(ms, d)], axis=0)
    n_layers_fox, pool, page = cache_fox_k.shape[:3]
    ck = cache_fox_k.reshape(n_layers_fox, pool, page * fox_kv, fox_hd)
    cv = cache_fox_v.reshape(n_layers_fox, pool, page * fox_kv, fox_hd)
    clf_t = jnp.swapaxes(cache_fox_logf, 2, 3)

    fkp, fvp, flp, fks, fvs, fls = [], [], [], [], [], []
    skp, svp, sks, svs = [], [], [], []
    lhp, lcp, lhs, lcs = [], [], [], []
    for i in range(depth):
        kind, j = i % 3, i // 3
        if kind == 0:
            qkv, logf = norm_matmul(x, norm_mix_pre[i], fox_w_qkv[j].astype(BF16),
                                    gate=(fox_w_f[j].astype(BF16), fox_b_f[j]), tm=TM, tn=TN)
            kp = qkv[:mp, fox_qw:fox_qw + fox_kw]
            vp = qkv[:mp, fox_qw + fox_kw:]
            fkp.append(kp.reshape(bp, seq, fox_kv, fox_hd))
            fvp.append(vp.reshape(bp, seq, fox_kv, fox_hd))
            flp.append(logf[:mp].reshape(bp, seq, fox_heads))
            qkv_s = qkv[mp:].reshape(bs, n_t, -1)
            lf_s = logf[mp:].reshape(bs, n_t, fox_heads)
            fks.append(qkv_s[:, :, fox_qw:fox_qw + fox_kw].reshape(bs, n_t, fox_kv, fox_hd))
            fvs.append(qkv_s[:, :, fox_qw + fox_kw:].reshape(bs, n_t, fox_kv, fox_hd))
            fls.append(lf_s)
            qx, kx = fox_cumsum(logf[:mp].reshape(bp, seq, fox_heads), n_kv=fox_kv, chunk=FOX_BLK)
            att_p = fox_flash(qkv, qx, kx, batch=bp, seq=seq, n_heads=fox_heads, n_kv=fox_kv, hd=fox_hd,
                              tq=FOX_TQ, blk=FOX_BLK)
            fox_g = fox_heads // fox_kv
            q_rows = qkv_s[:, :, :fox_qw].reshape(bs, n_t, fox_kv, fox_g, fox_hd).transpose(0, 2, 1, 3, 4)
            q_rows = q_rows.reshape(bs, fox_kv, n_t * fox_g, fox_hd)
            lf_new_t = jnp.pad(jnp.swapaxes(lf_s, 1, 2), ((0, 0), (0, 0), (0, page - n_t)))
            att_s = fox_decode(q_rows, qkv_s, lf_new_t, ck, cv, clf_t, page_table, j,
                               n_heads=fox_heads, n_kv=fox_kv, hd=fox_hd, pages=FOX_PAGES)
            att_s = att_s.reshape(bs, fox_kv, n_t, fox_g, fox_hd).transpose(0, 2, 1, 3, 4).reshape(ms, fox_qw)
            att = jnp.concatenate([att_p, att_s.astype(BF16)], axis=0)
            x = matmul_norm_residual(att, fox_w_o[j].astype(BF16), None, norm_mix_post[i], x, tm=TM)
        elif kind == 1:
            qkv = norm_matmul(x, norm_mix_pre[i], swa_w_qkv[j].astype(BF16), swa_b_qkv[j], tm=TM, tn=TN)
            att_p, k_last = swa_prompt(qkv, swa_sinks[j], batch=bp, seq=seq, n_heads=swa_heads, n_kv=swa_kv,
                                       hd=swa_hd, blk=WINDOW)
            keep = min(WINDOW, seq)
            skp.append(k_last.reshape(bp, keep, swa_kv, swa_hd))
            svp.append(qkv[:mp, swa_qw + swa_kw:].reshape(bp, seq, swa_kv, swa_hd)[:, seq - keep:])
            qkv_s = qkv[mp:].reshape(bs, n_t, -1)
            q_rows = qkv_s[:, :, :swa_qw].reshape(bs, n_t * swa_heads, swa_hd)
            nbuf = cache_swa_k.shape[2]
            att_s, nk, nv = swa_sample(q_rows, qkv_s, cache_swa_k[j].reshape(bs, nbuf, swa_kw),
                                       cache_swa_v[j].reshape(bs, nbuf, swa_kw), swa_sinks[j], past,
                                       n_heads=swa_heads, n_kv=swa_kv, hd=swa_hd)
            sks.append(nk.reshape(bs, nbuf, swa_kv, swa_hd))
            svs.append(nv.reshape(bs, nbuf, swa_kv, swa_hd))
            att = jnp.concatenate([att_p, att_s.reshape(ms, swa_qw).astype(BF16)], axis=0)
            x = matmul_norm_residual(att, swa_w_o[j].astype(BF16), swa_b_o[j], norm_mix_post[i], x, tm=TM)
        else:
            w_gu = jnp.concatenate([lru_w_gate[j], lru_w_in[j]], axis=1).astype(BF16)
            b_gu = jnp.concatenate([lru_b_gate[j], lru_b_in[j]])
            gu = norm_matmul(x, norm_mix_pre[i], w_gu, b_gu, n_gelu_cols=d_rnn, tm=TM, tn=TN)
            lw = (lru_conv_w[j], lru_conv_b[j], lru_w_a[j].astype(BF16), lru_b_a[j], lru_w_x[j].astype(BF16),
                  lru_b_x[j], lru_lambda[j])
            y_p, h_p, c_p = rglru(gu, jnp.zeros((bp, conv_w - 1, d_rnn), F32), jnp.zeros((bp, 1, d_rnn), F32),
                                  *lw, n_groups=bp, n_t=seq, n_seq=1, t_chunk=LRU_T_CHUNK, d_rnn=d_rnn)
            lhp.append(h_p.reshape(bp, d_rnn))
            lcp.append(c_p)
            gu_s = gu[mp:].reshape(bs, n_t, -1).swapaxes(0, 1).reshape(ms, -1)
            cb_s = state_lru_conv[j].swapaxes(0, 1).reshape(1, (conv_w - 1) * bs, d_rnn)
            y_s, h_s, c_s = rglru(gu_s, cb_s, state_lru_h[j].reshape(1, bs, d_rnn), *lw,
                                  n_groups=1, n_t=n_t, n_seq=bs, t_chunk=n_t, d_rnn=d_rnn)
            lhs.append(h_s.reshape(bs, d_rnn))
            lcs.append(c_s.reshape(conv_w - 1, bs, d_rnn).swapaxes(0, 1))
            y_s = y_s.reshape(n_t, bs, d_rnn).swapaxes(0, 1).reshape(ms, d_rnn)
            y = jnp.concatenate([y_p, y_s], axis=0)
            x = matmul_norm_residual(y, lru_w_out[j].astype(BF16), lru_b_out[j], norm_mix_post[i], x, tm=TM)
        x = mlp_sublayer(x, norm_mlp_pre[i], mlp_w_up[i].astype(BF16), mlp_w_down[i].astype(BF16),
                         norm_mlp_post[i], tm=TM, tf=TF)
    return (x[:mp].reshape(bp, seq, d), x[mp:].reshape(bs, n_t, d),
            jnp.stack(fkp), jnp.stack(fvp), jnp.stack(flp),
            jnp.stack(fks), jnp.stack(fvs), jnp.stack(fls),
            jnp.stack(skp), jnp.stack(svp), jnp.stack(sks), jnp.stack(svs),
            jnp.stack(lhp), jnp.stack(lcp), jnp.stack(lhs), jnp.stack(lcs))
```

```python
import functools

import jax
import jax.numpy as jnp
from jax import lax
from jax.experimental import pallas as pl
from jax.experimental.pallas import tpu as pltpu

F32 = jnp.float32
BF16 = jnp.bfloat16

RMS_EPS = 1e-6
WINDOW = 128
ROPE_THETA = 500000.0
LRU_C = 8.0
LOG2E = 1.4426950408889634

V7X_VMEM_LIMIT_BYTES = 56 * 1024 * 1024
LANES = 128

NT_DIMS = (((1,), (1,)), ((), ()))


def _params(*sem):
    return pltpu.CompilerParams(dimension_semantics=sem, vmem_limit_bytes=V7X_VMEM_LIMIT_BYTES)


def _rms(x, g):
    ms = jnp.mean(x * x, axis=-1, keepdims=True)
    return x * lax.rsqrt(ms + RMS_EPS) * g


def _expm1(x):
    u = jnp.exp(x)
    um1 = u - 1.0
    safe = jnp.where(u == 1.0, 1.0, jnp.log(u))
    return jnp.where(u == 1.0, x, jnp.where(um1 == -1.0, -1.0, um1 * x / safe))


def _split3(x):
    hi = x.astype(BF16)
    r1 = x - hi.astype(F32)
    mid = r1.astype(BF16)
    lo = (r1 - mid.astype(F32)).astype(BF16)
    return hi, mid, lo


def _norm_mm_kernel(*refs, n_gelu_tiles, has_bias, has_gate):
    it = iter(refs)
    x_ref, g_ref, w_ref = next(it), next(it), next(it)
    b_ref = next(it) if has_bias else None
    wf_ref, bf_ref = (next(it), next(it)) if has_gate else (None, None)
    o_ref = next(it)
    lf_ref = next(it) if has_gate else None
    xn_ref = next(it)
    j = pl.program_id(1)

    @pl.when(j == 0)
    def _():
        xn = _rms(x_ref[...], g_ref[...]).astype(BF16)
        xn_ref[...] = xn
        if has_gate:
            z = jnp.dot(xn, wf_ref[...], preferred_element_type=F32) + bf_ref[...]
            lf_ref[...] = jax.nn.log_sigmoid(z)

    acc = jnp.dot(xn_ref[...], w_ref[...], preferred_element_type=F32)
    if has_bias:
        acc = acc + b_ref[...]
    if n_gelu_tiles == 0:
        o_ref[...] = acc.astype(o_ref.dtype)
    else:
        @pl.when(j < n_gelu_tiles)
        def _():
            o_ref[...] = jax.nn.gelu(acc).astype(o_ref.dtype)

        @pl.when(j >= n_gelu_tiles)
        def _():
            o_ref[...] = acc.astype(o_ref.dtype)


def norm_matmul(x, g, w, b=None, *, gate=None, n_gelu_cols=0, tm, tn):
    m, d = x.shape
    n = w.shape[1]
    assert m % tm == 0 and n % tn == 0 and n_gelu_cols % tn == 0
    has_bias, has_gate = b is not None, gate is not None
    args = [x, g.reshape(1, d), w]
    in_specs = [
        pl.BlockSpec((tm, d), lambda i, j: (i, 0)),
        pl.BlockSpec((1, d), lambda i, j: (0, 0)),
        pl.BlockSpec((d, tn), lambda i, j: (0, j)),
    ]
    if has_bias:
        args.append(b.reshape(1, n))
        in_specs.append(pl.BlockSpec((1, tn), lambda i, j: (0, j)))
    out_shape = [jax.ShapeDtypeStruct((m, n), F32)]
    out_specs = [pl.BlockSpec((tm, tn), lambda i, j: (i, j))]
    if has_gate:
        w_f, b_f = gate
        h = w_f.shape[1]
        args += [w_f, b_f.reshape(1, h)]
        in_specs += [pl.BlockSpec((d, h), lambda i, j: (0, 0)), pl.BlockSpec((1, h), lambda i, j: (0, 0))]
        out_shape.append(jax.ShapeDtypeStruct((m, h), F32))
        out_specs.append(pl.BlockSpec((tm, h), lambda i, j: (i, 0)))
    outs = pl.pallas_call(
        functools.partial(_norm_mm_kernel, n_gelu_tiles=n_gelu_cols // tn, has_bias=has_bias, has_gate=has_gate),
        grid=(m // tm, n // tn),
        in_specs=in_specs,
        out_specs=out_specs,
        out_shape=out_shape,
        scratch_shapes=[pltpu.VMEM((tm, d), BF16)],
        compiler_params=_params("parallel", "arbitrary"),
        name="norm_matmul",
    )(*args)
    return outs if has_gate else outs[0]


def _mm_norm_res_kernel(*refs, has_bias):
    it = iter(refs)
    a_ref, w_ref = next(it), next(it)
    b_ref = next(it) if has_bias else None
    g_ref, x_ref, o_ref = next(it), next(it), next(it)
    mix = jnp.dot(a_ref[...], w_ref[...], preferred_element_type=F32)
    if has_bias:
        mix = mix + b_ref[...]
    o_ref[...] = x_ref[...] + _rms(mix, g_ref[...])


def matmul_norm_residual(a, w, b, g, x, *, tm):
    m, k = a.shape
    d = w.shape[1]
    assert m % tm == 0
    has_bias = b is not None
    args = [a, w]
    in_specs = [pl.BlockSpec((tm, k), lambda i: (i, 0)), pl.BlockSpec((k, d), lambda i: (0, 0))]
    if has_bias:
        args.append(b.reshape(1, d))
        in_specs.append(pl.BlockSpec((1, d), lambda i: (0, 0)))
    args += [g.reshape(1, d), x]
    in_specs += [pl.BlockSpec((1, d), lambda i: (0, 0)), pl.BlockSpec((tm, d), lambda i: (i, 0))]
    return pl.pallas_call(
        functools.partial(_mm_norm_res_kernel, has_bias=has_bias),
        grid=(m // tm,),
        in_specs=in_specs,
        out_specs=pl.BlockSpec((tm, d), lambda i: (i, 0)),
        out_shape=jax.ShapeDtypeStruct((m, d), F32),
        compiler_params=_params("parallel"),
        name="matmul_norm_residual",
    )(*args)


def _mlp_kernel(x_ref, gpre_ref, wup_ref, wdn_ref, gpost_ref, o_ref, xn_ref, acc_ref):
    f = pl.program_id(1)

    @pl.when(f == 0)
    def _():
        xn_ref[...] = _rms(x_ref[...], gpre_ref[...]).astype(BF16)
        acc_ref[...] = jnp.zeros_like(acc_ref)

    h = jnp.dot(xn_ref[...], wup_ref[...], preferred_element_type=F32)
    h = jnp.square(jnp.maximum(h, 0.0)).astype(BF16)
    acc_ref[...] += jnp.dot(h, wdn_ref[...], preferred_element_type=F32)

    @pl.when(f == pl.num_programs(1) - 1)
    def _():
        o_ref[...] = x_ref[...] + _rms(acc_ref[...], gpost_ref[...])


def mlp_sublayer(x, g_pre, w_up, w_down, g_post, *, tm, tf):
    m, d = x.shape
    dff = w_up.shape[1]
    assert m % tm == 0 and dff % tf == 0
    return pl.pallas_call(
        _mlp_kernel,
        grid=(m // tm, dff // tf),
        in_specs=[
            pl.BlockSpec((tm, d), lambda i, f: (i, 0)),
            pl.BlockSpec((1, d), lambda i, f: (0, 0)),
            pl.BlockSpec((d, tf), lambda i, f: (0, f)),
            pl.BlockSpec((tf, d), lambda i, f: (f, 0)),
            pl.BlockSpec((1, d), lambda i, f: (0, 0)),
        ],
        out_specs=pl.BlockSpec((tm, d), lambda i, f: (i, 0)),
        out_shape=jax.ShapeDtypeStruct((m, d), F32),
        scratch_shapes=[pltpu.VMEM((tm, d), BF16), pltpu.VMEM((tm, d), F32)],
        compiler_params=_params("parallel", "arbitrary"),
        name="mlp_sublayer",
    )(x, g_pre.reshape(1, d), w_up, w_down, g_post.reshape(1, d))


N_PARTS = 3


def _fox_cumsum_kernel(lf_ref, qx_ref, kx_ref, *, chunk, n_kv, n_g):
    s, h = lf_ref.shape[1], lf_ref.shape[2]
    row = lax.broadcasted_iota(jnp.int32, (chunk, chunk), 0)
    col = lax.broadcasted_iota(jnp.int32, (chunk, chunk), 1)
    lower = (row >= col).astype(BF16)

    def placement(n_tiles, lane_of, head_of):
        r = lax.broadcasted_iota(jnp.int32, (h, n_tiles * LANES), 0)
        c = lax.broadcasted_iota(jnp.int32, (h, n_tiles * LANES), 1)
        tile, lane = c // LANES, c % LANES
        mats = []
        for p in range(N_PARTS):
            hit = jnp.zeros((h, n_tiles * LANES), jnp.bool_)
            for g in range(n_g):
                hit = hit | ((r == head_of(tile, g)) & (lane == lane_of(g, p)))
            mats.append(hit)
        return mats, tile, lane

    q_hit, q_tile, q_lane = placement(h, lambda g, p: p, lambda tile, g: jnp.where(tile % n_g == g, tile, -1))
    q_mats = [m.astype(BF16) for m in q_hit]
    q_gl = N_PARTS * (1 + q_tile[0:1] % n_g)
    q_ones = ((q_lane[0:1] >= q_gl) & (q_lane[0:1] < q_gl + N_PARTS)).astype(F32)
    k_hit, _, k_lane = placement(n_kv, lambda g, p: N_PARTS * (1 + g) + p, lambda tile, g: tile * n_g + g)
    k_mats = [(-m.astype(F32)).astype(BF16) for m in k_hit]
    k_ones = (k_lane[0:1] < N_PARTS).astype(F32)

    carry = jnp.zeros((1, h), F32)
    for c in range(s // chunk):
        rows = slice(c * chunk, (c + 1) * chunk)
        hi, mid, lo = _split3(lf_ref[0, rows, :])
        cs = (jnp.dot(lower, hi, preferred_element_type=F32)
              + jnp.dot(lower, mid, preferred_element_type=F32)
              + jnp.dot(lower, lo, preferred_element_type=F32)) + carry
        carry = cs[chunk - 1:chunk, :]
        parts = _split3(cs * LOG2E)
        qx = q_ones + sum(jnp.dot(parts[p], q_mats[p], preferred_element_type=F32) for p in range(N_PARTS))
        kx = k_ones + sum(jnp.dot(parts[p], k_mats[p], preferred_element_type=F32) for p in range(N_PARTS))
        for hh in range(h):
            qx_ref[0, hh, rows, :] = qx[:, hh * LANES:(hh + 1) * LANES].astype(BF16)
        for kv in range(n_kv):
            kx_ref[0, kv, rows, :] = kx[:, kv * LANES:(kv + 1) * LANES].astype(BF16)


def fox_cumsum(logf, *, n_kv, chunk):
    b, s, h = logf.shape
    n_g = h // n_kv
    assert N_PARTS * (1 + n_g) <= LANES
    return pl.pallas_call(
        functools.partial(_fox_cumsum_kernel, chunk=chunk, n_kv=n_kv, n_g=n_g),
        grid=(b,),
        in_specs=[pl.BlockSpec((1, s, h), lambda i: (i, 0, 0))],
        out_specs=[
            pl.BlockSpec((1, h, s, LANES), lambda i: (i, 0, 0, 0)),
            pl.BlockSpec((1, n_kv, s, LANES), lambda i: (i, 0, 0, 0)),
        ],
        out_shape=[
            jax.ShapeDtypeStruct((b, h, s, LANES), BF16),
            jax.ShapeDtypeStruct((b, n_kv, s, LANES), BF16),
        ],
        compiler_params=_params("parallel"),
        name="fox_cumsum",
    )(logf)


def _lane_tiles(x):
    return [x[:, j * LANES:(j + 1) * LANES] for j in range(x.shape[1] // LANES)]


def _fox_flash_kernel(q_ref, k_ref, v_ref, qx_ref, kx_ref, o_ref, ka_ref, vb_ref, s_ref, m_ref, l_ref, acc_ref,
                      *, tq, blk, n_g, hd):
    qi = pl.program_id(2)
    per_q = tq // blk

    @pl.when(qi == 0)
    def _():
        ka_ref[:, :hd] = k_ref[...].astype(BF16)
        ka_ref[:, hd:] = kx_ref[0, 0]
        vb_ref[...] = v_ref[...].astype(BF16)

    q_scale = hd ** -0.5 * LOG2E
    qa = [jnp.concatenate([(q_ref[:, g * hd:(g + 1) * hd] * q_scale).astype(BF16), qx_ref[0, g]], axis=1)
          for g in range(n_g)]
    m_ref[...] = jnp.full_like(m_ref, -jnp.inf)

    def logits_pass(kc, diag):
        ka = ka_ref[pl.ds(pl.multiple_of(kc * blk, blk), blk), :]
        if diag is not None:
            row = lax.broadcasted_iota(jnp.int32, (tq, blk), 0)
            col = lax.broadcasted_iota(jnp.int32, (tq, blk), 1) + diag * blk
            causal = row >= col
        for g in range(n_g):
            s = lax.dot_general(qa[g], ka, NT_DIMS, preferred_element_type=F32)
            if diag is not None:
                s = jnp.where(causal, s, -jnp.inf)
            s_ref[g, kc] = s
            m = m_ref[g]
            for t in _lane_tiles(s):
                m = jnp.maximum(m, t)
            m_ref[g] = m

    def body1(kc, carry):
        logits_pass(kc, None)
        return carry

    lax.fori_loop(0, per_q * qi, body1, 0)
    for e in range(per_q):
        logits_pass(per_q * qi + e, e)
    for g in range(n_g):
        m_ref[g] = jnp.broadcast_to(jnp.max(m_ref[g], axis=1, keepdims=True), (tq, LANES))
    l_ref[...] = jnp.zeros_like(l_ref)
    acc_ref[...] = jnp.zeros_like(acc_ref)

    def body2(kc, carry):
        vb = vb_ref[pl.ds(pl.multiple_of(kc * blk, blk), blk), :]
        for g in range(n_g):
            m = m_ref[g]
            p = [jnp.exp2(t - m) for t in _lane_tiles(s_ref[g, kc])]
            l_ref[g] += sum(p)
            acc_ref[g] += jnp.dot(jnp.concatenate(p, axis=1).astype(BF16), vb, preferred_element_type=F32)
        return carry

    lax.fori_loop(0, per_q * (qi + 1), body2, 0)
    for g in range(n_g):
        inv = 1.0 / jnp.sum(l_ref[g], axis=1, keepdims=True)
        o_ref[:, g * hd:(g + 1) * hd] = (acc_ref[g] * inv).astype(o_ref.dtype)


def fox_flash(qkv, qx, kx, *, batch, seq, n_heads, n_kv, hd, tq, blk):
    assert hd == LANES and tq % blk == 0 and seq % tq == 0
    n_g = n_heads // n_kv
    nq = seq // tq
    gw = n_g * hd
    return pl.pallas_call(
        functools.partial(_fox_flash_kernel, tq=tq, blk=blk, n_g=n_g, hd=hd),
        grid=(batch, n_kv, nq),
        in_specs=[
            pl.BlockSpec((tq, gw), lambda b, kv, qi: (b * nq + qi, kv)),
            pl.BlockSpec((seq, hd), lambda b, kv, qi: (b, n_heads + kv)),
            pl.BlockSpec((seq, hd), lambda b, kv, qi: (b, n_heads + n_kv + kv)),
            pl.BlockSpec((1, n_g, tq, LANES), lambda b, kv, qi: (b, kv, qi, 0)),
            pl.BlockSpec((1, 1, seq, LANES), lambda b, kv, qi: (b, kv, 0, 0)),
        ],
        out_specs=pl.BlockSpec((tq, gw), lambda b, kv, qi: (b * nq + qi, kv)),
        out_shape=jax.ShapeDtypeStruct((batch * seq, n_heads * hd), BF16),
        scratch_shapes=[
            pltpu.VMEM((seq, hd + LANES), BF16),
            pltpu.VMEM((seq, hd), BF16),
            pltpu.VMEM((n_g, seq // blk, tq, blk), F32),
            pltpu.VMEM((n_g, tq, LANES), F32),
            pltpu.VMEM((n_g, tq, LANES), F32),
            pltpu.VMEM((n_g, tq, hd), F32),
        ],
        compiler_params=_params("parallel", "parallel", "arbitrary"),
        name="fox_flash",
    )(qkv, qkv, qkv, qx, kx)


def _fox_decode_kernel(pt_ref, q_ref, knew_ref, vnew_ref, lfnew_ref, *rest, pages, n_kv, n_heads, hd, n_t):
    del pt_ref
    k_refs, v_refs, lf_refs = rest[:pages], rest[pages:2 * pages], rest[2 * pages:3 * pages]
    o_ref, cq_ref, m_ref, l_ref, acc_ref, carry_ref, kpad_ref, vpad_ref = rest[3 * pages:]
    c = pl.program_id(1)
    scale = hd ** -0.5
    page = lf_refs[0].shape[3]
    n_g = n_heads // n_kv
    grp = n_t * n_g
    jj = lax.broadcasted_iota(jnp.int32, (page, page), 0)
    ss = lax.broadcasted_iota(jnp.int32, (page, page), 1)

    def lane_sums(xs, tri):
        stacked = jnp.concatenate([part for x in xs for part in _split3(x)], axis=0)
        out = jnp.dot(stacked, tri, preferred_element_type=F32)
        n = N_PARTS * n_heads
        return [out[i * n:i * n + n_heads] + out[i * n + n_heads:i * n + 2 * n_heads]
                + out[i * n + 2 * n_heads:(i + 1) * n] for i in range(len(xs))]

    def group_rows(x, kv):
        return jnp.concatenate([x[kv * n_g:(kv + 1) * n_g]] * n_t, axis=0)

    q = [q_ref[0, kv].astype(BF16) for kv in range(n_kv)]

    @pl.when(c == 0)
    def _():
        c_new = lane_sums([lfnew_ref[0]], (jj <= ss).astype(BF16))[0]
        kpad_ref[...] = jnp.zeros_like(kpad_ref)
        vpad_ref[...] = jnp.zeros_like(vpad_ref)
        tok = lax.broadcasted_iota(jnp.int32, (grp, page), 0) // n_g
        key = lax.broadcasted_iota(jnp.int32, (grp, page), 1)
        for kv in range(n_kv):
            cq = jnp.concatenate([c_new[kv * n_g:(kv + 1) * n_g, t:t + 1] for t in range(n_t)], axis=0)
            cq_ref[kv] = cq
            kpad_ref[kv, 0:n_t, :] = knew_ref[0, :, kv * hd:(kv + 1) * hd]
            vpad_ref[kv, 0:n_t, :] = vnew_ref[0, :, kv * hd:(kv + 1) * hd]
            s = lax.dot_general(q[kv], kpad_ref[kv].astype(BF16), NT_DIMS, preferred_element_type=F32) * scale
            s = s + (cq - group_rows(c_new, kv))
            s = jnp.where(key <= tok, s, -jnp.inf)
            m = jnp.max(s, axis=1, keepdims=True)
            p = jnp.exp(s - m)
            m_ref[kv] = m
            l_ref[kv] = jnp.sum(p, axis=1, keepdims=True)
            acc_ref[kv] = jnp.dot(p.astype(BF16), vpad_ref[kv].astype(BF16), preferred_element_type=F32)
        carry_ref[...] = jnp.zeros_like(carry_ref)

    lfs = [lf_refs[i][0, 0] for i in range(pages)]
    local = lane_sums(lfs, (jj > ss).astype(BF16))
    carry = carry_ref[...]
    rs = []
    for i in range(pages):
        rs.append(local[i] + carry)
        carry = carry + (local[i][:, 0:1] + lfs[i][:, 0:1])
    carry_ref[...] = carry
    raw = [[lax.dot_general(q[kv], k_refs[i][0, 0, pl.ds(kv, page, stride=n_kv), :].astype(BF16), NT_DIMS,
                            preferred_element_type=F32) for i in range(pages)] for kv in range(n_kv)]
    probs, alphas = [], []
    for kv in range(n_kv):
        cq = cq_ref[kv]
        logits = [raw[kv][i] * scale + (cq + group_rows(rs[i], kv)) for i in range(pages)]
        m_old = m_ref[kv]
        tile_max = logits[0]
        for s in logits[1:]:
            tile_max = jnp.maximum(tile_max, s)
        m_new = jnp.maximum(m_old, jnp.max(tile_max, axis=1, keepdims=True))
        alpha = jnp.exp(m_old - m_new)
        ps = [jnp.exp(s - m_new) for s in logits]
        l_ref[kv] = alpha * l_ref[kv] + jnp.sum(sum(ps), axis=1, keepdims=True)
        m_ref[kv] = m_new
        probs.append([p.astype(BF16) for p in ps])
        alphas.append(alpha)
    for kv in range(n_kv):
        pv = sum(jnp.dot(probs[kv][i], v_refs[i][0, 0, pl.ds(kv, page, stride=n_kv), :].astype(BF16),
                         preferred_element_type=F32) for i in range(pages))
        acc_ref[kv] = alphas[kv] * acc_ref[kv] + pv

    @pl.when(c == pl.num_programs(1) - 1)
    def _():
        for kv in range(n_kv):
            o_ref[0, kv] = acc_ref[kv] / l_ref[kv]


def fox_decode(q_rows, qkv_s, lf_new_t, cache_k, cache_v, cache_lf_t, page_table, layer, *, n_heads, n_kv, hd, pages):
    bsz, n_t = qkv_s.shape[0], qkv_s.shape[1]
    n_pages = page_table.shape[1]
    page = cache_lf_t.shape[3]
    width = n_kv * hd
    grp = n_t * (n_heads // n_kv)
    assert n_pages % pages == 0 and page == LANES and cache_k.shape[2] == page * n_kv
    n_chunks = n_pages // pages

    def page_map(i):
        return lambda b, c, pt: (layer, pt[b, n_pages - 1 - (c * pages + i)], 0, 0)

    kv_col = n_heads * hd // width
    in_specs = [
        pl.BlockSpec((1, n_kv, grp, hd), lambda b, c, pt: (b, 0, 0, 0)),
        pl.BlockSpec((1, n_t, width), lambda b, c, pt: (b, 0, kv_col)),
        pl.BlockSpec((1, n_t, width), lambda b, c, pt: (b, 0, kv_col + 1)),
        pl.BlockSpec((1, n_heads, page), lambda b, c, pt: (b, 0, 0)),
    ]
    in_specs += [pl.BlockSpec((1, 1, page * n_kv, hd), page_map(i)) for i in range(pages)]
    in_specs += [pl.BlockSpec((1, 1, page * n_kv, hd), page_map(i)) for i in range(pages)]
    in_specs += [pl.BlockSpec((1, 1, n_heads, page), page_map(i)) for i in range(pages)]
    grid_spec = pltpu.PrefetchScalarGridSpec(
        num_scalar_prefetch=1,
        grid=(bsz, n_chunks),
        in_specs=in_specs,
        out_specs=pl.BlockSpec((1, n_kv, grp, hd), lambda b, c, pt: (b, 0, 0, 0)),
        scratch_shapes=[
            pltpu.VMEM((n_kv, grp, 1), F32),
            pltpu.VMEM((n_kv, grp, 1), F32),
            pltpu.VMEM((n_kv, grp, 1), F32),
            pltpu.VMEM((n_kv, grp, hd), F32),
            pltpu.VMEM((n_heads, 1), F32),
            pltpu.VMEM((n_kv, page, hd), F32),
            pltpu.VMEM((n_kv, page, hd), F32),
        ],
    )
    return pl.pallas_call(
        functools.partial(_fox_decode_kernel, pages=pages, n_kv=n_kv, n_heads=n_heads, hd=hd, n_t=n_t),
        grid_spec=grid_spec,
        out_shape=jax.ShapeDtypeStruct((bsz, n_kv, grp, hd), F32),
        compiler_params=_params("parallel", "arbitrary"),
        name="fox_decode",
    )(page_table, q_rows, qkv_s, qkv_s, lf_new_t, *([cache_k] * pages), *([cache_v] * pages), *([cache_lf_t] * pages))


def _rope_tables(pos, hd, reps):
    rot = hd // 4
    half = rot // 2
    inv_freq = ROPE_THETA ** (-jnp.arange(half, dtype=F32) / half)
    ang = pos.astype(F32)[:, None] * inv_freq[None, :]
    cos, sin = jnp.cos(ang), jnp.sin(ang)
    n = pos.shape[0]
    ones = jnp.ones((n, hd - rot), F32)
    zeros_h = jnp.zeros((n, half), F32)
    zeros_r = jnp.zeros((n, hd - rot), F32)
    c = jnp.concatenate([cos, cos, ones], axis=1)
    s_up = jnp.concatenate([-sin, zeros_h, zeros_r], axis=1)
    s_dn = jnp.concatenate([zeros_h, sin, zeros_r], axis=1)
    return tuple(jnp.tile(t, (1, reps)) for t in (c, s_up, s_dn))


def _rope(x, c, s_up, s_dn, half):
    w = x.shape[1]
    pieces = []
    for j in range(w // LANES):
        xj = x[:, j * LANES:(j + 1) * LANES]
        up = pltpu.roll(xj, LANES - half, 1)
        dn = pltpu.roll(xj, half, 1)
        pieces.append(xj * c + up * s_up + dn * s_dn)
    return pieces[0] if len(pieces) == 1 else jnp.concatenate(pieces, axis=1)


def _swa_prompt_kernel(q_ref, kc_ref, kp_ref, vc_ref, vp_ref, cc_ref, su_ref, sd_ref, pc_ref, pu_ref, pd_ref,
                       sink_ref, o_ref, klast_ref, *, n_heads, n_kv, hd, blk):
    qb = pl.program_id(1)
    half = hd // 8
    n_g = n_heads // n_kv
    per_tile = LANES // hd
    assert per_tile == 2 and n_g % per_tile == 0
    nk = 2 * blk
    q = _rope(q_ref[...], cc_ref[...], su_ref[...], sd_ref[...], half) * (hd ** -0.5 * LOG2E)
    kc = _rope(kc_ref[...], cc_ref[...], su_ref[...], sd_ref[...], half)
    kp = _rope(kp_ref[...], pc_ref[...], pu_ref[...], pd_ref[...], half)
    k = jnp.concatenate([kp, kc], axis=0)
    v = jnp.concatenate([vp_ref[...], vc_ref[...]], axis=0)
    lane = lax.broadcasted_iota(jnp.int32, (nk, LANES), 1)
    d_row = lax.broadcasted_iota(jnp.int32, (LANES, nk), 0)
    out_row = lax.broadcasted_iota(jnp.int32, (LANES, blk), 0)

    def spread_keys(kv):
        tile = k[:, (kv // per_tile) * LANES:(kv // per_tile + 1) * LANES]
        swapped = pltpu.roll(tile, hd, 1)
        low, high = (tile, swapped) if kv % per_tile == 0 else (swapped, tile)
        return jnp.concatenate([jnp.where(lane < hd, low, 0.0), jnp.where(lane >= hd, high, 0.0)],
                               axis=0).astype(BF16)

    def spread_values_t(kv):
        tile_t = v[:, (kv // per_tile) * LANES:(kv // per_tile + 1) * LANES].T
        swapped_t = jnp.concatenate([tile_t[hd:], tile_t[:hd]], axis=0)
        low, high = (tile_t, swapped_t) if kv % per_tile == 0 else (swapped_t, tile_t)
        return jnp.concatenate([jnp.where(d_row < hd, low, 0.0), jnp.where(d_row >= hd, high, 0.0)],
                               axis=1).astype(BF16)

    j = lax.broadcasted_iota(jnp.int32, (nk, blk), 0)
    t = lax.broadcasted_iota(jnp.int32, (nk, blk), 1)
    first_key = jnp.where(qb > 0, 0, blk)
    band = (j > t) & (j <= t + blk) & (j >= first_key)
    pairs = n_g // per_tile
    n_tiles = n_kv * pairs
    k2 = [spread_keys(kv) for kv in range(n_kv)]
    raw = [lax.dot_general(k2[i // pairs], q[:, i * LANES:(i + 1) * LANES].astype(BF16), NT_DIMS,
                           preferred_element_type=F32) for i in range(n_tiles)]
    weights, scales = [], []
    for i in range(n_tiles):
        es, invs = [], []
        for e in range(per_tile):
            h = per_tile * i + e
            se = jnp.where(band, raw[i][e * nk:(e + 1) * nk], -jnp.inf)
            sink = sink_ref[0:1, h:h + 1] * LOG2E
            m = jnp.maximum(jnp.max(se, axis=0, keepdims=True), sink)
            ee = jnp.exp2(se - m)
            es.append(ee)
            invs.append(1.0 / (jnp.sum(ee, axis=0, keepdims=True) + jnp.exp2(sink - m)))
        weights.append(jnp.concatenate(es, axis=0).astype(BF16))
        scales.append(jnp.where(out_row < hd, invs[0], invs[1]))
    v2t = [spread_values_t(kv) for kv in range(n_kv)]
    for i in range(n_tiles):
        out_t = jnp.dot(v2t[i // pairs], weights[i], preferred_element_type=F32) * scales[i]
        o_ref[:, i * LANES:(i + 1) * LANES] = out_t.T.astype(o_ref.dtype)

    @pl.when(qb == pl.num_programs(1) - 1)
    def _():
        klast_ref[0] = kc


def swa_prompt(qkv, sinks, *, batch, seq, n_heads, n_kv, hd, blk):
    assert blk == WINDOW and (n_kv * hd) % LANES == 0
    nb = seq // blk
    qw, kw = n_heads * hd, n_kv * hd
    kcol = qw // kw
    pos = jnp.arange(seq)
    reps = LANES // hd
    cur = _rope_tables(pos, hd, reps)
    prev = _rope_tables(pos - blk, hd, reps)

    def cur_row(b, qb):
        return b * nb + qb

    def prev_row(b, qb):
        return b * nb + jnp.maximum(qb - 1, 0)

    tab_cur = pl.BlockSpec((blk, LANES), lambda b, qb: (qb, 0))
    tab_prev = pl.BlockSpec((blk, LANES), lambda b, qb: (qb, 0))
    return pl.pallas_call(
        functools.partial(_swa_prompt_kernel, n_heads=n_heads, n_kv=n_kv, hd=hd, blk=blk),
        grid=(batch, nb),
        in_specs=[
            pl.BlockSpec((blk, qw), lambda b, qb: (cur_row(b, qb), 0)),
            pl.BlockSpec((blk, kw), lambda b, qb: (cur_row(b, qb), kcol)),
            pl.BlockSpec((blk, kw), lambda b, qb: (prev_row(b, qb), kcol)),
            pl.BlockSpec((blk, kw), lambda b, qb: (cur_row(b, qb), kcol + 1)),
            pl.BlockSpec((blk, kw), lambda b, qb: (prev_row(b, qb), kcol + 1)),
            tab_cur, tab_cur, tab_cur, tab_prev, tab_prev, tab_prev,
            pl.BlockSpec((1, n_heads), lambda b, qb: (0, 0)),
        ],
        out_specs=[
            pl.BlockSpec((blk, qw), lambda b, qb: (cur_row(b, qb), 0)),
            pl.BlockSpec((1, blk, kw), lambda b, qb: (b, 0, 0)),
        ],
        out_shape=[
            jax.ShapeDtypeStruct((batch * seq, qw), BF16),
            jax.ShapeDtypeStruct((batch, blk, kw), F32),
        ],
        compiler_params=_params("parallel", "arbitrary"),
        name="swa_prompt",
    )(qkv, qkv, qkv, qkv, qkv, *cur, *prev, sinks.reshape(1, n_heads))


def _swa_sample_kernel(q_ref, knew_ref, vnew_ref, bk_ref, bv_ref, qc_ref, qu_ref, qd_ref, kc_ref, ku_ref, kd_ref,
                       sink_ref, o_ref, ok_ref, ov_ref, kall_ref, vall_ref, *, n_heads, n_kv, hd, n_t):
    scale = hd ** -0.5
    half = hd // 8
    n_g = n_heads // n_kv
    n_rows = n_t * n_heads
    nbuf = bk_ref.shape[1]
    width = n_kv * hd
    n_all = kall_ref.shape[0]
    knew = _rope(knew_ref[0], kc_ref[...], ku_ref[...], kd_ref[...], half)
    kall_ref[...] = jnp.zeros_like(kall_ref)
    vall_ref[...] = jnp.zeros_like(vall_ref)
    kall_ref[0:nbuf, :] = bk_ref[0]
    vall_ref[0:nbuf, :] = bv_ref[0]
    kall_ref[nbuf:nbuf + n_t, :] = knew
    vall_ref[nbuf:nbuf + n_t, :] = vnew_ref[0]
    ok_ref[0] = kall_ref[n_t:n_t + nbuf, :]
    ov_ref[0] = vall_ref[n_t:n_t + nbuf, :]
    q = q_ref[0]
    up = jnp.concatenate([q[:, half:], q[:, :half]], axis=1)
    dn = jnp.concatenate([q[:, hd - half:], q[:, :hd - half]], axis=1)
    q = q * qc_ref[...] + up * qu_ref[...] + dn * qd_ref[...]
    qt = jnp.concatenate([q] * n_kv, axis=1)
    row_kv = (lax.broadcasted_iota(jnp.int32, (n_rows, width), 0) % n_heads) // n_g
    lane_kv = lax.broadcasted_iota(jnp.int32, (n_rows, width), 1) // hd
    qbd = jnp.where(row_kv == lane_kv, qt, 0.0).astype(BF16)
    s = lax.dot_general(qbd, kall_ref[...].astype(BF16), NT_DIMS, preferred_element_type=F32) * scale
    tok = lax.broadcasted_iota(jnp.int32, (n_rows, n_all), 0) // n_heads
    key = lax.broadcasted_iota(jnp.int32, (n_rows, n_all), 1)
    band = ((key < nbuf) & (tok + nbuf - key < WINDOW)) | ((key >= nbuf) & (key - nbuf <= tok))
    s = jnp.where(band, s, -jnp.inf)
    sink = sink_ref[...]
    m = jnp.maximum(jnp.max(s, axis=1, keepdims=True), sink)
    e = jnp.exp(s - m)
    p = e / (jnp.sum(e, axis=1, keepdims=True) + jnp.exp(sink - m))
    out = jnp.dot(p.astype(BF16), vall_ref[...].astype(BF16), preferred_element_type=F32)
    row_kv = (lax.broadcasted_iota(jnp.int32, (n_rows, hd), 0) % n_heads) // n_g
    res = jnp.zeros((n_rows, hd), F32)
    for kv in range(n_kv):
        res = jnp.where(row_kv == kv, out[:, kv * hd:(kv + 1) * hd], res)
    o_ref[0] = res


def swa_sample(q_rows, qkv_s, buf_k, buf_v, sinks, past, *, n_heads, n_kv, hd):
    bsz, n_t = qkv_s.shape[0], qkv_s.shape[1]
    nbuf = buf_k.shape[1]
    assert nbuf == WINDOW and n_t <= 8
    width = n_kv * hd
    n_rows = n_t * n_heads
    n_all = nbuf + LANES
    pos = past + jnp.arange(n_t)
    qtab = tuple(jnp.repeat(t, n_heads, axis=0) for t in _rope_tables(pos, hd, 1))
    ktab = _rope_tables(pos, hd, LANES // hd)
    sink_rows = jnp.tile(sinks.astype(F32), n_t).reshape(n_rows, 1)
    kcol = n_heads * hd // width
    full2 = lambda shape: pl.BlockSpec(shape, lambda b: (0, 0))
    return pl.pallas_call(
        functools.partial(_swa_sample_kernel, n_heads=n_heads, n_kv=n_kv, hd=hd, n_t=n_t),
        grid=(bsz,),
        in_specs=[
            pl.BlockSpec((1, n_rows, hd), lambda b: (b, 0, 0)),
            pl.BlockSpec((1, n_t, width), lambda b: (b, 0, kcol)),
            pl.BlockSpec((1, n_t, width), lambda b: (b, 0, kcol + 1)),
            pl.BlockSpec((1, nbuf, width), lambda b: (b, 0, 0)),
            pl.BlockSpec((1, nbuf, width), lambda b: (b, 0, 0)),
            full2((n_rows, hd)), full2((n_rows, hd)), full2((n_rows, hd)),
            full2((n_t, LANES)), full2((n_t, LANES)), full2((n_t, LANES)),
            full2((n_rows, 1)),
        ],
        out_specs=[
            pl.BlockSpec((1, n_rows, hd), lambda b: (b, 0, 0)),
            pl.BlockSpec((1, nbuf, width), lambda b: (b, 0, 0)),
            pl.BlockSpec((1, nbuf, width), lambda b: (b, 0, 0)),
        ],
        out_shape=[
            jax.ShapeDtypeStruct((bsz, n_rows, hd), F32),
            jax.ShapeDtypeStruct((bsz, nbuf, width), F32),
            jax.ShapeDtypeStruct((bsz, nbuf, width), F32),
        ],
        scratch_shapes=[pltpu.VMEM((n_all, width), F32), pltpu.VMEM((n_all, width), F32)],
        compiler_params=_params("parallel"),
        name="swa_sample",
    )(q_rows, qkv_s, qkv_s, buf_k, buf_v, *qtab, *ktab, sink_rows)


def _rglru_kernel(gate_ref, u_ref, cb_ref, h0_ref, cw_ref, cbias_ref, wa_ref, ba_ref, wx_ref, bx_ref, lam_ref,
                  y_ref, hlast_ref, cout_ref, uext_ref, a_ref, d_ref, h_ref, *, n_t, n_seq, chunk):
    tc = pl.program_id(1)
    rows = n_t * n_seq
    tail = (cw_ref.shape[0] - 1) * n_seq
    head = uext_ref.shape[0] - rows
    n_blocks, cb = wa_ref.shape[0], wa_ref.shape[1]

    @pl.when(tc == 0)
    def _():
        uext_ref[head - tail:head, :] = cb_ref[0]
        h_ref[...] = h0_ref[0]

    @pl.when(tc > 0)
    def _():
        uext_ref[head - tail:head, :] = uext_ref[head + rows - tail:head + rows, :]

    uext_ref[head:head + rows, :] = u_ref[...]
    log_sig_lam = jax.nn.log_sigmoid(lam_ref[...])
    for c0 in range(0, rows, chunk):
        n = min(chunk, rows - c0)
        xc = cbias_ref[...]
        for i in range(cw_ref.shape[0]):
            xc = xc + uext_ref[head - tail + i * n_seq + c0:head - tail + i * n_seq + c0 + n, :] * cw_ref[i:i + 1, :]
        xb = xc.astype(BF16)
        r = jnp.concatenate([jnp.dot(xb[:, j * cb:(j + 1) * cb], wa_ref[j], preferred_element_type=F32)
                             for j in range(n_blocks)], axis=1)
        ig = jnp.concatenate([jnp.dot(xb[:, j * cb:(j + 1) * cb], wx_ref[j], preferred_element_type=F32)
                              for j in range(n_blocks)], axis=1)
        r = jax.nn.sigmoid(r + ba_ref[...])
        ig = jax.nn.sigmoid(ig + bx_ref[...])
        log_a = LRU_C * r * log_sig_lam
        a_ref[c0:c0 + n, :] = jnp.exp(log_a)
        d_ref[c0:c0 + n, :] = jnp.sqrt(-_expm1(2.0 * log_a)) * (ig * xc)

    def step(t, h):
        sl = pl.ds(pl.multiple_of(t * n_seq, n_seq), n_seq)
        h = a_ref[sl, :] * h + d_ref[sl, :]
        d_ref[sl, :] = h
        return h

    h_last = lax.fori_loop(0, n_t, step, h_ref[...], unroll=min(8, n_t))
    h_ref[...] = h_last
    for c0 in range(0, rows, chunk):
        n = min(chunk, rows - c0)
        y_ref[c0:c0 + n, :] = (d_ref[c0:c0 + n, :] * gate_ref[c0:c0 + n, :]).astype(y_ref.dtype)

    @pl.when(tc == pl.num_programs(1) - 1)
    def _():
        hlast_ref[0] = h_last
        cout_ref[0] = uext_ref[head + rows - tail:head + rows, :]


def rglru(gu, conv_buf, h0, conv_w, conv_b, w_a, b_a, w_x, b_x, lam, *, n_groups, n_t, n_seq, t_chunk, d_rnn):
    assert n_t % t_chunk == 0
    n_tc = n_t // t_chunk
    rows = t_chunk * n_seq
    tail = conv_buf.shape[1]
    assert tail <= rows
    head = -(-tail // 8) * 8
    chunk = min(rows, 256)
    vec = lambda a: a.reshape(1, d_rnn)
    const2 = lambda g, t: (0, 0)
    const3 = lambda g, t: (0, 0, 0)
    per_group = lambda g, t: (g, 0, 0)
    return pl.pallas_call(
        functools.partial(_rglru_kernel, n_t=t_chunk, n_seq=n_seq, chunk=chunk),
        grid=(n_groups, n_tc),
        in_specs=[
            pl.BlockSpec((rows, d_rnn), lambda g, t: (g * n_tc + t, 0)),
            pl.BlockSpec((rows, d_rnn), lambda g, t: (g * n_tc + t, 1)),
            pl.BlockSpec((1, tail, d_rnn), per_group),
            pl.BlockSpec((1, n_seq, d_rnn), per_group),
            pl.BlockSpec(conv_w.shape, const2),
            pl.BlockSpec((1, d_rnn), const2),
            pl.BlockSpec(w_a.shape, const3),
            pl.BlockSpec((1, d_rnn), const2),
            pl.BlockSpec(w_x.shape, const3),
            pl.BlockSpec((1, d_rnn), const2),
            pl.BlockSpec((1, d_rnn), const2),
        ],
        out_specs=[
            pl.BlockSpec((rows, d_rnn), lambda g, t: (g * n_tc + t, 0)),
            pl.BlockSpec((1, n_seq, d_rnn), per_group),
            pl.BlockSpec((1, tail, d_rnn), per_group),
        ],
        out_shape=[
            jax.ShapeDtypeStruct((n_groups * n_t * n_seq, d_rnn), BF16),
            jax.ShapeDtypeStruct((n_groups, n_seq, d_rnn), F32),
            jax.ShapeDtypeStruct((n_groups, tail, d_rnn), F32),
        ],
        scratch_shapes=[
            pltpu.VMEM((head + rows, d_rnn), F32),
            pltpu.VMEM((rows, d_rnn), F32),
            pltpu.VMEM((rows, d_rnn), F32),
            pltpu.VMEM((n_seq, d_rnn), F32),
        ],
        compiler_params=_params("parallel", "arbitrary"),
        name="rglru",
    )(gu, gu, conv_buf, h0, conv_w, vec(conv_b), w_a, vec(b_a), w_x, vec(b_x), vec(lam))


TM = 640
TN = 512
TF = 512
FOX_TQ = 512
FOX_BLK = 256
FOX_PAGES = 16
LRU_T_CHUNK = 256


def kernel(x_prompt, x_sample, cache_fox_k, cache_fox_v, cache_fox_logf, cache_swa_k, cache_swa_v, state_lru_h, state_lru_conv, page_table, norm_mix_pre, norm_mix_post, norm_mlp_pre, norm_mlp_post, mlp_w_up, mlp_w_down, fox_w_qkv, fox_w_f, fox_b_f, fox_w_o, swa_w_qkv, swa_b_qkv, swa_sinks, swa_w_o, swa_b_o, lru_w_gate, lru_b_gate, lru_w_in, lru_b_in, lru_conv_w, lru_conv_b, lru_w_a, lru_b_a, lru_w_x, lru_b_x, lru_lambda, lru_w_out, lru_b_out):
    bp, seq, d = x_prompt.shape
    bs, n_t, _ = x_sample.shape
    mp, ms = bp * seq, bs * n_t
    depth = norm_mix_pre.shape[0]
    fox_heads = fox_w_f.shape[2]
    fox_kv, fox_hd = cache_fox_k.shape[3], cache_fox_k.shape[4]
    swa_heads = swa_sinks.shape[1]
    swa_kv, swa_hd = cache_swa_k.shape[3], cache_swa_k.shape[4]
    d_rnn = lru_w_gate.shape[2]
    conv_w = lru_conv_w.shape[1]
    past = page_table.shape[1] * cache_fox_k.shape[2]
    fox_qw, fox_kw = fox_heads * fox_hd, fox_kv * fox_hd
    swa_qw, swa_kw = swa_heads * swa_hd, swa_kv * swa_hd

    x = jnp.concatenate([x_prompt.reshape(mp, d), x_sample.reshape(ms, d)], axis=0)
    n_layers_fox, pool, page = cache_fox_k.shape[:3]
    ck = cache_fox_k.reshape(n_layers_fox, pool, page * fox_kv, fox_hd)
    cv = cache_fox_v.reshape(n_layers_fox, pool, page * fox_kv, fox_hd)
    clf_t = jnp.swapaxes(cache_fox_logf, 2, 3)

    fkp, fvp, flp, fks, fvs, fls = [], [], [], [], [], []
    skp, svp, sks, svs = [], [], [], []
    lhp, lcp, lhs, lcs = [], [], [], []
    for i in range(depth):
        kind, j = i % 3, i // 3
        if kind == 0:
            qkv, logf = norm_matmul(x, norm_mix_pre[i], fox_w_qkv[j].astype(BF16),
                                    gate=(fox_w_f[j].astype(BF16), fox_b_f[j]), tm=TM, tn=TN)
            kp = qkv[:mp, fox_qw:fox_qw + fox_kw]
            vp = qkv[:mp, fox_qw + fox_kw:]
            fkp.append(kp.reshape(bp, seq, fox_kv, fox_hd))
            fvp.append(vp.reshape(bp, seq, fox_kv, fox_hd))
            flp.append(logf[:mp].reshape(bp, seq, fox_heads))
            qkv_s = qkv[mp:].reshape(bs, n_t, -1)
            lf_s = logf[mp:].reshape(bs, n_t, fox_heads)
            fks.append(qkv_s[:, :, fox_qw:fox_qw + fox_kw].reshape(bs, n_t, fox_kv, fox_hd))
            fvs.append(qkv_s[:, :, fox_qw + fox_kw:].reshape(bs, n_t, fox_kv, fox_hd))
            fls.append(lf_s)
            qx, kx = fox_cumsum(logf[:mp].reshape(bp, seq, fox_heads), n_kv=fox_kv, chunk=FOX_BLK)
            att_p = fox_flash(qkv, qx, kx, batch=bp, seq=seq, n_heads=fox_heads, n_kv=fox_kv, hd=fox_hd,
                              tq=FOX_TQ, blk=FOX_BLK)
            fox_g = fox_heads // fox_kv
            q_rows = qkv_s[:, :, :fox_qw].reshape(bs, n_t, fox_kv, fox_g, fox_hd).transpose(0, 2, 1, 3, 4)
            q_rows = q_rows.reshape(bs, fox_kv, n_t * fox_g, fox_hd)
            lf_new_t = jnp.pad(jnp.swapaxes(lf_s, 1, 2), ((0, 0), (0, 0), (0, page - n_t)))
            att_s = fox_decode(q_rows, qkv_s, lf_new_t, ck, cv, clf_t, page_table, j,
                               n_heads=fox_heads, n_kv=fox_kv, hd=fox_hd, pages=FOX_PAGES)
            att_s = att_s.reshape(bs, fox_kv, n_t, fox_g, fox_hd).transpose(0, 2, 1, 3, 4).reshape(ms, fox_qw)
            att = jnp.concatenate([att_p, att_s.astype(BF16)], axis=0)
            x = matmul_norm_residual(att, fox_w_o[j].astype(BF16), None, norm_mix_post[i], x, tm=TM)
        elif kind == 1:
            qkv = norm_matmul(x, norm_mix_pre[i], swa_w_qkv[j].astype(BF16), swa_b_qkv[j], tm=TM, tn=TN)
            att_p, k_last = swa_prompt(qkv, swa_sinks[j], batch=bp, seq=seq, n_heads=swa_heads, n_kv=swa_kv,
                                       hd=swa_hd, blk=WINDOW)
            keep = min(WINDOW, seq)
            skp.append(k_last.reshape(bp, keep, swa_kv, swa_hd))
            svp.append(qkv[:mp, swa_qw + swa_kw:].reshape(bp, seq, swa_kv, swa_hd)[:, seq - keep:])
            qkv_s = qkv[mp:].reshape(bs, n_t, -1)
            q_rows = qkv_s[:, :, :swa_qw].reshape(bs, n_t * swa_heads, swa_hd)
            nbuf = cache_swa_k.shape[2]
            att_s, nk, nv = swa_sample(q_rows, qkv_s, cache_swa_k[j].reshape(bs, nbuf, swa_kw),
                                       cache_swa_v[j].reshape(bs, nbuf, swa_kw), swa_sinks[j], past,
                                       n_heads=swa_heads, n_kv=swa_kv, hd=swa_hd)
            sks.append(nk.reshape(bs, nbuf, swa_kv, swa_hd))
            svs.append(nv.reshape(bs, nbuf, swa_kv, swa_hd))
            att = jnp.concatenate([att_p, att_s.reshape(ms, swa_qw).astype(BF16)], axis=0)
            x = matmul_norm_residual(att, swa_w_o[j].astype(BF16), swa_b_o[j], norm_mix_post[i], x, tm=TM)
        else:
            w_gu = jnp.concatenate([lru_w_gate[j], lru_w_in[j]], axis=1).astype(BF16)
            b_gu = jnp.concatenate([lru_b_gate[j], lru_b_in[j]])
            gu = norm_matmul(x, norm_mix_pre[i], w_gu, b_gu, n_gelu_cols=d_rnn, tm=TM, tn=TN)
            lw = (lru_conv_w[j], lru_conv_b[j], lru_w_a[j].astype(BF16), lru_b_a[j], lru_w_x[j].astype(BF16),
                  lru_b_x[j], lru_lambda[j])
            y_p, h_p, c_p = rglru(gu, jnp.zeros((bp, conv_w - 1, d_rnn), F32), jnp.zeros((bp, 1, d_rnn), F32),
                                  *lw, n_groups=bp, n_t=seq, n_seq=1, t_chunk=LRU_T_CHUNK, d_rnn=d_rnn)
            lhp.append(h_p.reshape(bp, d_rnn))
            lcp.append(c_p)
            gu_s = gu[mp:].reshape(bs, n_t, -1).swapaxes(0, 1).reshape(ms, -1)
            cb_s = state_lru_conv[j].swapaxes(0, 1).reshape(1, (conv_w - 1) * bs, d_rnn)
            y_s, h_s, c_s = rglru(gu_s, cb_s, state_lru_h[j].reshape(1, bs, d_rnn), *lw,
                                  n_groups=1, n_t=n_t, n_seq=bs, t_chunk=n_t, d_rnn=d_rnn)
            lhs.append(h_s.reshape(bs, d_rnn))
            lcs.append(c_s.reshape(conv_w - 1, bs, d_rnn).swapaxes(0, 1))
            y_s = y_s.reshape(n_t, bs, d_rnn).swapaxes(0, 1).reshape(ms, d_rnn)
            y = jnp.concatenate([y_p, y_s], axis=0)
            x = matmul_norm_residual(y, lru_w_out[j].astype(BF16), lru_b_out[j], norm_mix_post[i], x, tm=TM)
        x = mlp_sublayer(x, norm_mlp_pre[i], mlp_w_up[i].astype(BF16), mlp_w_down[i].astype(BF16),
                         norm_mlp_post[i], tm=TM, tf=TF)
    return (x[:mp].reshape(bp, seq, d), x[mp:].reshape(bs, n_t, d),
            jnp.stack(fkp), jnp.stack(fvp), jnp.stack(flp),
            jnp.stack(fks), jnp.stack(fvs), jnp.stack(fls),
            jnp.stack(skp), jnp.stack(svp), jnp.stack(sks), jnp.stack(svs),
            jnp.stack(lhp), jnp.stack(lcp), jnp.stack(lhs), jnp.stack(lcs))
```

```python
import functools

import jax
import jax.numpy as jnp
from jax import lax
from jax.experimental import pallas as pl
from jax.experimental.pallas import tpu as pltpu

F32 = jnp.float32
BF16 = jnp.bfloat16

RMS_EPS = 1e-6
WINDOW = 128
ROPE_THETA = 500000.0
LRU_C = 8.0
LOG2E = 1.4426950408889634

V7X_VMEM_LIMIT_BYTES = 56 * 1024 * 1024
LANES = 128

NT_DIMS = (((1,), (1,)), ((), ()))


def _params(*sem):
    return pltpu.CompilerParams(dimension_semantics=sem, vmem_limit_bytes=V7X_VMEM_LIMIT_BYTES)


def _rms(x, g):
    ms = jnp.mean(x * x, axis=-1, keepdims=True)
    return x * lax.rsqrt(ms + RMS_EPS) * g


def _expm1(x):
    u = jnp.exp(x)
    um1 = u - 1.0
    safe = jnp.where(u == 1.0, 1.0, jnp.log(u))
    return jnp.where(u == 1.0, x, jnp.where(um1 == -1.0, -1.0, um1 * x / safe))


def _split3(x):
    hi = x.astype(BF16)
    r1 = x - hi.astype(F32)
    mid = r1.astype(BF16)
    lo = (r1 - mid.astype(F32)).astype(BF16)
    return hi, mid, lo


def _norm_mm_kernel(*refs, n_gelu_tiles, has_bias, has_gate):
    it = iter(refs)
    x_ref, g_ref, w_ref = next(it), next(it), next(it)
    b_ref = next(it) if has_bias else None
    wf_ref, bf_ref = (next(it), next(it)) if has_gate else (None, None)
    o_ref = next(it)
    lf_ref = next(it) if has_gate else None
    xn_ref = next(it)
    j = pl.program_id(1)

    @pl.when(j == 0)
    def _():
        xn = _rms(x_ref[...], g_ref[...]).astype(BF16)
        xn_ref[...] = xn
        if has_gate:
            z = jnp.dot(xn, wf_ref[...], preferred_element_type=F32) + bf_ref[...]
            lf_ref[...] = jax.nn.log_sigmoid(z)

    acc = jnp.dot(xn_ref[...], w_ref[...], preferred_element_type=F32)
    if has_bias:
        acc = acc + b_ref[...]
    if n_gelu_tiles == 0:
        o_ref[...] = acc.astype(o_ref.dtype)
    else:
        @pl.when(j < n_gelu_tiles)
        def _():
            o_ref[...] = jax.nn.gelu(acc).astype(o_ref.dtype)

        @pl.when(j >= n_gelu_tiles)
        def _():
            o_ref[...] = acc.astype(o_ref.dtype)


def norm_matmul(x, g, w, b=None, *, gate=None, n_gelu_cols=0, tm, tn):
    m, d = x.shape
    n = w.shape[1]
    assert m % tm == 0 and n % tn == 0 and n_gelu_cols % tn == 0
    has_bias, has_gate = b is not None, gate is not None
    args = [x, g.reshape(1, d), w]
    in_specs = [
        pl.BlockSpec((tm, d), lambda i, j: (i, 0)),
        pl.BlockSpec((1, d), lambda i, j: (0, 0)),
        pl.BlockSpec((d, tn), lambda i, j: (0, j)),
    ]
    if has_bias:
        args.append(b.reshape(1, n))
        in_specs.append(pl.BlockSpec((1, tn), lambda i, j: (0, j)))
    out_shape = [jax.ShapeDtypeStruct((m, n), F32)]
    out_specs = [pl.BlockSpec((tm, tn), lambda i, j: (i, j))]
    if has_gate:
        w_f, b_f = gate
        h = w_f.shape[1]
        args += [w_f, b_f.reshape(1, h)]
        in_specs += [pl.BlockSpec((d, h), lambda i, j: (0, 0)), pl.BlockSpec((1, h), lambda i, j: (0, 0))]
        out_shape.append(jax.ShapeDtypeStruct((m, h), F32))
        out_specs.append(pl.BlockSpec((tm, h), lambda i, j: (i, 0)))
    outs = pl.pallas_call(
        functools.partial(_norm_mm_kernel, n_gelu_tiles=n_gelu_cols // tn, has_bias=has_bias, has_gate=has_gate),
        grid=(m // tm, n // tn),
        in_specs=in_specs,
        out_specs=out_specs,
        out_shape=out_shape,
        scratch_shapes=[pltpu.VMEM((tm, d), BF16)],
        compiler_params=_params("parallel", "arbitrary"),
        name="norm_matmul",
    )(*args)
    return outs if has_gate else outs[0]


def _mm_norm_res_kernel(*refs, has_bias):
    it = iter(refs)
    a_ref, w_ref = next(it), next(it)
    b_ref = next(it) if has_bias else None
    g_ref, x_ref, o_ref = next(it), next(it), next(it)
    mix = jnp.dot(a_ref[...], w_ref[...], preferred_element_type=F32)
    if has_bias:
        mix = mix + b_ref[...]
    o_ref[...] = x_ref[...] + _rms(mix, g_ref[...])


def matmul_norm_residual(a, w, b, g, x, *, tm):
    m, k = a.shape
    d = w.shape[1]
    assert m % tm == 0
    has_bias = b is not None
    args = [a, w]
    in_specs = [pl.BlockSpec((tm, k), lambda i: (i, 0)), pl.BlockSpec((k, d), lambda i: (0, 0))]
    if has_bias:
        args.append(b.reshape(1, d))
        in_specs.append(pl.BlockSpec((1, d), lambda i: (0, 0)))
    args += [g.reshape(1, d), x]
    in_specs += [pl.BlockSpec((1, d), lambda i: (0, 0)), pl.BlockSpec((tm, d), lambda i: (i, 0))]
    return pl.pallas_call(
        functools.partial(_mm_norm_res_kernel, has_bias=has_bias),
        grid=(m // tm,),
        in_specs=in_specs,
        out_specs=pl.BlockSpec((tm, d), lambda i: (i, 0)),
        out_shape=jax.ShapeDtypeStruct((m, d), F32),
        compiler_params=_params("parallel"),
        name="matmul_norm_residual",
    )(*args)


def _mlp_kernel(x_ref, gpre_ref, wup_ref, wdn_ref, gpost_ref, o_ref, xn_ref, acc_ref):
    f = pl.program_id(1)

    @pl.when(f == 0)
    def _():
        xn_ref[...] = _rms(x_ref[...], gpre_ref[...]).astype(BF16)
        acc_ref[...] = jnp.zeros_like(acc_ref)

    h = jnp.dot(xn_ref[...], wup_ref[...], preferred_element_type=F32)
    h = jnp.square(jnp.maximum(h, 0.0)).astype(BF16)
    acc_ref[...] += jnp.dot(h, wdn_ref[...], preferred_element_type=F32)

    @pl.when(f == pl.num_programs(1) - 1)
    def _():
        o_ref[...] = x_ref[...] + _rms(acc_ref[...], gpost_ref[...])


def mlp_sublayer(x, g_pre, w_up, w_down, g_post, *, tm, tf):
    m, d = x.shape
    dff = w_up.shape[1]
    assert m % tm == 0 and dff % tf == 0
    return pl.pallas_call(
        _mlp_kernel,
        grid=(m // tm, dff // tf),
        in_specs=[
            pl.BlockSpec((tm, d), lambda i, f: (i, 0)),
            pl.BlockSpec((1, d), lambda i, f: (0, 0)),
            pl.BlockSpec((d, tf), lambda i, f: (0, f)),
            pl.BlockSpec((tf, d), lambda i, f: (f, 0)),
            pl.BlockSpec((1, d), lambda i, f: (0, 0)),
        ],
        out_specs=pl.BlockSpec((tm, d), lambda i, f: (i, 0)),
        out_shape=jax.ShapeDtypeStruct((m, d), F32),
        scratch_shapes=[pltpu.VMEM((tm, d), BF16), pltpu.VMEM((tm, d), F32)],
        compiler_params=_params("parallel", "arbitrary"),
        name="mlp_sublayer",
    )(x, g_pre.reshape(1, d), w_up, w_down, g_post.reshape(1, d))


N_PARTS = 3


def _fox_cumsum_kernel(lf_ref, qx_ref, kx_ref, *, chunk, n_kv, n_g):
    s, h = lf_ref.shape[1], lf_ref.shape[2]
    row = lax.broadcasted_iota(jnp.int32, (chunk, chunk), 0)
    col = lax.broadcasted_iota(jnp.int32, (chunk, chunk), 1)
    lower = (row >= col).astype(BF16)

    def placement(n_tiles, lane_of, head_of):
        r = lax.broadcasted_iota(jnp.int32, (h, n_tiles * LANES), 0)
        c = lax.broadcasted_iota(jnp.int32, (h, n_tiles * LANES), 1)
        tile, lane = c // LANES, c % LANES
        mats = []
        for p in range(N_PARTS):
            hit = jnp.zeros((h, n_tiles * LANES), jnp.bool_)
            for g in range(n_g):
                hit = hit | ((r == head_of(tile, g)) & (lane == lane_of(g, p)))
            mats.append(hit)
        return mats, tile, lane

    q_hit, q_tile, q_lane = placement(h, lambda g, p: p, lambda tile, g: jnp.where(tile % n_g == g, tile, -1))
    q_mats = [m.astype(BF16) for m in q_hit]
    q_gl = N_PARTS * (1 + q_tile[0:1] % n_g)
    q_ones = ((q_lane[0:1] >= q_gl) & (q_lane[0:1] < q_gl + N_PARTS)).astype(F32)
    k_hit, _, k_lane = placement(n_kv, lambda g, p: N_PARTS * (1 + g) + p, lambda tile, g: tile * n_g + g)
    k_mats = [(-m.astype(F32)).astype(BF16) for m in k_hit]
    k_ones = (k_lane[0:1] < N_PARTS).astype(F32)

    carry = jnp.zeros((1, h), F32)
    for c in range(s // chunk):
        rows = slice(c * chunk, (c + 1) * chunk)
        hi, mid, lo = _split3(lf_ref[0, rows, :])
        cs = (jnp.dot(lower, hi, preferred_element_type=F32)
              + jnp.dot(lower, mid, preferred_element_type=F32)
              + jnp.dot(lower, lo, preferred_element_type=F32)) + carry
        carry = cs[chunk - 1:chunk, :]
        parts = _split3(cs * LOG2E)
        qx = q_ones + sum(jnp.dot(parts[p], q_mats[p], preferred_element_type=F32) for p in range(N_PARTS))
        kx = k_ones + sum(jnp.dot(parts[p], k_mats[p], preferred_element_type=F32) for p in range(N_PARTS))
        for hh in range(h):
            qx_ref[0, hh, rows, :] = qx[:, hh * LANES:(hh + 1) * LANES].astype(BF16)
        for kv in range(n_kv):
            kx_ref[0, kv, rows, :] = kx[:, kv * LANES:(kv + 1) * LANES].astype(BF16)


def fox_cumsum(logf, *, n_kv, chunk):
    b, s, h = logf.shape
    n_g = h // n_kv
    assert N_PARTS * (1 + n_g) <= LANES
    return pl.pallas_call(
        functools.partial(_fox_cumsum_kernel, chunk=chunk, n_kv=n_kv, n_g=n_g),
        grid=(b,),
        in_specs=[pl.BlockSpec((1, s, h), lambda i: (i, 0, 0))],
        out_specs=[
            pl.BlockSpec((1, h, s, LANES), lambda i: (i, 0, 0, 0)),
            pl.BlockSpec((1, n_kv, s, LANES), lambda i: (i, 0, 0, 0)),
        ],
        out_shape=[
            jax.ShapeDtypeStruct((b, h, s, LANES), BF16),
            jax.ShapeDtypeStruct((b, n_kv, s, LANES), BF16),
        ],
        compiler_params=_params("parallel"),
        name="fox_cumsum",
    )(logf)


def _lane_tiles(x):
    return [x[:, j * LANES:(j + 1) * LANES] for j in range(x.shape[1] // LANES)]


def _fox_flash_kernel(q_ref, k_ref, v_ref, qx_ref, kx_ref, o_ref, ka_ref, vb_ref, s_ref, m_ref, l_ref, acc_ref,
                      *, tq, blk, n_g, hd):
    qi = pl.program_id(2)
    per_q = tq // blk

    @pl.when(qi == 0)
    def _():
        ka_ref[:, :hd] = k_ref[...].astype(BF16)
        ka_ref[:, hd:] = kx_ref[0, 0]
        vb_ref[...] = v_ref[...].astype(BF16)

    q_scale = hd ** -0.5 * LOG2E
    qa = [jnp.concatenate([(q_ref[:, g * hd:(g + 1) * hd] * q_scale).astype(BF16), qx_ref[0, g]], axis=1)
          for g in range(n_g)]
    m_ref[...] = jnp.full_like(m_ref, -jnp.inf)

    def logits_pass(kc, diag):
        ka = ka_ref[pl.ds(pl.multiple_of(kc * blk, blk), blk), :]
        if diag is not None:
            row = lax.broadcasted_iota(jnp.int32, (tq, blk), 0)
            col = lax.broadcasted_iota(jnp.int32, (tq, blk), 1) + diag * blk
            causal = row >= col
        for g in range(n_g):
            s = lax.dot_general(qa[g], ka, NT_DIMS, preferred_element_type=F32)
            if diag is not None:
                s = jnp.where(causal, s, -jnp.inf)
            s_ref[g, kc] = s
            m = m_ref[g]
            for t in _lane_tiles(s):
                m = jnp.maximum(m, t)
            m_ref[g] = m

    def body1(kc, carry):
        logits_pass(kc, None)
        return carry

    lax.fori_loop(0, per_q * qi, body1, 0)
    for e in range(per_q):
        logits_pass(per_q * qi + e, e)
    for g in range(n_g):
        m_ref[g] = jnp.broadcast_to(jnp.max(m_ref[g], axis=1, keepdims=True), (tq, LANES))
    l_ref[...] = jnp.zeros_like(l_ref)
    acc_ref[...] = jnp.zeros_like(acc_ref)

    def body2(kc, carry):
        vb = vb_ref[pl.ds(pl.multiple_of(kc * blk, blk), blk), :]
        for g in range(n_g):
            m = m_ref[g]
            p = [jnp.exp2(t - m) for t in _lane_tiles(s_ref[g, kc])]
            l_ref[g] += sum(p)
            acc_ref[g] += jnp.dot(jnp.concatenate(p, axis=1).astype(BF16), vb, preferred_element_type=F32)
        return carry

    lax.fori_loop(0, per_q * (qi + 1), body2, 0)
    for g in range(n_g):
        inv = 1.0 / jnp.sum(l_ref[g], axis=1, keepdims=True)
        o_ref[:, g * hd:(g + 1) * hd] = (acc_ref[g] * inv).astype(o_ref.dtype)


def fox_flash(qkv, qx, kx, *, batch, seq, n_heads, n_kv, hd, tq, blk):
    assert hd == LANES and tq % blk == 0 and seq % tq == 0
    n_g = n_heads // n_kv
    nq = seq // tq
    gw = n_g * hd
    return pl.pallas_call(
        functools.partial(_fox_flash_kernel, tq=tq, blk=blk, n_g=n_g, hd=hd),
        grid=(batch, n_kv, nq),
        in_specs=[
            pl.BlockSpec((tq, gw), lambda b, kv, qi: (b * nq + qi, kv)),
            pl.BlockSpec((seq, hd), lambda b, kv, qi: (b, n_heads + kv)),
            pl.BlockSpec((seq, hd), lambda b, kv, qi: (b, n_heads + n_kv + kv)),
            pl.BlockSpec((1, n_g, tq, LANES), lambda b, kv, qi: (b, kv, qi, 0)),
            pl.BlockSpec((1, 1, seq, LANES), lambda b, kv, qi: (b, kv, 0, 0)),
        ],
        out_specs=pl.BlockSpec((tq, gw), lambda b, kv, qi: (b * nq + qi, kv)),
        out_shape=jax.ShapeDtypeStruct((qkv.shape[0], n_heads * hd), BF16),
        scratch_shapes=[
            pltpu.VMEM((seq, hd + LANES), BF16),
            pltpu.VMEM((seq, hd), BF16),
            pltpu.VMEM((n_g, seq // blk, tq, blk), F32),
            pltpu.VMEM((n_g, tq, LANES), F32),
            pltpu.VMEM((n_g, tq, LANES), F32),
            pltpu.VMEM((n_g, tq, hd), F32),
        ],
        compiler_params=_params("parallel", "parallel", "arbitrary"),
        name="fox_flash",
    )(qkv, qkv, qkv, qx, kx)


def _fox_decode_kernel(pt_ref, q_ref, knew_ref, vnew_ref, lfnew_ref, *rest, pages, n_kv, n_heads, hd, n_t):
    del pt_ref
    k_refs, v_refs, lf_refs = rest[:pages], rest[pages:2 * pages], rest[2 * pages:3 * pages]
    o_ref, cq_ref, m_ref, l_ref, acc_ref, carry_ref, kpad_ref, vpad_ref = rest[3 * pages:]
    c = pl.program_id(1)
    scale = hd ** -0.5
    page = lf_refs[0].shape[3]
    n_g = n_heads // n_kv
    grp = n_t * n_g
    jj = lax.broadcasted_iota(jnp.int32, (page, page), 0)
    ss = lax.broadcasted_iota(jnp.int32, (page, page), 1)

    def lane_sums(xs, tri):
        stacked = jnp.concatenate([part for x in xs for part in _split3(x)], axis=0)
        out = jnp.dot(stacked, tri, preferred_element_type=F32)
        n = N_PARTS * n_heads
        return [out[i * n:i * n + n_heads] + out[i * n + n_heads:i * n + 2 * n_heads]
                + out[i * n + 2 * n_heads:(i + 1) * n] for i in range(len(xs))]

    def group_rows(x, kv):
        return jnp.concatenate([x[kv * n_g:(kv + 1) * n_g]] * n_t, axis=0)

    q = [q_ref[0, kv].astype(BF16) for kv in range(n_kv)]

    @pl.when(c == 0)
    def _():
        c_new = lane_sums([lfnew_ref[0]], (jj <= ss).astype(BF16))[0]
        kpad_ref[...] = jnp.zeros_like(kpad_ref)
        vpad_ref[...] = jnp.zeros_like(vpad_ref)
        tok = lax.broadcasted_iota(jnp.int32, (grp, page), 0) // n_g
        key = lax.broadcasted_iota(jnp.int32, (grp, page), 1)
        for kv in range(n_kv):
            cq = jnp.concatenate([c_new[kv * n_g:(kv + 1) * n_g, t:t + 1] for t in range(n_t)], axis=0)
            cq_ref[kv] = cq
            kpad_ref[kv, 0:n_t, :] = knew_ref[0, :, kv * hd:(kv + 1) * hd]
            vpad_ref[kv, 0:n_t, :] = vnew_ref[0, :, kv * hd:(kv + 1) * hd]
            s = lax.dot_general(q[kv], kpad_ref[kv].astype(BF16), NT_DIMS, preferred_element_type=F32) * scale
            s = s + (cq - group_rows(c_new, kv))
            s = jnp.where(key <= tok, s, -jnp.inf)
            m = jnp.max(s, axis=1, keepdims=True)
            p = jnp.exp(s - m)
            m_ref[kv] = m
            l_ref[kv] = jnp.sum(p, axis=1, keepdims=True)
            acc_ref[kv] = jnp.dot(p.astype(BF16), vpad_ref[kv].astype(BF16), preferred_element_type=F32)
        carry_ref[...] = jnp.zeros_like(carry_ref)

    lfs = [lf_refs[i][0, 0] for i in range(pages)]
    local = lane_sums(lfs, (jj > ss).astype(BF16))
    carry = carry_ref[...]
    rs = []
    for i in range(pages):
        rs.append(local[i] + carry)
        carry = carry + (local[i][:, 0:1] + lfs[i][:, 0:1])
    carry_ref[...] = carry
    raw = [[lax.dot_general(q[kv], k_refs[i][0, 0, pl.ds(kv, page, stride=n_kv), :].astype(BF16), NT_DIMS,
                            preferred_element_type=F32) for i in range(pages)] for kv in range(n_kv)]
    probs, alphas = [], []
    for kv in range(n_kv):
        cq = cq_ref[kv]
        logits = [raw[kv][i] * scale + (cq + group_rows(rs[i], kv)) for i in range(pages)]
        m_old = m_ref[kv]
        tile_max = logits[0]
        for s in logits[1:]:
            tile_max = jnp.maximum(tile_max, s)
        m_new = jnp.maximum(m_old, jnp.max(tile_max, axis=1, keepdims=True))
        alpha = jnp.exp(m_old - m_new)
        ps = [jnp.exp(s - m_new) for s in logits]
        l_ref[kv] = alpha * l_ref[kv] + jnp.sum(sum(ps), axis=1, keepdims=True)
        m_ref[kv] = m_new
        probs.append([p.astype(BF16) for p in ps])
        alphas.append(alpha)
    for kv in range(n_kv):
        pv = sum(jnp.dot(probs[kv][i], v_refs[i][0, 0, pl.ds(kv, page, stride=n_kv), :].astype(BF16),
                         preferred_element_type=F32) for i in range(pages))
        acc_ref[kv] = alphas[kv] * acc_ref[kv] + pv

    @pl.when(c == pl.num_programs(1) - 1)
    def _():
        for kv in range(n_kv):
            o_ref[0, kv] = acc_ref[kv] / l_ref[kv]


def fox_decode(q_rows, qkv_s, lf_new_t, cache_k, cache_v, cache_lf_t, page_table, layer, *, n_heads, n_kv, hd, pages):
    bsz, n_t = qkv_s.shape[0], qkv_s.shape[1]
    n_pages = page_table.shape[1]
    page = cache_lf_t.shape[3]
    width = n_kv * hd
    grp = n_t * (n_heads // n_kv)
    assert n_pages % pages == 0 and page == LANES and cache_k.shape[2] == page * n_kv
    n_chunks = n_pages // pages

    def page_map(i):
        return lambda b, c, pt: (layer, pt[b, n_pages - 1 - (c * pages + i)], 0, 0)

    kv_col = n_heads * hd // width
    in_specs = [
        pl.BlockSpec((1, n_kv, grp, hd), lambda b, c, pt: (b, 0, 0, 0)),
        pl.BlockSpec((1, n_t, width), lambda b, c, pt: (b, 0, kv_col)),
        pl.BlockSpec((1, n_t, width), lambda b, c, pt: (b, 0, kv_col + 1)),
        pl.BlockSpec((1, n_heads, page), lambda b, c, pt: (b, 0, 0)),
    ]
    in_specs += [pl.BlockSpec((1, 1, page * n_kv, hd), page_map(i)) for i in range(pages)]
    in_specs += [pl.BlockSpec((1, 1, page * n_kv, hd), page_map(i)) for i in range(pages)]
    in_specs += [pl.BlockSpec((1, 1, n_heads, page), page_map(i)) for i in range(pages)]
    grid_spec = pltpu.PrefetchScalarGridSpec(
        num_scalar_prefetch=1,
        grid=(bsz, n_chunks),
        in_specs=in_specs,
        out_specs=pl.BlockSpec((1, n_kv, grp, hd), lambda b, c, pt: (b, 0, 0, 0)),
        scratch_shapes=[
            pltpu.VMEM((n_kv, grp, 1), F32),
            pltpu.VMEM((n_kv, grp, 1), F32),
            pltpu.VMEM((n_kv, grp, 1), F32),
            pltpu.VMEM((n_kv, grp, hd), F32),
            pltpu.VMEM((n_heads, 1), F32),
            pltpu.VMEM((n_kv, page, hd), F32),
            pltpu.VMEM((n_kv, page, hd), F32),
        ],
    )
    return pl.pallas_call(
        functools.partial(_fox_decode_kernel, pages=pages, n_kv=n_kv, n_heads=n_heads, hd=hd, n_t=n_t),
        grid_spec=grid_spec,
        out_shape=jax.ShapeDtypeStruct((bsz, n_kv, grp, hd), F32),
        compiler_params=_params("parallel", "arbitrary"),
        name="fox_decode",
    )(page_table, q_rows, qkv_s, qkv_s, lf_new_t, *([cache_k] * pages), *([cache_v] * pages), *([cache_lf_t] * pages))


def _rope_tables(pos, hd, reps):
    rot = hd // 4
    half = rot // 2
    inv_freq = ROPE_THETA ** (-jnp.arange(half, dtype=F32) / half)
    ang = pos.astype(F32)[:, None] * inv_freq[None, :]
    cos, sin = jnp.cos(ang), jnp.sin(ang)
    n = pos.shape[0]
    ones = jnp.ones((n, hd - rot), F32)
    zeros_h = jnp.zeros((n, half), F32)
    zeros_r = jnp.zeros((n, hd - rot), F32)
    c = jnp.concatenate([cos, cos, ones], axis=1)
    s_up = jnp.concatenate([-sin, zeros_h, zeros_r], axis=1)
    s_dn = jnp.concatenate([zeros_h, sin, zeros_r], axis=1)
    return tuple(jnp.tile(t, (1, reps)) for t in (c, s_up, s_dn))


def _rope(x, c, s_up, s_dn, half):
    w = x.shape[1]
    pieces = []
    for j in range(w // LANES):
        xj = x[:, j * LANES:(j + 1) * LANES]
        up = pltpu.roll(xj, LANES - half, 1)
        dn = pltpu.roll(xj, half, 1)
        pieces.append(xj * c + up * s_up + dn * s_dn)
    return pieces[0] if len(pieces) == 1 else jnp.concatenate(pieces, axis=1)


def _swa_prompt_kernel(q_ref, kc_ref, kp_ref, vc_ref, vp_ref, cc_ref, su_ref, sd_ref, pc_ref, pu_ref, pd_ref,
                       sink_ref, o_ref, klast_ref, *, n_heads, n_kv, hd, blk):
    qb = pl.program_id(1)
    half = hd // 8
    n_g = n_heads // n_kv
    per_tile = LANES // hd
    assert per_tile == 2 and n_g % per_tile == 0
    nk = 2 * blk
    q = _rope(q_ref[...], cc_ref[...], su_ref[...], sd_ref[...], half) * (hd ** -0.5 * LOG2E)
    kc = _rope(kc_ref[...], cc_ref[...], su_ref[...], sd_ref[...], half)
    kp = _rope(kp_ref[...], pc_ref[...], pu_ref[...], pd_ref[...], half)
    k = jnp.concatenate([kp, kc], axis=0)
    v = jnp.concatenate([vp_ref[...], vc_ref[...]], axis=0)
    lane = lax.broadcasted_iota(jnp.int32, (nk, LANES), 1)
    d_row = lax.broadcasted_iota(jnp.int32, (LANES, nk), 0)
    out_row = lax.broadcasted_iota(jnp.int32, (LANES, blk), 0)

    def spread_keys(kv):
        tile = k[:, (kv // per_tile) * LANES:(kv // per_tile + 1) * LANES]
        swapped = pltpu.roll(tile, hd, 1)
        low, high = (tile, swapped) if kv % per_tile == 0 else (swapped, tile)
        return jnp.concatenate([jnp.where(lane < hd, low, 0.0), jnp.where(lane >= hd, high, 0.0)],
                               axis=0).astype(BF16)

    def spread_values_t(kv):
        tile_t = v[:, (kv // per_tile) * LANES:(kv // per_tile + 1) * LANES].T
        swapped_t = jnp.concatenate([tile_t[hd:], tile_t[:hd]], axis=0)
        low, high = (tile_t, swapped_t) if kv % per_tile == 0 else (swapped_t, tile_t)
        return jnp.concatenate([jnp.where(d_row < hd, low, 0.0), jnp.where(d_row >= hd, high, 0.0)],
                               axis=1).astype(BF16)

    j = lax.broadcasted_iota(jnp.int32, (nk, blk), 0)
    t = lax.broadcasted_iota(jnp.int32, (nk, blk), 1)
    first_key = jnp.where(qb > 0, 0, blk)
    band = (j > t) & (j <= t + blk) & (j >= first_key)
    pairs = n_g // per_tile
    n_tiles = n_kv * pairs
    k2 = [spread_keys(kv) for kv in range(n_kv)]
    raw = [lax.dot_general(k2[i // pairs], q[:, i * LANES:(i + 1) * LANES].astype(BF16), NT_DIMS,
                           preferred_element_type=F32) for i in range(n_tiles)]
    weights, scales = [], []
    for i in range(n_tiles):
        es, invs = [], []
        for e in range(per_tile):
            h = per_tile * i + e
            se = jnp.where(band, raw[i][e * nk:(e + 1) * nk], -jnp.inf)
            sink = sink_ref[0:1, h:h + 1] * LOG2E
            m = jnp.maximum(jnp.max(se, axis=0, keepdims=True), sink)
            ee = jnp.exp2(se - m)
            es.append(ee)
            invs.append(1.0 / (jnp.sum(ee, axis=0, keepdims=True) + jnp.exp2(sink - m)))
        weights.append(jnp.concatenate(es, axis=0).astype(BF16))
        scales.append(jnp.where(out_row < hd, invs[0], invs[1]))
    v2t = [spread_values_t(kv) for kv in range(n_kv)]
    for i in range(n_tiles):
        out_t = jnp.dot(v2t[i // pairs], weights[i], preferred_element_type=F32) * scales[i]
        o_ref[:, i * LANES:(i + 1) * LANES] = out_t.T.astype(o_ref.dtype)

    @pl.when(qb == pl.num_programs(1) - 1)
    def _():
        klast_ref[0] = kc


def swa_prompt(qkv, sinks, *, batch, seq, n_heads, n_kv, hd, blk):
    assert blk == WINDOW and (n_kv * hd) % LANES == 0
    nb = seq // blk
    qw, kw = n_heads * hd, n_kv * hd
    kcol = qw // kw
    pos = jnp.arange(seq)
    reps = LANES // hd
    cur = _rope_tables(pos, hd, reps)
    prev = _rope_tables(pos - blk, hd, reps)

    def cur_row(b, qb):
        return b * nb + qb

    def prev_row(b, qb):
        return b * nb + jnp.maximum(qb - 1, 0)

    tab_cur = pl.BlockSpec((blk, LANES), lambda b, qb: (qb, 0))
    tab_prev = pl.BlockSpec((blk, LANES), lambda b, qb: (qb, 0))
    return pl.pallas_call(
        functools.partial(_swa_prompt_kernel, n_heads=n_heads, n_kv=n_kv, hd=hd, blk=blk),
        grid=(batch, nb),
        in_specs=[
            pl.BlockSpec((blk, qw), lambda b, qb: (cur_row(b, qb), 0)),
            pl.BlockSpec((blk, kw), lambda b, qb: (cur_row(b, qb), kcol)),
            pl.BlockSpec((blk, kw), lambda b, qb: (prev_row(b, qb), kcol)),
            pl.BlockSpec((blk, kw), lambda b, qb: (cur_row(b, qb), kcol + 1)),
            pl.BlockSpec((blk, kw), lambda b, qb: (prev_row(b, qb), kcol + 1)),
            tab_cur, tab_cur, tab_cur, tab_prev, tab_prev, tab_prev,
            pl.BlockSpec((1, n_heads), lambda b, qb: (0, 0)),
        ],
        out_specs=[
            pl.BlockSpec((blk, qw), lambda b, qb: (cur_row(b, qb), 0)),
            pl.BlockSpec((1, blk, kw), lambda b, qb: (b, 0, 0)),
        ],
        out_shape=[
            jax.ShapeDtypeStruct((qkv.shape[0], qw), BF16),
            jax.ShapeDtypeStruct((batch, blk, kw), F32),
        ],
        compiler_params=_params("parallel", "arbitrary"),
        name="swa_prompt",
    )(qkv, qkv, qkv, qkv, qkv, *cur, *prev, sinks.reshape(1, n_heads))


def _swa_sample_kernel(q_ref, knew_ref, vnew_ref, bk_ref, bv_ref, qc_ref, qu_ref, qd_ref, kc_ref, ku_ref, kd_ref,
                       sink_ref, o_ref, ok_ref, ov_ref, kall_ref, vall_ref, *, n_heads, n_kv, hd, n_t):
    scale = hd ** -0.5
    half = hd // 8
    n_g = n_heads // n_kv
    n_rows = n_t * n_heads
    nbuf = bk_ref.shape[1]
    width = n_kv * hd
    n_all = kall_ref.shape[0]
    knew = _rope(knew_ref[0], kc_ref[...], ku_ref[...], kd_ref[...], half)
    kall_ref[...] = jnp.zeros_like(kall_ref)
    vall_ref[...] = jnp.zeros_like(vall_ref)
    kall_ref[0:nbuf, :] = bk_ref[0]
    vall_ref[0:nbuf, :] = bv_ref[0]
    kall_ref[nbuf:nbuf + n_t, :] = knew
    vall_ref[nbuf:nbuf + n_t, :] = vnew_ref[0]
    ok_ref[0] = kall_ref[n_t:n_t + nbuf, :]
    ov_ref[0] = vall_ref[n_t:n_t + nbuf, :]
    q = q_ref[0]
    up = jnp.concatenate([q[:, half:], q[:, :half]], axis=1)
    dn = jnp.concatenate([q[:, hd - half:], q[:, :hd - half]], axis=1)
    q = q * qc_ref[...] + up * qu_ref[...] + dn * qd_ref[...]
    qt = jnp.concatenate([q] * n_kv, axis=1)
    row_kv = (lax.broadcasted_iota(jnp.int32, (n_rows, width), 0) % n_heads) // n_g
    lane_kv = lax.broadcasted_iota(jnp.int32, (n_rows, width), 1) // hd
    qbd = jnp.where(row_kv == lane_kv, qt, 0.0).astype(BF16)
    s = lax.dot_general(qbd, kall_ref[...].astype(BF16), NT_DIMS, preferred_element_type=F32) * scale
    tok = lax.broadcasted_iota(jnp.int32, (n_rows, n_all), 0) // n_heads
    key = lax.broadcasted_iota(jnp.int32, (n_rows, n_all), 1)
    band = ((key < nbuf) & (tok + nbuf - key < WINDOW)) | ((key >= nbuf) & (key - nbuf <= tok))
    s = jnp.where(band, s, -jnp.inf)
    sink = sink_ref[...]
    m = jnp.maximum(jnp.max(s, axis=1, keepdims=True), sink)
    e = jnp.exp(s - m)
    p = e / (jnp.sum(e, axis=1, keepdims=True) + jnp.exp(sink - m))
    out = jnp.dot(p.astype(BF16), vall_ref[...].astype(BF16), preferred_element_type=F32)
    row_kv = (lax.broadcasted_iota(jnp.int32, (n_rows, hd), 0) % n_heads) // n_g
    res = jnp.zeros((n_rows, hd), F32)
    for kv in range(n_kv):
        res = jnp.where(row_kv == kv, out[:, kv * hd:(kv + 1) * hd], res)
    o_ref[0] = res


def swa_sample(q_rows, qkv_s, buf_k, buf_v, sinks, past, *, n_heads, n_kv, hd):
    bsz, n_t = qkv_s.shape[0], qkv_s.shape[1]
    nbuf = buf_k.shape[1]
    assert nbuf == WINDOW and n_t <= 8
    width = n_kv * hd
    n_rows = n_t * n_heads
    n_all = nbuf + LANES
    pos = past + jnp.arange(n_t)
    qtab = tuple(jnp.repeat(t, n_heads, axis=0) for t in _rope_tables(pos, hd, 1))
    ktab = _rope_tables(pos, hd, LANES // hd)
    sink_rows = jnp.tile(sinks.astype(F32), n_t).reshape(n_rows, 1)
    kcol = n_heads * hd // width
    full2 = lambda shape: pl.BlockSpec(shape, lambda b: (0, 0))
    return pl.pallas_call(
        functools.partial(_swa_sample_kernel, n_heads=n_heads, n_kv=n_kv, hd=hd, n_t=n_t),
        grid=(bsz,),
        in_specs=[
            pl.BlockSpec((1, n_rows, hd), lambda b: (b, 0, 0)),
            pl.BlockSpec((1, n_t, width), lambda b: (b, 0, kcol)),
            pl.BlockSpec((1, n_t, width), lambda b: (b, 0, kcol + 1)),
            pl.BlockSpec((1, nbuf, width), lambda b: (b, 0, 0)),
            pl.BlockSpec((1, nbuf, width), lambda b: (b, 0, 0)),
            full2((n_rows, hd)), full2((n_rows, hd)), full2((n_rows, hd)),
            full2((n_t, LANES)), full2((n_t, LANES)), full2((n_t, LANES)),
            full2((n_rows, 1)),
        ],
        out_specs=[
            pl.BlockSpec((1, n_rows, hd), lambda b: (b, 0, 0)),
            pl.BlockSpec((1, nbuf, width), lambda b: (b, 0, 0)),
            pl.BlockSpec((1, nbuf, width), lambda b: (b, 0, 0)),
        ],
        out_shape=[
            jax.ShapeDtypeStruct((bsz, n_rows, hd), F32),
            jax.ShapeDtypeStruct((bsz, nbuf, width), F32),
            jax.ShapeDtypeStruct((bsz, nbuf, width), F32),
        ],
        scratch_shapes=[pltpu.VMEM((n_all, width), F32), pltpu.VMEM((n_all, width), F32)],
        compiler_params=_params("parallel"),
        name="swa_sample",
    )(q_rows, qkv_s, qkv_s, buf_k, buf_v, *qtab, *ktab, sink_rows)


def _rglru_kernel(gate_ref, u_ref, cb_ref, h0_ref, cw_ref, cbias_ref, wa_ref, ba_ref, wx_ref, bx_ref, lam_ref,
                  y_ref, hlast_ref, cout_ref, uext_ref, a_ref, d_ref, h_ref, *, n_t, n_seq, chunk):
    tc = pl.program_id(1)
    rows = n_t * n_seq
    tail = (cw_ref.shape[0] - 1) * n_seq
    head = uext_ref.shape[0] - rows
    n_blocks, cb = wa_ref.shape[0], wa_ref.shape[1]

    @pl.when(tc == 0)
    def _():
        uext_ref[head - tail:head, :] = cb_ref[0]
        h_ref[...] = h0_ref[0]

    @pl.when(tc > 0)
    def _():
        uext_ref[head - tail:head, :] = uext_ref[head + rows - tail:head + rows, :]

    uext_ref[head:head + rows, :] = u_ref[...]
    log_sig_lam = jax.nn.log_sigmoid(lam_ref[...])
    for c0 in range(0, rows, chunk):
        n = min(chunk, rows - c0)
        xc = cbias_ref[...]
        for i in range(cw_ref.shape[0]):
            xc = xc + uext_ref[head - tail + i * n_seq + c0:head - tail + i * n_seq + c0 + n, :] * cw_ref[i:i + 1, :]
        xb = xc.astype(BF16)
        r = jnp.concatenate([jnp.dot(xb[:, j * cb:(j + 1) * cb], wa_ref[j], preferred_element_type=F32)
                             for j in range(n_blocks)], axis=1)
        ig = jnp.concatenate([jnp.dot(xb[:, j * cb:(j + 1) * cb], wx_ref[j], preferred_element_type=F32)
                              for j in range(n_blocks)], axis=1)
        r = jax.nn.sigmoid(r + ba_ref[...])
        ig = jax.nn.sigmoid(ig + bx_ref[...])
        log_a = LRU_C * r * log_sig_lam
        a_ref[c0:c0 + n, :] = jnp.exp(log_a)
        d_ref[c0:c0 + n, :] = jnp.sqrt(-_expm1(2.0 * log_a)) * (ig * xc)

    def step(t, h):
        sl = pl.ds(pl.multiple_of(t * n_seq, n_seq), n_seq)
        h = a_ref[sl, :] * h + d_ref[sl, :]
        d_ref[sl, :] = h
        return h

    h_last = lax.fori_loop(0, n_t, step, h_ref[...], unroll=min(8, n_t))
    h_ref[...] = h_last
    for c0 in range(0, rows, chunk):
        n = min(chunk, rows - c0)
        y_ref[c0:c0 + n, :] = (d_ref[c0:c0 + n, :] * gate_ref[c0:c0 + n, :]).astype(y_ref.dtype)

    @pl.when(tc == pl.num_programs(1) - 1)
    def _():
        hlast_ref[0] = h_last
        cout_ref[0] = uext_ref[head + rows - tail:head + rows, :]


def rglru(gu, conv_buf, h0, conv_w, conv_b, w_a, b_a, w_x, b_x, lam, *, n_groups, n_t, n_seq, t_chunk, d_rnn):
    assert n_t % t_chunk == 0
    n_tc = n_t // t_chunk
    rows = t_chunk * n_seq
    tail = conv_buf.shape[1]
    assert tail <= rows
    head = -(-tail // 8) * 8
    chunk = min(rows, 256)
    vec = lambda a: a.reshape(1, d_rnn)
    const2 = lambda g, t: (0, 0)
    const3 = lambda g, t: (0, 0, 0)
    per_group = lambda g, t: (g, 0, 0)
    return pl.pallas_call(
        functools.partial(_rglru_kernel, n_t=t_chunk, n_seq=n_seq, chunk=chunk),
        grid=(n_groups, n_tc),
        in_specs=[
            pl.BlockSpec((rows, d_rnn), lambda g, t: (g * n_tc + t, 0)),
            pl.BlockSpec((rows, d_rnn), lambda g, t: (g * n_tc + t, 1)),
            pl.BlockSpec((1, tail, d_rnn), per_group),
            pl.BlockSpec((1, n_seq, d_rnn), per_group),
            pl.BlockSpec(conv_w.shape, const2),
            pl.BlockSpec((1, d_rnn), const2),
            pl.BlockSpec(w_a.shape, const3),
            pl.BlockSpec((1, d_rnn), const2),
            pl.BlockSpec(w_x.shape, const3),
            pl.BlockSpec((1, d_rnn), const2),
            pl.BlockSpec((1, d_rnn), const2),
        ],
        out_specs=[
            pl.BlockSpec((rows, d_rnn), lambda g, t: (g * n_tc + t, 0)),
            pl.BlockSpec((1, n_seq, d_rnn), per_group),
            pl.BlockSpec((1, tail, d_rnn), per_group),
        ],
        out_shape=[
            jax.ShapeDtypeStruct((gu.shape[0], d_rnn), BF16),
            jax.ShapeDtypeStruct((n_groups, n_seq, d_rnn), F32),
            jax.ShapeDtypeStruct((n_groups, tail, d_rnn), F32),
        ],
        scratch_shapes=[
            pltpu.VMEM((head + rows, d_rnn), F32),
            pltpu.VMEM((rows, d_rnn), F32),
            pltpu.VMEM((rows, d_rnn), F32),
            pltpu.VMEM((n_seq, d_rnn), F32),
        ],
        compiler_params=_params("parallel", "arbitrary"),
        name="rglru",
    )(gu, gu, conv_buf, h0, conv_w, vec(conv_b), w_a, vec(b_a), w_x, vec(b_x), vec(lam))


TM = 640
TN_MAX = 1280
TF = 1024


def _proj_tile(n):
    return max(t for t in range(LANES, TN_MAX + 1, LANES) if n % t == 0)
FOX_TQ = 512
FOX_BLK = 256
FOX_PAGES = 16
LRU_T_CHUNK = 256


def kernel(x_prompt, x_sample, cache_fox_k, cache_fox_v, cache_fox_logf, cache_swa_k, cache_swa_v, state_lru_h, state_lru_conv, page_table, norm_mix_pre, norm_mix_post, norm_mlp_pre, norm_mlp_post, mlp_w_up, mlp_w_down, fox_w_qkv, fox_w_f, fox_b_f, fox_w_o, swa_w_qkv, swa_b_qkv, swa_sinks, swa_w_o, swa_b_o, lru_w_gate, lru_b_gate, lru_w_in, lru_b_in, lru_conv_w, lru_conv_b, lru_w_a, lru_b_a, lru_w_x, lru_b_x, lru_lambda, lru_w_out, lru_b_out):
    bp, seq, d = x_prompt.shape
    bs, n_t, _ = x_sample.shape
    mp, ms = bp * seq, bs * n_t
    depth = norm_mix_pre.shape[0]
    fox_heads = fox_w_f.shape[2]
    fox_kv, fox_hd = cache_fox_k.shape[3], cache_fox_k.shape[4]
    swa_heads = swa_sinks.shape[1]
    swa_kv, swa_hd = cache_swa_k.shape[3], cache_swa_k.shape[4]
    d_rnn = lru_w_gate.shape[2]
    conv_w = lru_conv_w.shape[1]
    past = page_table.shape[1] * cache_fox_k.shape[2]
    fox_qw, fox_kw = fox_heads * fox_hd, fox_kv * fox_hd
    swa_qw, swa_kw = swa_heads * swa_hd, swa_kv * swa_hd

    x = jnp.concatenate([x_prompt.reshape(mp, d), x_sample.reshape(ms, d)], axis=0)
    n_layers_fox, pool, page = cache_fox_k.shape[:3]
    ck = cache_fox_k.reshape(n_layers_fox, pool, page * fox_kv, fox_hd)
    cv = cache_fox_v.reshape(n_layers_fox, pool, page * fox_kv, fox_hd)
    clf_t = jnp.swapaxes(cache_fox_logf, 2, 3)

    fkp, fvp, flp, fks, fvs, fls = [], [], [], [], [], []
    skp, svp, sks, svs = [], [], [], []
    lhp, lcp, lhs, lcs = [], [], [], []
    for i in range(depth):
        kind, j = i % 3, i // 3
        if kind == 0:
            qkv, logf = norm_matmul(x, norm_mix_pre[i], fox_w_qkv[j].astype(BF16),
                                    gate=(fox_w_f[j].astype(BF16), fox_b_f[j]), tm=TM,
                                    tn=_proj_tile(fox_w_qkv.shape[2]))
            kp = qkv[:mp, fox_qw:fox_qw + fox_kw]
            vp = qkv[:mp, fox_qw + fox_kw:]
            fkp.append(kp.reshape(bp, seq, fox_kv, fox_hd))
            fvp.append(vp.reshape(bp, seq, fox_kv, fox_hd))
            flp.append(logf[:mp].reshape(bp, seq, fox_heads))
            qkv_s = qkv[mp:].reshape(bs, n_t, -1)
            lf_s = logf[mp:].reshape(bs, n_t, fox_heads)
            fks.append(qkv_s[:, :, fox_qw:fox_qw + fox_kw].reshape(bs, n_t, fox_kv, fox_hd))
            fvs.append(qkv_s[:, :, fox_qw + fox_kw:].reshape(bs, n_t, fox_kv, fox_hd))
            fls.append(lf_s)
            qx, kx = fox_cumsum(logf[:mp].reshape(bp, seq, fox_heads), n_kv=fox_kv, chunk=FOX_BLK)
            att_p = fox_flash(qkv, qx, kx, batch=bp, seq=seq, n_heads=fox_heads, n_kv=fox_kv, hd=fox_hd,
                              tq=FOX_TQ, blk=FOX_BLK)
            fox_g = fox_heads // fox_kv
            q_rows = qkv_s[:, :, :fox_qw].reshape(bs, n_t, fox_kv, fox_g, fox_hd).transpose(0, 2, 1, 3, 4)
            q_rows = q_rows.reshape(bs, fox_kv, n_t * fox_g, fox_hd)
            lf_new_t = jnp.pad(jnp.swapaxes(lf_s, 1, 2), ((0, 0), (0, 0), (0, page - n_t)))
            att_s = fox_decode(q_rows, qkv_s, lf_new_t, ck, cv, clf_t, page_table, j,
                               n_heads=fox_heads, n_kv=fox_kv, hd=fox_hd, pages=FOX_PAGES)
            att_s = att_s.reshape(bs, fox_kv, n_t, fox_g, fox_hd).transpose(0, 2, 1, 3, 4).reshape(ms, fox_qw)
            att = lax.dynamic_update_slice(att_p, att_s.astype(BF16), (mp, 0))
            x = matmul_norm_residual(att, fox_w_o[j].astype(BF16), None, norm_mix_post[i], x, tm=TM)
        elif kind == 1:
            qkv = norm_matmul(x, norm_mix_pre[i], swa_w_qkv[j].astype(BF16), swa_b_qkv[j], tm=TM,
                              tn=_proj_tile(swa_w_qkv.shape[2]))
            att_p, k_last = swa_prompt(qkv, swa_sinks[j], batch=bp, seq=seq, n_heads=swa_heads, n_kv=swa_kv,
                                       hd=swa_hd, blk=WINDOW)
            keep = min(WINDOW, seq)
            skp.append(k_last.reshape(bp, keep, swa_kv, swa_hd))
            v_last = qkv[:mp].reshape(bp, seq, -1)[:, seq - keep:, swa_qw + swa_kw:]
            svp.append(v_last.reshape(bp, keep, swa_kv, swa_hd))
            qkv_s = qkv[mp:].reshape(bs, n_t, -1)
            q_rows = qkv_s[:, :, :swa_qw].reshape(bs, n_t * swa_heads, swa_hd)
            nbuf = cache_swa_k.shape[2]
            att_s, nk, nv = swa_sample(q_rows, qkv_s, cache_swa_k[j].reshape(bs, nbuf, swa_kw),
                                       cache_swa_v[j].reshape(bs, nbuf, swa_kw), swa_sinks[j], past,
                                       n_heads=swa_heads, n_kv=swa_kv, hd=swa_hd)
            sks.append(nk.reshape(bs, nbuf, swa_kv, swa_hd))
            svs.append(nv.reshape(bs, nbuf, swa_kv, swa_hd))
            att = lax.dynamic_update_slice(att_p, att_s.reshape(ms, swa_qw).astype(BF16), (mp, 0))
            x = matmul_norm_residual(att, swa_w_o[j].astype(BF16), swa_b_o[j], norm_mix_post[i], x, tm=TM)
        else:
            w_gu = jnp.concatenate([lru_w_gate[j], lru_w_in[j]], axis=1).astype(BF16)
            b_gu = jnp.concatenate([lru_b_gate[j], lru_b_in[j]])
            gu = norm_matmul(x, norm_mix_pre[i], w_gu, b_gu, n_gelu_cols=d_rnn, tm=TM, tn=_proj_tile(d_rnn))
            lw = (lru_conv_w[j], lru_conv_b[j], lru_w_a[j].astype(BF16), lru_b_a[j], lru_w_x[j].astype(BF16),
                  lru_b_x[j], lru_lambda[j])
            y_p, h_p, c_p = rglru(gu, jnp.zeros((bp, conv_w - 1, d_rnn), F32), jnp.zeros((bp, 1, d_rnn), F32),
                                  *lw, n_groups=bp, n_t=seq, n_seq=1, t_chunk=LRU_T_CHUNK, d_rnn=d_rnn)
            lhp.append(h_p.reshape(bp, d_rnn))
            lcp.append(c_p)
            gu_s = gu[mp:].reshape(bs, n_t, -1).swapaxes(0, 1).reshape(ms, -1)
            cb_s = state_lru_conv[j].swapaxes(0, 1).reshape(1, (conv_w - 1) * bs, d_rnn)
            y_s, h_s, c_s = rglru(gu_s, cb_s, state_lru_h[j].reshape(1, bs, d_rnn), *lw,
                                  n_groups=1, n_t=n_t, n_seq=bs, t_chunk=n_t, d_rnn=d_rnn)
            lhs.append(h_s.reshape(bs, d_rnn))
            lcs.append(c_s.reshape(conv_w - 1, bs, d_rnn).swapaxes(0, 1))
            y_s = y_s.reshape(n_t, bs, d_rnn).swapaxes(0, 1).reshape(ms, d_rnn)
            y = lax.dynamic_update_slice(y_p, y_s, (mp, 0))
            x = matmul_norm_residual(y, lru_w_out[j].astype(BF16), lru_b_out[j], norm_mix_post[i], x, tm=TM)
        x = mlp_sublayer(x, norm_mlp_pre[i], mlp_w_up[i].astype(BF16), mlp_w_down[i].astype(BF16),
                         norm_mlp_post[i], tm=TM, tf=TF)
    return (x[:mp].reshape(bp, seq, d), x[mp:].reshape(bs, n_t, d),
            jnp.stack(fkp), jnp.stack(fvp), jnp.stack(flp),
            jnp.stack(fks), jnp.stack(fvs), jnp.stack(fls),
            jnp.stack(skp), jnp.stack(svp), jnp.stack(sks), jnp.stack(svs),
            jnp.stack(lhp), jnp.stack(lcp), jnp.stack(lhs), jnp.stack(lcs))
```

```python
import functools

import jax
import jax.numpy as jnp
from jax import lax
from jax.experimental import pallas as pl
from jax.experimental.pallas import tpu as pltpu

F32 = jnp.float32
BF16 = jnp.bfloat16

RMS_EPS = 1e-6
WINDOW = 128
ROPE_THETA = 500000.0
LRU_C = 8.0
LOG2E = 1.4426950408889634

V7X_VMEM_LIMIT_BYTES = 56 * 1024 * 1024
LANES = 128

NT_DIMS = (((1,), (1,)), ((), ()))


def _params(*sem):
    return pltpu.CompilerParams(dimension_semantics=sem, vmem_limit_bytes=V7X_VMEM_LIMIT_BYTES)


def _rms(x, g):
    ms = jnp.mean(x * x, axis=-1, keepdims=True)
    return x * lax.rsqrt(ms + RMS_EPS) * g


def _expm1(x):
    u = jnp.exp(x)
    um1 = u - 1.0
    safe = jnp.where(u == 1.0, 1.0, jnp.log(u))
    return jnp.where(u == 1.0, x, jnp.where(um1 == -1.0, -1.0, um1 * x / safe))


def _split3(x):
    hi = x.astype(BF16)
    r1 = x - hi.astype(F32)
    mid = r1.astype(BF16)
    lo = (r1 - mid.astype(F32)).astype(BF16)
    return hi, mid, lo


def _norm_mm_kernel(*refs, n_gelu_tiles, has_bias, has_gate, kv_rows):
    it = iter(refs)
    x_ref, g_ref, w_ref = next(it), next(it), next(it)
    b_ref = next(it) if has_bias else None
    wf_ref, bf_ref = (next(it), next(it)) if has_gate else (None, None)
    o_ref = next(it)
    lf_ref = next(it) if has_gate else None
    krows_ref, vrows_ref = (next(it), next(it)) if kv_rows else (None, None)
    xn_ref = next(it)
    j = pl.program_id(1)

    @pl.when(j == 0)
    def _():
        xn = _rms(x_ref[...], g_ref[...]).astype(BF16)
        xn_ref[...] = xn
        if has_gate:
            z = jnp.dot(xn, wf_ref[...], preferred_element_type=F32) + bf_ref[...]
            lf_ref[...] = jax.nn.log_sigmoid(z)

    acc = jnp.dot(xn_ref[...], w_ref[...], preferred_element_type=F32)
    if has_bias:
        acc = acc + b_ref[...]
    if kv_rows:
        n_kv, hd, off = kv_rows
        tm = o_ref.shape[0]

        @pl.when(j == pl.num_programs(1) - 1)
        def _():
            for kv in range(n_kv):
                krows_ref[pl.ds(kv, tm, stride=n_kv), :] = acc[:, off + kv * hd:off + (kv + 1) * hd]
                vrows_ref[pl.ds(kv, tm, stride=n_kv), :] = acc[:, off + (n_kv + kv) * hd:off + (n_kv + kv + 1) * hd]

    if n_gelu_tiles == 0:
        o_ref[...] = acc.astype(o_ref.dtype)
    else:
        @pl.when(j < n_gelu_tiles)
        def _():
            o_ref[...] = jax.nn.gelu(acc).astype(o_ref.dtype)

        @pl.when(j >= n_gelu_tiles)
        def _():
            o_ref[...] = acc.astype(o_ref.dtype)


def norm_matmul(x, g, w, b=None, *, gate=None, kv_heads=None, n_gelu_cols=0, tm, tn):
    m, d = x.shape
    n = w.shape[1]
    assert m % tm == 0 and n % tn == 0 and n_gelu_cols % tn == 0
    has_bias, has_gate = b is not None, gate is not None
    kv_rows = None
    if kv_heads is not None:
        n_kv, hd = kv_heads
        off = tn - 2 * n_kv * hd
        assert off >= 0 and hd % LANES == 0
        kv_rows = (n_kv, hd, off)
    args = [x, g.reshape(1, d), w]
    in_specs = [
        pl.BlockSpec((tm, d), lambda i, j: (i, 0)),
        pl.BlockSpec((1, d), lambda i, j: (0, 0)),
        pl.BlockSpec((d, tn), lambda i, j: (0, j)),
    ]
    if has_bias:
        args.append(b.reshape(1, n))
        in_specs.append(pl.BlockSpec((1, tn), lambda i, j: (0, j)))
    out_shape = [jax.ShapeDtypeStruct((m, n), F32)]
    out_specs = [pl.BlockSpec((tm, tn), lambda i, j: (i, j))]
    if has_gate:
        w_f, b_f = gate
        h = w_f.shape[1]
        args += [w_f, b_f.reshape(1, h)]
        in_specs += [pl.BlockSpec((d, h), lambda i, j: (0, 0)), pl.BlockSpec((1, h), lambda i, j: (0, 0))]
        out_shape.append(jax.ShapeDtypeStruct((m, h), F32))
        out_specs.append(pl.BlockSpec((tm, h), lambda i, j: (i, 0)))
    if kv_rows:
        out_shape += [jax.ShapeDtypeStruct((m * n_kv, hd), F32)] * 2
        out_specs += [pl.BlockSpec((tm * n_kv, hd), lambda i, j: (i, 0))] * 2
    outs = pl.pallas_call(
        functools.partial(_norm_mm_kernel, n_gelu_tiles=n_gelu_cols // tn, has_bias=has_bias, has_gate=has_gate,
                          kv_rows=kv_rows),
        grid=(m // tm, n // tn),
        in_specs=in_specs,
        out_specs=out_specs,
        out_shape=out_shape,
        scratch_shapes=[pltpu.VMEM((tm, d), BF16)],
        compiler_params=_params("parallel", "arbitrary"),
        name="norm_matmul",
    )(*args)
    return outs if len(outs) > 1 else outs[0]


def _mm_norm_res_kernel(*refs, has_bias):
    it = iter(refs)
    a_ref, w_ref = next(it), next(it)
    b_ref = next(it) if has_bias else None
    g_ref, x_ref, o_ref = next(it), next(it), next(it)
    mix = jnp.dot(a_ref[...], w_ref[...], preferred_element_type=F32)
    if has_bias:
        mix = mix + b_ref[...]
    o_ref[...] = x_ref[...] + _rms(mix, g_ref[...])


def matmul_norm_residual(a, w, b, g, x, *, tm):
    m, k = a.shape
    d = w.shape[1]
    assert m % tm == 0
    has_bias = b is not None
    args = [a, w]
    in_specs = [pl.BlockSpec((tm, k), lambda i: (i, 0)), pl.BlockSpec((k, d), lambda i: (0, 0))]
    if has_bias:
        args.append(b.reshape(1, d))
        in_specs.append(pl.BlockSpec((1, d), lambda i: (0, 0)))
    args += [g.reshape(1, d), x]
    in_specs += [pl.BlockSpec((1, d), lambda i: (0, 0)), pl.BlockSpec((tm, d), lambda i: (i, 0))]
    return pl.pallas_call(
        functools.partial(_mm_norm_res_kernel, has_bias=has_bias),
        grid=(m // tm,),
        in_specs=in_specs,
        out_specs=pl.BlockSpec((tm, d), lambda i: (i, 0)),
        out_shape=jax.ShapeDtypeStruct((m, d), F32),
        compiler_params=_params("parallel"),
        name="matmul_norm_residual",
    )(*args)


def _mlp_kernel(x_ref, gpre_ref, wup_ref, wdn_ref, gpost_ref, o_ref, xn_ref, acc_ref):
    f = pl.program_id(1)

    @pl.when(f == 0)
    def _():
        xn_ref[...] = _rms(x_ref[...], gpre_ref[...]).astype(BF16)
        acc_ref[...] = jnp.zeros_like(acc_ref)

    h = jnp.dot(xn_ref[...], wup_ref[0], preferred_element_type=F32)
    h = jnp.square(jnp.maximum(h, 0.0)).astype(BF16)
    acc_ref[...] += jnp.dot(h, wdn_ref[0], preferred_element_type=F32)

    @pl.when(f == pl.num_programs(1) - 1)
    def _():
        o_ref[...] = x_ref[...] + _rms(acc_ref[...], gpost_ref[...])


def mlp_sublayer(x, g_pre, w_up, w_down, g_post, layer, *, tm, tf):
    m, d = x.shape
    dff = w_up.shape[2]
    assert m % tm == 0 and dff % tf == 0
    return pl.pallas_call(
        _mlp_kernel,
        grid=(m // tm, dff // tf),
        in_specs=[
            pl.BlockSpec((tm, d), lambda i, f: (i, 0)),
            pl.BlockSpec((1, d), lambda i, f: (0, 0)),
            pl.BlockSpec((1, d, tf), lambda i, f: (layer, 0, f)),
            pl.BlockSpec((1, tf, d), lambda i, f: (layer, f, 0)),
            pl.BlockSpec((1, d), lambda i, f: (0, 0)),
        ],
        out_specs=pl.BlockSpec((tm, d), lambda i, f: (i, 0)),
        out_shape=jax.ShapeDtypeStruct((m, d), F32),
        scratch_shapes=[pltpu.VMEM((tm, d), BF16), pltpu.VMEM((tm, d), F32)],
        compiler_params=_params("parallel", "arbitrary"),
        name="mlp_sublayer",
    )(x, g_pre.reshape(1, d), w_up, w_down, g_post.reshape(1, d))


N_PARTS = 3


def _fox_cumsum_kernel(lf_ref, qx_ref, kx_ref, *, chunk, n_kv, n_g):
    s, h = lf_ref.shape[1], lf_ref.shape[2]
    row = lax.broadcasted_iota(jnp.int32, (chunk, chunk), 0)
    col = lax.broadcasted_iota(jnp.int32, (chunk, chunk), 1)
    lower = (row >= col).astype(BF16)

    def placement(n_tiles, lane_of, head_of):
        r = lax.broadcasted_iota(jnp.int32, (h, n_tiles * LANES), 0)
        c = lax.broadcasted_iota(jnp.int32, (h, n_tiles * LANES), 1)
        tile, lane = c // LANES, c % LANES
        mats = []
        for p in range(N_PARTS):
            hit = jnp.zeros((h, n_tiles * LANES), jnp.bool_)
            for g in range(n_g):
                hit = hit | ((r == head_of(tile, g)) & (lane == lane_of(g, p)))
            mats.append(hit)
        return mats, tile, lane

    q_hit, q_tile, q_lane = placement(h, lambda g, p: p, lambda tile, g: jnp.where(tile % n_g == g, tile, -1))
    q_mats = [m.astype(BF16) for m in q_hit]
    q_gl = N_PARTS * (1 + q_tile[0:1] % n_g)
    q_ones = ((q_lane[0:1] >= q_gl) & (q_lane[0:1] < q_gl + N_PARTS)).astype(F32)
    k_hit, _, k_lane = placement(n_kv, lambda g, p: N_PARTS * (1 + g) + p, lambda tile, g: tile * n_g + g)
    k_mats = [(-m.astype(F32)).astype(BF16) for m in k_hit]
    k_ones = (k_lane[0:1] < N_PARTS).astype(F32)

    carry = jnp.zeros((1, h), F32)
    for c in range(s // chunk):
        rows = slice(c * chunk, (c + 1) * chunk)
        hi, mid, lo = _split3(lf_ref[0, rows, :])
        cs = (jnp.dot(lower, hi, preferred_element_type=F32)
              + jnp.dot(lower, mid, preferred_element_type=F32)
              + jnp.dot(lower, lo, preferred_element_type=F32)) + carry
        carry = cs[chunk - 1:chunk, :]
        parts = _split3(cs * LOG2E)
        qx = q_ones + sum(jnp.dot(parts[p], q_mats[p], preferred_element_type=F32) for p in range(N_PARTS))
        kx = k_ones + sum(jnp.dot(parts[p], k_mats[p], preferred_element_type=F32) for p in range(N_PARTS))
        for hh in range(h):
            qx_ref[0, hh, rows, :] = qx[:, hh * LANES:(hh + 1) * LANES].astype(BF16)
        for kv in range(n_kv):
            kx_ref[0, kv, rows, :] = kx[:, kv * LANES:(kv + 1) * LANES].astype(BF16)


def fox_cumsum(logf, *, n_kv, chunk):
    b, s, h = logf.shape
    n_g = h // n_kv
    assert N_PARTS * (1 + n_g) <= LANES
    return pl.pallas_call(
        functools.partial(_fox_cumsum_kernel, chunk=chunk, n_kv=n_kv, n_g=n_g),
        grid=(b,),
        in_specs=[pl.BlockSpec((1, s, h), lambda i: (i, 0, 0))],
        out_specs=[
            pl.BlockSpec((1, h, s, LANES), lambda i: (i, 0, 0, 0)),
            pl.BlockSpec((1, n_kv, s, LANES), lambda i: (i, 0, 0, 0)),
        ],
        out_shape=[
            jax.ShapeDtypeStruct((b, h, s, LANES), BF16),
            jax.ShapeDtypeStruct((b, n_kv, s, LANES), BF16),
        ],
        compiler_params=_params("parallel"),
        name="fox_cumsum",
    )(logf)


def _lane_tiles(x):
    return [x[:, j * LANES:(j + 1) * LANES] for j in range(x.shape[1] // LANES)]


def _fox_flash_kernel(q_ref, k_ref, v_ref, qx_ref, kx_ref, o_ref, ka_ref, vb_ref, s_ref, m_ref, l_ref, acc_ref,
                      *, tq, blk, n_g, hd):
    qi = pl.program_id(2)
    per_q = tq // blk

    @pl.when(qi == 0)
    def _():
        ka_ref[:, :hd] = k_ref[...].astype(BF16)
        ka_ref[:, hd:] = kx_ref[0, 0]
        vb_ref[...] = v_ref[...].astype(BF16)

    q_scale = hd ** -0.5 * LOG2E
    qa = [jnp.concatenate([(q_ref[:, g * hd:(g + 1) * hd] * q_scale).astype(BF16), qx_ref[0, g]], axis=1)
          for g in range(n_g)]
    m_ref[...] = jnp.full_like(m_ref, -jnp.inf)

    def logits_pass(kc, diag):
        ka = ka_ref[pl.ds(pl.multiple_of(kc * blk, blk), blk), :]
        if diag is not None:
            row = lax.broadcasted_iota(jnp.int32, (tq, blk), 0)
            col = lax.broadcasted_iota(jnp.int32, (tq, blk), 1) + diag * blk
            causal = row >= col
        for g in range(n_g):
            s = lax.dot_general(qa[g], ka, NT_DIMS, preferred_element_type=F32)
            if diag is not None:
                s = jnp.where(causal, s, -jnp.inf)
            s_ref[g, kc] = s
            m = m_ref[g]
            for t in _lane_tiles(s):
                m = jnp.maximum(m, t)
            m_ref[g] = m

    def body1(kc, carry):
        logits_pass(kc, None)
        return carry

    lax.fori_loop(0, per_q * qi, body1, 0)
    for e in range(per_q):
        logits_pass(per_q * qi + e, e)
    for g in range(n_g):
        m_ref[g] = jnp.broadcast_to(jnp.max(m_ref[g], axis=1, keepdims=True), (tq, LANES))
    l_ref[...] = jnp.zeros_like(l_ref)
    acc_ref[...] = jnp.zeros_like(acc_ref)

    def body2(kc, carry):
        vb = vb_ref[pl.ds(pl.multiple_of(kc * blk, blk), blk), :]
        for g in range(n_g):
            m = m_ref[g]
            p = [jnp.exp2(t - m) for t in _lane_tiles(s_ref[g, kc])]
            l_ref[g] += sum(p)
            acc_ref[g] += jnp.dot(jnp.concatenate(p, axis=1).astype(BF16), vb, preferred_element_type=F32)
        return carry

    lax.fori_loop(0, per_q * (qi + 1), body2, 0)
    for g in range(n_g):
        inv = 1.0 / jnp.sum(l_ref[g], axis=1, keepdims=True)
        o_ref[:, g * hd:(g + 1) * hd] = (acc_ref[g] * inv).astype(o_ref.dtype)


def fox_flash(qkv, qx, kx, *, batch, seq, n_heads, n_kv, hd, tq, blk):
    assert hd == LANES and tq % blk == 0 and seq % tq == 0
    n_g = n_heads // n_kv
    nq = seq // tq
    gw = n_g * hd
    return pl.pallas_call(
        functools.partial(_fox_flash_kernel, tq=tq, blk=blk, n_g=n_g, hd=hd),
        grid=(batch, n_kv, nq),
        in_specs=[
            pl.BlockSpec((tq, gw), lambda b, kv, qi: (b * nq + qi, kv)),
            pl.BlockSpec((seq, hd), lambda b, kv, qi: (b, n_heads + kv)),
            pl.BlockSpec((seq, hd), lambda b, kv, qi: (b, n_heads + n_kv + kv)),
            pl.BlockSpec((1, n_g, tq, LANES), lambda b, kv, qi: (b, kv, qi, 0)),
            pl.BlockSpec((1, 1, seq, LANES), lambda b, kv, qi: (b, kv, 0, 0)),
        ],
        out_specs=pl.BlockSpec((tq, gw), lambda b, kv, qi: (b * nq + qi, kv)),
        out_shape=jax.ShapeDtypeStruct((qkv.shape[0], n_heads * hd), BF16),
        scratch_shapes=[
            pltpu.VMEM((seq, hd + LANES), BF16),
            pltpu.VMEM((seq, hd), BF16),
            pltpu.VMEM((n_g, seq // blk, tq, blk), F32),
            pltpu.VMEM((n_g, tq, LANES), F32),
            pltpu.VMEM((n_g, tq, LANES), F32),
            pltpu.VMEM((n_g, tq, hd), F32),
        ],
        compiler_params=_params("parallel", "parallel", "arbitrary"),
        name="fox_flash",
    )(qkv, qkv, qkv, qx, kx)


def _fox_decode_kernel(pt_ref, q_ref, knew_ref, vnew_ref, lfnew_ref, *rest, pages, n_kv, n_heads, hd, n_t):
    del pt_ref
    k_refs, v_refs, lf_refs = rest[:pages], rest[pages:2 * pages], rest[2 * pages:3 * pages]
    o_ref, cq_ref, m_ref, l_ref, acc_ref, carry_ref, kpad_ref, vpad_ref = rest[3 * pages:]
    c = pl.program_id(1)
    scale = hd ** -0.5
    page = lf_refs[0].shape[3]
    n_g = n_heads // n_kv
    grp = n_t * n_g
    jj = lax.broadcasted_iota(jnp.int32, (page, page), 0)
    ss = lax.broadcasted_iota(jnp.int32, (page, page), 1)

    def lane_sums(xs, tri):
        stacked = jnp.concatenate([part for x in xs for part in _split3(x)], axis=0)
        out = jnp.dot(stacked, tri, preferred_element_type=F32)
        n = N_PARTS * n_heads
        return [out[i * n:i * n + n_heads] + out[i * n + n_heads:i * n + 2 * n_heads]
                + out[i * n + 2 * n_heads:(i + 1) * n] for i in range(len(xs))]

    def group_rows(x, kv):
        return jnp.concatenate([x[kv * n_g:(kv + 1) * n_g]] * n_t, axis=0)

    q = [q_ref[0, kv].astype(BF16) for kv in range(n_kv)]

    @pl.when(c == 0)
    def _():
        c_new = lane_sums([lfnew_ref[0]], (jj <= ss).astype(BF16))[0]
        kpad_ref[...] = jnp.zeros_like(kpad_ref)
        vpad_ref[...] = jnp.zeros_like(vpad_ref)
        tok = lax.broadcasted_iota(jnp.int32, (grp, page), 0) // n_g
        key = lax.broadcasted_iota(jnp.int32, (grp, page), 1)
        for kv in range(n_kv):
            cq = jnp.concatenate([c_new[kv * n_g:(kv + 1) * n_g, t:t + 1] for t in range(n_t)], axis=0)
            cq_ref[kv] = cq
            kpad_ref[kv, 0:n_t, :] = knew_ref[0, :, kv * hd:(kv + 1) * hd]
            vpad_ref[kv, 0:n_t, :] = vnew_ref[0, :, kv * hd:(kv + 1) * hd]
            s = lax.dot_general(q[kv], kpad_ref[kv].astype(BF16), NT_DIMS, preferred_element_type=F32) * scale
            s = s + (cq - group_rows(c_new, kv))
            s = jnp.where(key <= tok, s, -jnp.inf)
            m = jnp.max(s, axis=1, keepdims=True)
            p = jnp.exp(s - m)
            m_ref[kv] = m
            l_ref[kv] = jnp.sum(p, axis=1, keepdims=True)
            acc_ref[kv] = jnp.dot(p.astype(BF16), vpad_ref[kv].astype(BF16), preferred_element_type=F32)
        carry_ref[...] = jnp.zeros_like(carry_ref)

    lfs = [lf_refs[i][0, 0] for i in range(pages)]
    local = lane_sums(lfs, (jj > ss).astype(BF16))
    carry = carry_ref[...]
    rs = []
    for i in range(pages):
        rs.append(local[i] + carry)
        carry = carry + (local[i][:, 0:1] + lfs[i][:, 0:1])
    carry_ref[...] = carry
    raw = [[lax.dot_general(q[kv], k_refs[i][0, 0, pl.ds(kv, page, stride=n_kv), :].astype(BF16), NT_DIMS,
                            preferred_element_type=F32) for i in range(pages)] for kv in range(n_kv)]
    probs, alphas = [], []
    for kv in range(n_kv):
        cq = cq_ref[kv]
        logits = [raw[kv][i] * scale + (cq + group_rows(rs[i], kv)) for i in range(pages)]
        m_old = m_ref[kv]
        tile_max = logits[0]
        for s in logits[1:]:
            tile_max = jnp.maximum(tile_max, s)
        m_new = jnp.maximum(m_old, jnp.max(tile_max, axis=1, keepdims=True))
        alpha = jnp.exp(m_old - m_new)
        ps = [jnp.exp(s - m_new) for s in logits]
        l_ref[kv] = alpha * l_ref[kv] + jnp.sum(sum(ps), axis=1, keepdims=True)
        m_ref[kv] = m_new
        probs.append([p.astype(BF16) for p in ps])
        alphas.append(alpha)
    for kv in range(n_kv):
        pv = sum(jnp.dot(probs[kv][i], v_refs[i][0, 0, pl.ds(kv, page, stride=n_kv), :].astype(BF16),
                         preferred_element_type=F32) for i in range(pages))
        acc_ref[kv] = alphas[kv] * acc_ref[kv] + pv

    @pl.when(c == pl.num_programs(1) - 1)
    def _():
        for kv in range(n_kv):
            o_ref[0, kv] = acc_ref[kv] / l_ref[kv]


def fox_decode(q_rows, qkv_s, lf_new_t, cache_k, cache_v, cache_lf_t, page_table, layer, *, n_heads, n_kv, hd, pages):
    bsz, n_t = qkv_s.shape[0], qkv_s.shape[1]
    n_pages = page_table.shape[1]
    page = cache_lf_t.shape[3]
    width = n_kv * hd
    grp = n_t * (n_heads // n_kv)
    assert n_pages % pages == 0 and page == LANES and cache_k.shape[2] == page * n_kv
    n_chunks = n_pages // pages

    def page_map(i):
        return lambda b, c, pt: (layer, pt[b, n_pages - 1 - (c * pages + i)], 0, 0)

    kv_col = n_heads * hd // width
    in_specs = [
        pl.BlockSpec((1, n_kv, grp, hd), lambda b, c, pt: (b, 0, 0, 0)),
        pl.BlockSpec((1, n_t, width), lambda b, c, pt: (b, 0, kv_col)),
        pl.BlockSpec((1, n_t, width), lambda b, c, pt: (b, 0, kv_col + 1)),
        pl.BlockSpec((1, n_heads, page), lambda b, c, pt: (b, 0, 0)),
    ]
    in_specs += [pl.BlockSpec((1, 1, page * n_kv, hd), page_map(i)) for i in range(pages)]
    in_specs += [pl.BlockSpec((1, 1, page * n_kv, hd), page_map(i)) for i in range(pages)]
    in_specs += [pl.BlockSpec((1, 1, n_heads, page), page_map(i)) for i in range(pages)]
    grid_spec = pltpu.PrefetchScalarGridSpec(
        num_scalar_prefetch=1,
        grid=(bsz, n_chunks),
        in_specs=in_specs,
        out_specs=pl.BlockSpec((1, n_kv, grp, hd), lambda b, c, pt: (b, 0, 0, 0)),
        scratch_shapes=[
            pltpu.VMEM((n_kv, grp, 1), F32),
            pltpu.VMEM((n_kv, grp, 1), F32),
            pltpu.VMEM((n_kv, grp, 1), F32),
            pltpu.VMEM((n_kv, grp, hd), F32),
            pltpu.VMEM((n_heads, 1), F32),
            pltpu.VMEM((n_kv, page, hd), F32),
            pltpu.VMEM((n_kv, page, hd), F32),
        ],
    )
    return pl.pallas_call(
        functools.partial(_fox_decode_kernel, pages=pages, n_kv=n_kv, n_heads=n_heads, hd=hd, n_t=n_t),
        grid_spec=grid_spec,
        out_shape=jax.ShapeDtypeStruct((bsz, n_kv, grp, hd), F32),
        compiler_params=_params("parallel", "arbitrary"),
        name="fox_decode",
    )(page_table, q_rows, qkv_s, qkv_s, lf_new_t, *([cache_k] * pages), *([cache_v] * pages), *([cache_lf_t] * pages))


def _rope_tables(pos, hd, reps):
    rot = hd // 4
    half = rot // 2
    inv_freq = ROPE_THETA ** (-jnp.arange(half, dtype=F32) / half)
    ang = pos.astype(F32)[:, None] * inv_freq[None, :]
    cos, sin = jnp.cos(ang), jnp.sin(ang)
    n = pos.shape[0]
    ones = jnp.ones((n, hd - rot), F32)
    zeros_h = jnp.zeros((n, half), F32)
    zeros_r = jnp.zeros((n, hd - rot), F32)
    c = jnp.concatenate([cos, cos, ones], axis=1)
    s_up = jnp.concatenate([-sin, zeros_h, zeros_r], axis=1)
    s_dn = jnp.concatenate([zeros_h, sin, zeros_r], axis=1)
    return tuple(jnp.tile(t, (1, reps)) for t in (c, s_up, s_dn))


def _rope(x, c, s_up, s_dn, half):
    w = x.shape[1]
    pieces = []
    for j in range(w // LANES):
        xj = x[:, j * LANES:(j + 1) * LANES]
        up = pltpu.roll(xj, LANES - half, 1)
        dn = pltpu.roll(xj, half, 1)
        pieces.append(xj * c + up * s_up + dn * s_dn)
    return pieces[0] if len(pieces) == 1 else jnp.concatenate(pieces, axis=1)


def _swa_prompt_kernel(q_ref, kc_ref, kp_ref, vc_ref, vp_ref, cc_ref, su_ref, sd_ref, pc_ref, pu_ref, pd_ref,
                       sink_ref, o_ref, klast_ref, *, n_heads, n_kv, hd, blk):
    qb = pl.program_id(1)
    half = hd // 8
    n_g = n_heads // n_kv
    per_tile = LANES // hd
    assert per_tile == 2 and n_g % per_tile == 0
    nk = 2 * blk
    q = _rope(q_ref[...], cc_ref[...], su_ref[...], sd_ref[...], half) * (hd ** -0.5 * LOG2E)
    kc = _rope(kc_ref[...], cc_ref[...], su_ref[...], sd_ref[...], half)
    kp = _rope(kp_ref[...], pc_ref[...], pu_ref[...], pd_ref[...], half)
    k = jnp.concatenate([kp, kc], axis=0)
    v = jnp.concatenate([vp_ref[...], vc_ref[...]], axis=0)
    lane = lax.broadcasted_iota(jnp.int32, (nk, LANES), 1)
    d_row = lax.broadcasted_iota(jnp.int32, (LANES, nk), 0)
    out_row = lax.broadcasted_iota(jnp.int32, (LANES, blk), 0)

    def spread_keys(kv):
        tile = k[:, (kv // per_tile) * LANES:(kv // per_tile + 1) * LANES]
        swapped = pltpu.roll(tile, hd, 1)
        low, high = (tile, swapped) if kv % per_tile == 0 else (swapped, tile)
        return jnp.concatenate([jnp.where(lane < hd, low, 0.0), jnp.where(lane >= hd, high, 0.0)],
                               axis=0).astype(BF16)

    def spread_values_t(kv):
        tile_t = v[:, (kv // per_tile) * LANES:(kv // per_tile + 1) * LANES].T
        swapped_t = jnp.concatenate([tile_t[hd:], tile_t[:hd]], axis=0)
        low, high = (tile_t, swapped_t) if kv % per_tile == 0 else (swapped_t, tile_t)
        return jnp.concatenate([jnp.where(d_row < hd, low, 0.0), jnp.where(d_row >= hd, high, 0.0)],
                               axis=1).astype(BF16)

    j = lax.broadcasted_iota(jnp.int32, (nk, blk), 0)
    t = lax.broadcasted_iota(jnp.int32, (nk, blk), 1)
    first_key = jnp.where(qb > 0, 0, blk)
    band = (j > t) & (j <= t + blk) & (j >= first_key)
    pairs = n_g // per_tile
    n_tiles = n_kv * pairs
    k2 = [spread_keys(kv) for kv in range(n_kv)]
    raw = [lax.dot_general(k2[i // pairs], q[:, i * LANES:(i + 1) * LANES].astype(BF16), NT_DIMS,
                           preferred_element_type=F32) for i in range(n_tiles)]
    weights, scales = [], []
    for i in range(n_tiles):
        es, invs = [], []
        for e in range(per_tile):
            h = per_tile * i + e
            se = jnp.where(band, raw[i][e * nk:(e + 1) * nk], -jnp.inf)
            sink = sink_ref[0:1, h:h + 1] * LOG2E
            m = jnp.maximum(jnp.max(se, axis=0, keepdims=True), sink)
            ee = jnp.exp2(se - m)
            es.append(ee)
            invs.append(1.0 / (jnp.sum(ee, axis=0, keepdims=True) + jnp.exp2(sink - m)))
        weights.append(jnp.concatenate(es, axis=0).astype(BF16))
        scales.append(jnp.where(out_row < hd, invs[0], invs[1]))
    v2t = [spread_values_t(kv) for kv in range(n_kv)]
    for i in range(n_tiles):
        out_t = jnp.dot(v2t[i // pairs], weights[i], preferred_element_type=F32) * scales[i]
        o_ref[:, i * LANES:(i + 1) * LANES] = out_t.T.astype(o_ref.dtype)

    @pl.when(qb == pl.num_programs(1) - 1)
    def _():
        klast_ref[0] = kc


def swa_prompt(qkv, sinks, *, batch, seq, n_heads, n_kv, hd, blk):
    assert blk == WINDOW and (n_kv * hd) % LANES == 0
    nb = seq // blk
    qw, kw = n_heads * hd, n_kv * hd
    kcol = qw // kw
    pos = jnp.arange(seq)
    reps = LANES // hd
    cur = _rope_tables(pos, hd, reps)
    prev = _rope_tables(pos - blk, hd, reps)

    def cur_row(b, qb):
        return b * nb + qb

    def prev_row(b, qb):
        return b * nb + jnp.maximum(qb - 1, 0)

    tab_cur = pl.BlockSpec((blk, LANES), lambda b, qb: (qb, 0))
    tab_prev = pl.BlockSpec((blk, LANES), lambda b, qb: (qb, 0))
    return pl.pallas_call(
        functools.partial(_swa_prompt_kernel, n_heads=n_heads, n_kv=n_kv, hd=hd, blk=blk),
        grid=(batch, nb),
        in_specs=[
            pl.BlockSpec((blk, qw), lambda b, qb: (cur_row(b, qb), 0)),
            pl.BlockSpec((blk, kw), lambda b, qb: (cur_row(b, qb), kcol)),
            pl.BlockSpec((blk, kw), lambda b, qb: (prev_row(b, qb), kcol)),
            pl.BlockSpec((blk, kw), lambda b, qb: (cur_row(b, qb), kcol + 1)),
            pl.BlockSpec((blk, kw), lambda b, qb: (prev_row(b, qb), kcol + 1)),
            tab_cur, tab_cur, tab_cur, tab_prev, tab_prev, tab_prev,
            pl.BlockSpec((1, n_heads), lambda b, qb: (0, 0)),
        ],
        out_specs=[
            pl.BlockSpec((blk, qw), lambda b, qb: (cur_row(b, qb), 0)),
            pl.BlockSpec((1, blk, kw), lambda b, qb: (b, 0, 0)),
        ],
        out_shape=[
            jax.ShapeDtypeStruct((qkv.shape[0], qw), BF16),
            jax.ShapeDtypeStruct((batch, blk, kw), F32),
        ],
        compiler_params=_params("parallel", "arbitrary"),
        name="swa_prompt",
    )(qkv, qkv, qkv, qkv, qkv, *cur, *prev, sinks.reshape(1, n_heads))


def _swa_sample_kernel(q_ref, knew_ref, vnew_ref, bk_ref, bv_ref, qc_ref, qu_ref, qd_ref, kc_ref, ku_ref, kd_ref,
                       sink_ref, o_ref, ok_ref, ov_ref, kall_ref, vall_ref, *, n_heads, n_kv, hd, n_t):
    scale = hd ** -0.5
    half = hd // 8
    n_g = n_heads // n_kv
    n_rows = n_t * n_heads
    nbuf = bk_ref.shape[1]
    width = n_kv * hd
    n_all = kall_ref.shape[0]
    knew = _rope(knew_ref[0], kc_ref[...], ku_ref[...], kd_ref[...], half)
    kall_ref[...] = jnp.zeros_like(kall_ref)
    vall_ref[...] = jnp.zeros_like(vall_ref)
    kall_ref[0:nbuf, :] = bk_ref[0]
    vall_ref[0:nbuf, :] = bv_ref[0]
    kall_ref[nbuf:nbuf + n_t, :] = knew
    vall_ref[nbuf:nbuf + n_t, :] = vnew_ref[0]
    ok_ref[0] = kall_ref[n_t:n_t + nbuf, :]
    ov_ref[0] = vall_ref[n_t:n_t + nbuf, :]
    q = q_ref[0]
    up = jnp.concatenate([q[:, half:], q[:, :half]], axis=1)
    dn = jnp.concatenate([q[:, hd - half:], q[:, :hd - half]], axis=1)
    q = q * qc_ref[...] + up * qu_ref[...] + dn * qd_ref[...]
    qt = jnp.concatenate([q] * n_kv, axis=1)
    row_kv = (lax.broadcasted_iota(jnp.int32, (n_rows, width), 0) % n_heads) // n_g
    lane_kv = lax.broadcasted_iota(jnp.int32, (n_rows, width), 1) // hd
    qbd = jnp.where(row_kv == lane_kv, qt, 0.0).astype(BF16)
    s = lax.dot_general(qbd, kall_ref[...].astype(BF16), NT_DIMS, preferred_element_type=F32) * scale
    tok = lax.broadcasted_iota(jnp.int32, (n_rows, n_all), 0) // n_heads
    key = lax.broadcasted_iota(jnp.int32, (n_rows, n_all), 1)
    band = ((key < nbuf) & (tok + nbuf - key < WINDOW)) | ((key >= nbuf) & (key - nbuf <= tok))
    s = jnp.where(band, s, -jnp.inf)
    sink = sink_ref[...]
    m = jnp.maximum(jnp.max(s, axis=1, keepdims=True), sink)
    e = jnp.exp(s - m)
    p = e / (jnp.sum(e, axis=1, keepdims=True) + jnp.exp(sink - m))
    out = jnp.dot(p.astype(BF16), vall_ref[...].astype(BF16), preferred_element_type=F32)
    row_kv = (lax.broadcasted_iota(jnp.int32, (n_rows, hd), 0) % n_heads) // n_g
    res = jnp.zeros((n_rows, hd), F32)
    for kv in range(n_kv):
        res = jnp.where(row_kv == kv, out[:, kv * hd:(kv + 1) * hd], res)
    o_ref[0] = res


def swa_sample(q_rows, qkv_s, buf_k, buf_v, sinks, past, *, n_heads, n_kv, hd):
    bsz, n_t = qkv_s.shape[0], qkv_s.shape[1]
    nbuf = buf_k.shape[1]
    assert nbuf == WINDOW and n_t <= 8
    width = n_kv * hd
    n_rows = n_t * n_heads
    n_all = nbuf + LANES
    pos = past + jnp.arange(n_t)
    qtab = tuple(jnp.repeat(t, n_heads, axis=0) for t in _rope_tables(pos, hd, 1))
    ktab = _rope_tables(pos, hd, LANES // hd)
    sink_rows = jnp.tile(sinks.astype(F32), n_t).reshape(n_rows, 1)
    kcol = n_heads * hd // width
    full2 = lambda shape: pl.BlockSpec(shape, lambda b: (0, 0))
    return pl.pallas_call(
        functools.partial(_swa_sample_kernel, n_heads=n_heads, n_kv=n_kv, hd=hd, n_t=n_t),
        grid=(bsz,),
        in_specs=[
            pl.BlockSpec((1, n_rows, hd), lambda b: (b, 0, 0)),
            pl.BlockSpec((1, n_t, width), lambda b: (b, 0, kcol)),
            pl.BlockSpec((1, n_t, width), lambda b: (b, 0, kcol + 1)),
            pl.BlockSpec((1, nbuf, width), lambda b: (b, 0, 0)),
            pl.BlockSpec((1, nbuf, width), lambda b: (b, 0, 0)),
            full2((n_rows, hd)), full2((n_rows, hd)), full2((n_rows, hd)),
            full2((n_t, LANES)), full2((n_t, LANES)), full2((n_t, LANES)),
            full2((n_rows, 1)),
        ],
        out_specs=[
            pl.BlockSpec((1, n_rows, hd), lambda b: (b, 0, 0)),
            pl.BlockSpec((1, nbuf, width), lambda b: (b, 0, 0)),
            pl.BlockSpec((1, nbuf, width), lambda b: (b, 0, 0)),
        ],
        out_shape=[
            jax.ShapeDtypeStruct((bsz, n_rows, hd), F32),
            jax.ShapeDtypeStruct((bsz, nbuf, width), F32),
            jax.ShapeDtypeStruct((bsz, nbuf, width), F32),
        ],
        scratch_shapes=[pltpu.VMEM((n_all, width), F32), pltpu.VMEM((n_all, width), F32)],
        compiler_params=_params("parallel"),
        name="swa_sample",
    )(q_rows, qkv_s, qkv_s, buf_k, buf_v, *qtab, *ktab, sink_rows)


def _rglru_kernel(gate_ref, u_ref, cb_ref, h0_ref, cw_ref, cbias_ref, wa_ref, ba_ref, wx_ref, bx_ref, lam_ref,
                  y_ref, hlast_ref, cout_ref, uext_ref, a_ref, d_ref, h_ref, *, n_t, n_seq, chunk):
    tc = pl.program_id(1)
    rows = n_t * n_seq
    tail = (cw_ref.shape[0] - 1) * n_seq
    head = uext_ref.shape[0] - rows
    n_blocks, cb = wa_ref.shape[0], wa_ref.shape[1]

    @pl.when(tc == 0)
    def _():
        uext_ref[head - tail:head, :] = cb_ref[0]
        h_ref[...] = h0_ref[0]

    @pl.when(tc > 0)
    def _():
        uext_ref[head - tail:head, :] = uext_ref[head + rows - tail:head + rows, :]

    uext_ref[head:head + rows, :] = u_ref[...]
    log_sig_lam = jax.nn.log_sigmoid(lam_ref[...])
    for c0 in range(0, rows, chunk):
        n = min(chunk, rows - c0)
        xc = cbias_ref[...]
        for i in range(cw_ref.shape[0]):
            xc = xc + uext_ref[head - tail + i * n_seq + c0:head - tail + i * n_seq + c0 + n, :] * cw_ref[i:i + 1, :]
        xb = xc.astype(BF16)
        r = jnp.concatenate([jnp.dot(xb[:, j * cb:(j + 1) * cb], wa_ref[j], preferred_element_type=F32)
                             for j in range(n_blocks)], axis=1)
        ig = jnp.concatenate([jnp.dot(xb[:, j * cb:(j + 1) * cb], wx_ref[j], preferred_element_type=F32)
                              for j in range(n_blocks)], axis=1)
        r = jax.nn.sigmoid(r + ba_ref[...])
        ig = jax.nn.sigmoid(ig + bx_ref[...])
        log_a = LRU_C * r * log_sig_lam
        a_ref[c0:c0 + n, :] = jnp.exp(log_a)
        d_ref[c0:c0 + n, :] = jnp.sqrt(-_expm1(2.0 * log_a)) * (ig * xc)

    def step(t, h):
        sl = pl.ds(pl.multiple_of(t * n_seq, n_seq), n_seq)
        h = a_ref[sl, :] * h + d_ref[sl, :]
        d_ref[sl, :] = h
        return h

    h_last = lax.fori_loop(0, n_t, step, h_ref[...], unroll=min(8, n_t))
    h_ref[...] = h_last
    for c0 in range(0, rows, chunk):
        n = min(chunk, rows - c0)
        y_ref[c0:c0 + n, :] = (d_ref[c0:c0 + n, :] * gate_ref[c0:c0 + n, :]).astype(y_ref.dtype)

    @pl.when(tc == pl.num_programs(1) - 1)
    def _():
        hlast_ref[0] = h_last
        cout_ref[0] = uext_ref[head + rows - tail:head + rows, :]


def rglru(gu, conv_buf, h0, conv_w, conv_b, w_a, b_a, w_x, b_x, lam, *, n_groups, n_t, n_seq, t_chunk, d_rnn):
    assert n_t % t_chunk == 0
    n_tc = n_t // t_chunk
    rows = t_chunk * n_seq
    tail = conv_buf.shape[1]
    assert tail <= rows
    head = -(-tail // 8) * 8
    chunk = min(rows, 256)
    vec = lambda a: a.reshape(1, d_rnn)
    const2 = lambda g, t: (0, 0)
    const3 = lambda g, t: (0, 0, 0)
    per_group = lambda g, t: (g, 0, 0)
    return pl.pallas_call(
        functools.partial(_rglru_kernel, n_t=t_chunk, n_seq=n_seq, chunk=chunk),
        grid=(n_groups, n_tc),
        in_specs=[
            pl.BlockSpec((rows, d_rnn), lambda g, t: (g * n_tc + t, 0)),
            pl.BlockSpec((rows, d_rnn), lambda g, t: (g * n_tc + t, 1)),
            pl.BlockSpec((1, tail, d_rnn), per_group),
            pl.BlockSpec((1, n_seq, d_rnn), per_group),
            pl.BlockSpec(conv_w.shape, const2),
            pl.BlockSpec((1, d_rnn), const2),
            pl.BlockSpec(w_a.shape, const3),
            pl.BlockSpec((1, d_rnn), const2),
            pl.BlockSpec(w_x.shape, const3),
            pl.BlockSpec((1, d_rnn), const2),
            pl.BlockSpec((1, d_rnn), const2),
        ],
        out_specs=[
            pl.BlockSpec((rows, d_rnn), lambda g, t: (g * n_tc + t, 0)),
            pl.BlockSpec((1, n_seq, d_rnn), per_group),
            pl.BlockSpec((1, tail, d_rnn), per_group),
        ],
        out_shape=[
            jax.ShapeDtypeStruct((gu.shape[0], d_rnn), BF16),
            jax.ShapeDtypeStruct((n_groups, n_seq, d_rnn), F32),
            jax.ShapeDtypeStruct((n_groups, tail, d_rnn), F32),
        ],
        scratch_shapes=[
            pltpu.VMEM((head + rows, d_rnn), F32),
            pltpu.VMEM((rows, d_rnn), F32),
            pltpu.VMEM((rows, d_rnn), F32),
            pltpu.VMEM((n_seq, d_rnn), F32),
        ],
        compiler_params=_params("parallel", "arbitrary"),
        name="rglru",
    )(gu, gu, conv_buf, h0, conv_w, vec(conv_b), w_a, vec(b_a), w_x, vec(b_x), vec(lam))


TM = 640
TN_MAX = 1280
TF = 1024


def _proj_tile(n):
    return max(t for t in range(LANES, TN_MAX + 1, LANES) if n % t == 0)
FOX_TQ = 512
FOX_BLK = 256
FOX_PAGES = 16
LRU_T_CHUNK = 256


def kernel(x_prompt, x_sample, cache_fox_k, cache_fox_v, cache_fox_logf, cache_swa_k, cache_swa_v, state_lru_h, state_lru_conv, page_table, norm_mix_pre, norm_mix_post, norm_mlp_pre, norm_mlp_post, mlp_w_up, mlp_w_down, fox_w_qkv, fox_w_f, fox_b_f, fox_w_o, swa_w_qkv, swa_b_qkv, swa_sinks, swa_w_o, swa_b_o, lru_w_gate, lru_b_gate, lru_w_in, lru_b_in, lru_conv_w, lru_conv_b, lru_w_a, lru_b_a, lru_w_x, lru_b_x, lru_lambda, lru_w_out, lru_b_out):
    bp, seq, d = x_prompt.shape
    bs, n_t, _ = x_sample.shape
    mp, ms = bp * seq, bs * n_t
    depth = norm_mix_pre.shape[0]
    fox_heads = fox_w_f.shape[2]
    fox_kv, fox_hd = cache_fox_k.shape[3], cache_fox_k.shape[4]
    swa_heads = swa_sinks.shape[1]
    swa_kv, swa_hd = cache_swa_k.shape[3], cache_swa_k.shape[4]
    d_rnn = lru_w_gate.shape[2]
    conv_w = lru_conv_w.shape[1]
    past = page_table.shape[1] * cache_fox_k.shape[2]
    fox_qw, fox_kw = fox_heads * fox_hd, fox_kv * fox_hd
    swa_qw, swa_kw = swa_heads * swa_hd, swa_kv * swa_hd

    x = jnp.concatenate([x_prompt.reshape(mp, d), x_sample.reshape(ms, d)], axis=0)
    n_layers_fox, pool, page = cache_fox_k.shape[:3]
    ck = cache_fox_k.reshape(n_layers_fox, pool, page * fox_kv, fox_hd)
    cv = cache_fox_v.reshape(n_layers_fox, pool, page * fox_kv, fox_hd)
    clf_t = jnp.swapaxes(cache_fox_logf, 2, 3)

    w_up_all, w_down_all = mlp_w_up.astype(BF16), mlp_w_down.astype(BF16)

    fkp, fvp, flp, fks, fvs, fls = [], [], [], [], [], []
    skp, svp, sks, svs = [], [], [], []
    lhp, lcp, lhs, lcs = [], [], [], []
    for i in range(depth):
        kind, j = i % 3, i // 3
        if kind == 0:
            qkv, logf, k_rows, v_rows = norm_matmul(x, norm_mix_pre[i], fox_w_qkv[j].astype(BF16),
                                                    gate=(fox_w_f[j].astype(BF16), fox_b_f[j]),
                                                    kv_heads=(fox_kv, fox_hd), tm=TM,
                                                    tn=_proj_tile(fox_w_qkv.shape[2]))
            fkp.append(k_rows[:mp * fox_kv].reshape(bp, seq, fox_kv, fox_hd))
            fvp.append(v_rows[:mp * fox_kv].reshape(bp, seq, fox_kv, fox_hd))
            flp.append(logf[:mp].reshape(bp, seq, fox_heads))
            qkv_s = qkv[mp:].reshape(bs, n_t, -1)
            lf_s = logf[mp:].reshape(bs, n_t, fox_heads)
            fks.append(k_rows[mp * fox_kv:].reshape(bs, n_t, fox_kv, fox_hd))
            fvs.append(v_rows[mp * fox_kv:].reshape(bs, n_t, fox_kv, fox_hd))
            fls.append(lf_s)
            qx, kx = fox_cumsum(logf[:mp].reshape(bp, seq, fox_heads), n_kv=fox_kv, chunk=FOX_BLK)
            att_p = fox_flash(qkv, qx, kx, batch=bp, seq=seq, n_heads=fox_heads, n_kv=fox_kv, hd=fox_hd,
                              tq=FOX_TQ, blk=FOX_BLK)
            fox_g = fox_heads // fox_kv
            q_rows = qkv_s[:, :, :fox_qw].reshape(bs, n_t, fox_kv, fox_g, fox_hd).transpose(0, 2, 1, 3, 4)
            q_rows = q_rows.reshape(bs, fox_kv, n_t * fox_g, fox_hd)
            lf_new_t = jnp.pad(jnp.swapaxes(lf_s, 1, 2), ((0, 0), (0, 0), (0, page - n_t)))
            att_s = fox_decode(q_rows, qkv_s, lf_new_t, ck, cv, clf_t, page_table, j,
                               n_heads=fox_heads, n_kv=fox_kv, hd=fox_hd, pages=FOX_PAGES)
            att_s = att_s.reshape(bs, fox_kv, n_t, fox_g, fox_hd).transpose(0, 2, 1, 3, 4).reshape(ms, fox_qw)
            att = lax.dynamic_update_slice(att_p, att_s.astype(BF16), (mp, 0))
            x = matmul_norm_residual(att, fox_w_o[j].astype(BF16), None, norm_mix_post[i], x, tm=TM)
        elif kind == 1:
            qkv = norm_matmul(x, norm_mix_pre[i], swa_w_qkv[j].astype(BF16), swa_b_qkv[j], tm=TM,
                              tn=_proj_tile(swa_w_qkv.shape[2]))
            att_p, k_last = swa_prompt(qkv, swa_sinks[j], batch=bp, seq=seq, n_heads=swa_heads, n_kv=swa_kv,
                                       hd=swa_hd, blk=WINDOW)
            keep = min(WINDOW, seq)
            skp.append(k_last.reshape(bp, keep, swa_kv, swa_hd))
            v_last = jnp.stack([qkv[(b + 1) * seq - keep:(b + 1) * seq, swa_qw + swa_kw:] for b in range(bp)])
            svp.append(v_last.reshape(bp, keep, swa_kv, swa_hd))
            qkv_s = qkv[mp:].reshape(bs, n_t, -1)
            q_rows = qkv_s[:, :, :swa_qw].reshape(bs, n_t * swa_heads, swa_hd)
            nbuf = cache_swa_k.shape[2]
            att_s, nk, nv = swa_sample(q_rows, qkv_s, cache_swa_k[j].reshape(bs, nbuf, swa_kw),
                                       cache_swa_v[j].reshape(bs, nbuf, swa_kw), swa_sinks[j], past,
                                       n_heads=swa_heads, n_kv=swa_kv, hd=swa_hd)
            sks.append(nk.reshape(bs, nbuf, swa_kv, swa_hd))
            svs.append(nv.reshape(bs, nbuf, swa_kv, swa_hd))
            att = lax.dynamic_update_slice(att_p, att_s.reshape(ms, swa_qw).astype(BF16), (mp, 0))
            x = matmul_norm_residual(att, swa_w_o[j].astype(BF16), swa_b_o[j], norm_mix_post[i], x, tm=TM)
        else:
            w_gu = jnp.concatenate([lru_w_gate[j], lru_w_in[j]], axis=1).astype(BF16)
            b_gu = jnp.concatenate([lru_b_gate[j], lru_b_in[j]])
            gu = norm_matmul(x, norm_mix_pre[i], w_gu, b_gu, n_gelu_cols=d_rnn, tm=TM, tn=_proj_tile(d_rnn))
            lw = (lru_conv_w[j], lru_conv_b[j], lru_w_a[j].astype(BF16), lru_b_a[j], lru_w_x[j].astype(BF16),
                  lru_b_x[j], lru_lambda[j])
            y_p, h_p, c_p = rglru(gu, jnp.zeros((bp, conv_w - 1, d_rnn), F32), jnp.zeros((bp, 1, d_rnn), F32),
                                  *lw, n_groups=bp, n_t=seq, n_seq=1, t_chunk=LRU_T_CHUNK, d_rnn=d_rnn)
            lhp.append(h_p.reshape(bp, d_rnn))
            lcp.append(c_p)
            gu_s = gu[mp:].reshape(bs, n_t, -1).swapaxes(0, 1).reshape(ms, -1)
            cb_s = state_lru_conv[j].swapaxes(0, 1).reshape(1, (conv_w - 1) * bs, d_rnn)
            y_s, h_s, c_s = rglru(gu_s, cb_s, state_lru_h[j].reshape(1, bs, d_rnn), *lw,
                                  n_groups=1, n_t=n_t, n_seq=bs, t_chunk=n_t, d_rnn=d_rnn)
            lhs.append(h_s.reshape(bs, d_rnn))
            lcs.append(c_s.reshape(conv_w - 1, bs, d_rnn).swapaxes(0, 1))
            y_s = y_s.reshape(n_t, bs, d_rnn).swapaxes(0, 1).reshape(ms, d_rnn)
            y = lax.dynamic_update_slice(y_p, y_s, (mp, 0))
            x = matmul_norm_residual(y, lru_w_out[j].astype(BF16), lru_b_out[j], norm_mix_post[i], x, tm=TM)
        x = mlp_sublayer(x, norm_mlp_pre[i], w_up_all, w_down_all, norm_mlp_post[i], i, tm=TM, tf=TF)
    return (x[:mp].reshape(bp, seq, d), x[mp:].reshape(bs, n_t, d),
            jnp.stack(fkp), jnp.stack(fvp), jnp.stack(flp),
            jnp.stack(fks), jnp.stack(fvs), jnp.stack(fls),
            jnp.stack(skp), jnp.stack(svp), jnp.stack(sks), jnp.stack(svs),
            jnp.stack(lhp), jnp.stack(lcp), jnp.stack(lhs), jnp.stack(lcs))
```

```python
import functools

import jax
import jax.numpy as jnp
from jax import lax
from jax.experimental import pallas as pl
from jax.experimental.pallas import tpu as pltpu

F32 = jnp.float32
BF16 = jnp.bfloat16

RMS_EPS = 1e-6
WINDOW = 128
ROPE_THETA = 500000.0
LRU_C = 8.0
LOG2E = 1.4426950408889634

V7X_VMEM_LIMIT_BYTES = 56 * 1024 * 1024
LANES = 128

NT_DIMS = (((1,), (1,)), ((), ()))


def _params(*sem):
    return pltpu.CompilerParams(dimension_semantics=sem, vmem_limit_bytes=V7X_VMEM_LIMIT_BYTES)


def _rms(x, g):
    ms = jnp.mean(x * x, axis=-1, keepdims=True)
    return x * lax.rsqrt(ms + RMS_EPS) * g


def _expm1(x):
    u = jnp.exp(x)
    um1 = u - 1.0
    safe = jnp.where(u == 1.0, 1.0, jnp.log(u))
    return jnp.where(u == 1.0, x, jnp.where(um1 == -1.0, -1.0, um1 * x / safe))


def _split3(x):
    hi = x.astype(BF16)
    r1 = x - hi.astype(F32)
    mid = r1.astype(BF16)
    lo = (r1 - mid.astype(F32)).astype(BF16)
    return hi, mid, lo


def _norm_mm_kernel(*refs, n_gelu_tiles, has_bias, has_gate, kv_rows):
    it = iter(refs)
    x_ref, g_ref, w_ref = next(it), next(it), next(it)
    b_ref = next(it) if has_bias else None
    wf_ref, bf_ref = (next(it), next(it)) if has_gate else (None, None)
    o_ref = next(it)
    lf_ref = next(it) if has_gate else None
    krows_ref, vrows_ref = (next(it), next(it)) if kv_rows else (None, None)
    xn_ref = next(it)
    j = pl.program_id(1)

    @pl.when(j == 0)
    def _():
        xn = _rms(x_ref[...], g_ref[...]).astype(BF16)
        xn_ref[...] = xn
        if has_gate:
            z = jnp.dot(xn, wf_ref[...], preferred_element_type=F32) + bf_ref[...]
            lf_ref[...] = jax.nn.log_sigmoid(z)

    acc = jnp.dot(xn_ref[...], w_ref[...], preferred_element_type=F32)
    if has_bias:
        acc = acc + b_ref[...]
    if kv_rows:
        n_kv, hd, off = kv_rows
        tm = o_ref.shape[0]

        @pl.when(j == pl.num_programs(1) - 1)
        def _():
            for kv in range(n_kv):
                krows_ref[pl.ds(kv, tm, stride=n_kv), :] = acc[:, off + kv * hd:off + (kv + 1) * hd]
                vrows_ref[pl.ds(kv, tm, stride=n_kv), :] = acc[:, off + (n_kv + kv) * hd:off + (n_kv + kv + 1) * hd]

    if n_gelu_tiles == 0:
        o_ref[...] = acc.astype(o_ref.dtype)
    else:
        @pl.when(j < n_gelu_tiles)
        def _():
            o_ref[...] = jax.nn.gelu(acc).astype(o_ref.dtype)

        @pl.when(j >= n_gelu_tiles)
        def _():
            o_ref[...] = acc.astype(o_ref.dtype)


def norm_matmul(x, g, w, b=None, *, gate=None, kv_heads=None, n_gelu_cols=0, tm, tn):
    m, d = x.shape
    n = w.shape[1]
    assert m % tm == 0 and n % tn == 0 and n_gelu_cols % tn == 0
    has_bias, has_gate = b is not None, gate is not None
    kv_rows = None
    if kv_heads is not None:
        n_kv, hd = kv_heads
        off = tn - 2 * n_kv * hd
        assert off >= 0 and hd % LANES == 0
        kv_rows = (n_kv, hd, off)
    args = [x, g.reshape(1, d), w]
    in_specs = [
        pl.BlockSpec((tm, d), lambda i, j: (i, 0)),
        pl.BlockSpec((1, d), lambda i, j: (0, 0)),
        pl.BlockSpec((d, tn), lambda i, j: (0, j)),
    ]
    if has_bias:
        args.append(b.reshape(1, n))
        in_specs.append(pl.BlockSpec((1, tn), lambda i, j: (0, j)))
    out_shape = [jax.ShapeDtypeStruct((m, n), F32)]
    out_specs = [pl.BlockSpec((tm, tn), lambda i, j: (i, j))]
    if has_gate:
        w_f, b_f = gate
        h = w_f.shape[1]
        args += [w_f, b_f.reshape(1, h)]
        in_specs += [pl.BlockSpec((d, h), lambda i, j: (0, 0)), pl.BlockSpec((1, h), lambda i, j: (0, 0))]
        out_shape.append(jax.ShapeDtypeStruct((m, h), F32))
        out_specs.append(pl.BlockSpec((tm, h), lambda i, j: (i, 0)))
    if kv_rows:
        out_shape += [jax.ShapeDtypeStruct((m * n_kv, hd), F32)] * 2
        out_specs += [pl.BlockSpec((tm * n_kv, hd), lambda i, j: (i, 0))] * 2
    outs = pl.pallas_call(
        functools.partial(_norm_mm_kernel, n_gelu_tiles=n_gelu_cols // tn, has_bias=has_bias, has_gate=has_gate,
                          kv_rows=kv_rows),
        grid=(m // tm, n // tn),
        in_specs=in_specs,
        out_specs=out_specs,
        out_shape=out_shape,
        scratch_shapes=[pltpu.VMEM((tm, d), BF16)],
        compiler_params=_params("parallel", "arbitrary"),
        name="norm_matmul",
    )(*args)
    return outs if len(outs) > 1 else outs[0]


def _mm_norm_res_kernel(*refs, has_bias):
    it = iter(refs)
    a_ref, w_ref = next(it), next(it)
    b_ref = next(it) if has_bias else None
    g_ref, x_ref, o_ref = next(it), next(it), next(it)
    mix = jnp.dot(a_ref[...], w_ref[...], preferred_element_type=F32)
    if has_bias:
        mix = mix + b_ref[...]
    o_ref[...] = x_ref[...] + _rms(mix, g_ref[...])


def matmul_norm_residual(a, w, b, g, x, *, tm):
    m, k = a.shape
    d = w.shape[1]
    assert m % tm == 0
    has_bias = b is not None
    args = [a, w]
    in_specs = [pl.BlockSpec((tm, k), lambda i: (i, 0)), pl.BlockSpec((k, d), lambda i: (0, 0))]
    if has_bias:
        args.append(b.reshape(1, d))
        in_specs.append(pl.BlockSpec((1, d), lambda i: (0, 0)))
    args += [g.reshape(1, d), x]
    in_specs += [pl.BlockSpec((1, d), lambda i: (0, 0)), pl.BlockSpec((tm, d), lambda i: (i, 0))]
    return pl.pallas_call(
        functools.partial(_mm_norm_res_kernel, has_bias=has_bias),
        grid=(m // tm,),
        in_specs=in_specs,
        out_specs=pl.BlockSpec((tm, d), lambda i: (i, 0)),
        out_shape=jax.ShapeDtypeStruct((m, d), F32),
        compiler_params=_params("parallel"),
        name="matmul_norm_residual",
    )(*args)


def _mlp_kernel(x_ref, gpre_ref, wup_ref, wdn_ref, gpost_ref, o_ref, xn_ref, acc_ref):
    f = pl.program_id(1)

    @pl.when(f == 0)
    def _():
        xn_ref[...] = _rms(x_ref[...], gpre_ref[...]).astype(BF16)
        acc_ref[...] = jnp.zeros_like(acc_ref)

    h = jnp.dot(xn_ref[...], wup_ref[0], preferred_element_type=F32)
    h = jnp.square(jnp.maximum(h, 0.0)).astype(BF16)
    acc_ref[...] += jnp.dot(h, wdn_ref[0], preferred_element_type=F32)

    @pl.when(f == pl.num_programs(1) - 1)
    def _():
        o_ref[...] = x_ref[...] + _rms(acc_ref[...], gpost_ref[...])


def mlp_sublayer(x, g_pre, w_up, w_down, g_post, layer, *, tm, tf):
    m, d = x.shape
    dff = w_up.shape[2]
    assert m % tm == 0 and dff % tf == 0
    return pl.pallas_call(
        _mlp_kernel,
        grid=(m // tm, dff // tf),
        in_specs=[
            pl.BlockSpec((tm, d), lambda i, f: (i, 0)),
            pl.BlockSpec((1, d), lambda i, f: (0, 0)),
            pl.BlockSpec((1, d, tf), lambda i, f: (layer, 0, f)),
            pl.BlockSpec((1, tf, d), lambda i, f: (layer, f, 0)),
            pl.BlockSpec((1, d), lambda i, f: (0, 0)),
        ],
        out_specs=pl.BlockSpec((tm, d), lambda i, f: (i, 0)),
        out_shape=jax.ShapeDtypeStruct((m, d), F32),
        scratch_shapes=[pltpu.VMEM((tm, d), BF16), pltpu.VMEM((tm, d), F32)],
        compiler_params=_params("parallel", "arbitrary"),
        name="mlp_sublayer",
    )(x, g_pre.reshape(1, d), w_up, w_down, g_post.reshape(1, d))


N_PARTS = 3


def _fox_cumsum_kernel(lf_ref, qx_ref, kx_ref, *, chunk, n_kv, n_g):
    s, h = lf_ref.shape[1], lf_ref.shape[2]
    row = lax.broadcasted_iota(jnp.int32, (chunk, chunk), 0)
    col = lax.broadcasted_iota(jnp.int32, (chunk, chunk), 1)
    lower = (row >= col).astype(BF16)

    def placement(n_tiles, lane_of, head_of):
        r = lax.broadcasted_iota(jnp.int32, (h, n_tiles * LANES), 0)
        c = lax.broadcasted_iota(jnp.int32, (h, n_tiles * LANES), 1)
        tile, lane = c // LANES, c % LANES
        mats = []
        for p in range(N_PARTS):
            hit = jnp.zeros((h, n_tiles * LANES), jnp.bool_)
            for g in range(n_g):
                hit = hit | ((r == head_of(tile, g)) & (lane == lane_of(g, p)))
            mats.append(hit)
        return mats, tile, lane

    q_hit, q_tile, q_lane = placement(h, lambda g, p: p, lambda tile, g: jnp.where(tile % n_g == g, tile, -1))
    q_mats = [m.astype(BF16) for m in q_hit]
    q_gl = N_PARTS * (1 + q_tile[0:1] % n_g)
    q_ones = ((q_lane[0:1] >= q_gl) & (q_lane[0:1] < q_gl + N_PARTS)).astype(F32)
    k_hit, _, k_lane = placement(n_kv, lambda g, p: N_PARTS * (1 + g) + p, lambda tile, g: tile * n_g + g)
    k_mats = [(-m.astype(F32)).astype(BF16) for m in k_hit]
    k_ones = (k_lane[0:1] < N_PARTS).astype(F32)

    carry = jnp.zeros((1, h), F32)
    for c in range(s // chunk):
        rows = slice(c * chunk, (c + 1) * chunk)
        hi, mid, lo = _split3(lf_ref[0, rows, :])
        cs = (jnp.dot(lower, hi, preferred_element_type=F32)
              + jnp.dot(lower, mid, preferred_element_type=F32)
              + jnp.dot(lower, lo, preferred_element_type=F32)) + carry
        carry = cs[chunk - 1:chunk, :]
        parts = _split3(cs * LOG2E)
        qx = q_ones + sum(jnp.dot(parts[p], q_mats[p], preferred_element_type=F32) for p in range(N_PARTS))
        kx = k_ones + sum(jnp.dot(parts[p], k_mats[p], preferred_element_type=F32) for p in range(N_PARTS))
        for hh in range(h):
            qx_ref[0, hh, rows, :] = qx[:, hh * LANES:(hh + 1) * LANES].astype(BF16)
        for kv in range(n_kv):
            kx_ref[0, kv, rows, :] = kx[:, kv * LANES:(kv + 1) * LANES].astype(BF16)


def fox_cumsum(logf, *, n_kv, chunk):
    b, s, h = logf.shape
    n_g = h // n_kv
    assert N_PARTS * (1 + n_g) <= LANES
    return pl.pallas_call(
        functools.partial(_fox_cumsum_kernel, chunk=chunk, n_kv=n_kv, n_g=n_g),
        grid=(b,),
        in_specs=[pl.BlockSpec((1, s, h), lambda i: (i, 0, 0))],
        out_specs=[
            pl.BlockSpec((1, h, s, LANES), lambda i: (i, 0, 0, 0)),
            pl.BlockSpec((1, n_kv, s, LANES), lambda i: (i, 0, 0, 0)),
        ],
        out_shape=[
            jax.ShapeDtypeStruct((b, h, s, LANES), BF16),
            jax.ShapeDtypeStruct((b, n_kv, s, LANES), BF16),
        ],
        compiler_params=_params("parallel"),
        name="fox_cumsum",
    )(logf)


def _lane_tiles(x):
    return [x[:, j * LANES:(j + 1) * LANES] for j in range(x.shape[1] // LANES)]


def _fox_flash_kernel(q_ref, k_ref, v_ref, qx_ref, kx_ref, o_ref, ka_ref, vb_ref, s_ref, m_ref, l_ref, acc_ref,
                      *, tq, blk, n_g, hd):
    qi = pl.program_id(2)
    per_q = tq // blk

    @pl.when(qi == 0)
    def _():
        ka_ref[:, :hd] = k_ref[...].astype(BF16)
        ka_ref[:, hd:] = kx_ref[0, 0]
        vb_ref[...] = v_ref[...].astype(BF16)

    q_scale = hd ** -0.5 * LOG2E
    qa = [jnp.concatenate([(q_ref[:, g * hd:(g + 1) * hd] * q_scale).astype(BF16), qx_ref[0, g]], axis=1)
          for g in range(n_g)]
    m_ref[...] = jnp.full_like(m_ref, -jnp.inf)

    def logits_pass(kc, diag):
        ka = ka_ref[pl.ds(pl.multiple_of(kc * blk, blk), blk), :]
        if diag is not None:
            row = lax.broadcasted_iota(jnp.int32, (tq, blk), 0)
            col = lax.broadcasted_iota(jnp.int32, (tq, blk), 1) + diag * blk
            causal = row >= col
        for g in range(n_g):
            s = lax.dot_general(qa[g], ka, NT_DIMS, preferred_element_type=F32)
            if diag is not None:
                s = jnp.where(causal, s, -jnp.inf)
            s_ref[g, kc] = s
            m = m_ref[g]
            for t in _lane_tiles(s):
                m = jnp.maximum(m, t)
            m_ref[g] = m

    def body1(kc, carry):
        logits_pass(kc, None)
        return carry

    lax.fori_loop(0, per_q * qi, body1, 0)
    for e in range(per_q):
        logits_pass(per_q * qi + e, e)
    for g in range(n_g):
        m_ref[g] = jnp.broadcast_to(jnp.max(m_ref[g], axis=1, keepdims=True), (tq, LANES))
    l_ref[...] = jnp.zeros_like(l_ref)
    acc_ref[...] = jnp.zeros_like(acc_ref)

    def body2(kc, carry):
        vb = vb_ref[pl.ds(pl.multiple_of(kc * blk, blk), blk), :]
        for g in range(n_g):
            m = m_ref[g]
            p = [jnp.exp2(t - m) for t in _lane_tiles(s_ref[g, kc])]
            l_ref[g] += sum(p)
            acc_ref[g] += jnp.dot(jnp.concatenate(p, axis=1).astype(BF16), vb, preferred_element_type=F32)
        return carry

    lax.fori_loop(0, per_q * (qi + 1), body2, 0)
    for g in range(n_g):
        inv = 1.0 / jnp.sum(l_ref[g], axis=1, keepdims=True)
        o_ref[:, g * hd:(g + 1) * hd] = (acc_ref[g] * inv).astype(o_ref.dtype)


def fox_flash(qkv, qx, kx, *, batch, seq, n_heads, n_kv, hd, tq, blk):
    assert hd == LANES and tq % blk == 0 and seq % tq == 0
    n_g = n_heads // n_kv
    nq = seq // tq
    gw = n_g * hd
    return pl.pallas_call(
        functools.partial(_fox_flash_kernel, tq=tq, blk=blk, n_g=n_g, hd=hd),
        grid=(batch, n_kv, nq),
        in_specs=[
            pl.BlockSpec((tq, gw), lambda b, kv, qi: (b * nq + qi, kv)),
            pl.BlockSpec((seq, hd), lambda b, kv, qi: (b, n_heads + kv)),
            pl.BlockSpec((seq, hd), lambda b, kv, qi: (b, n_heads + n_kv + kv)),
            pl.BlockSpec((1, n_g, tq, LANES), lambda b, kv, qi: (b, kv, qi, 0)),
            pl.BlockSpec((1, 1, seq, LANES), lambda b, kv, qi: (b, kv, 0, 0)),
        ],
        out_specs=pl.BlockSpec((tq, gw), lambda b, kv, qi: (b * nq + qi, kv)),
        out_shape=jax.ShapeDtypeStruct((qkv.shape[0], n_heads * hd), BF16),
        scratch_shapes=[
            pltpu.VMEM((seq, hd + LANES), BF16),
            pltpu.VMEM((seq, hd), BF16),
            pltpu.VMEM((n_g, seq // blk, tq, blk), F32),
            pltpu.VMEM((n_g, tq, LANES), F32),
            pltpu.VMEM((n_g, tq, LANES), F32),
            pltpu.VMEM((n_g, tq, hd), F32),
        ],
        compiler_params=_params("parallel", "parallel", "arbitrary"),
        name="fox_flash",
    )(qkv, qkv, qkv, qx, kx)


def _fox_decode_kernel(pt_ref, q_ref, knew_ref, vnew_ref, lfnew_ref, *rest, pages, n_kv, n_heads, hd, n_t):
    del pt_ref
    k_refs, v_refs, lf_refs = rest[:pages], rest[pages:2 * pages], rest[2 * pages:3 * pages]
    o_ref, cq_ref, m_ref, l_ref, acc_ref, carry_ref, kpad_ref, vpad_ref = rest[3 * pages:]
    c = pl.program_id(1)
    scale = hd ** -0.5
    page = lf_refs[0].shape[3]
    n_g = n_heads // n_kv
    grp = n_t * n_g
    jj = lax.broadcasted_iota(jnp.int32, (page, page), 0)
    ss = lax.broadcasted_iota(jnp.int32, (page, page), 1)

    def lane_sums(xs, tri):
        stacked = jnp.concatenate([part for x in xs for part in _split3(x)], axis=0)
        out = jnp.dot(stacked, tri, preferred_element_type=F32)
        n = N_PARTS * n_heads
        return [out[i * n:i * n + n_heads] + out[i * n + n_heads:i * n + 2 * n_heads]
                + out[i * n + 2 * n_heads:(i + 1) * n] for i in range(len(xs))]

    def group_rows(x, kv):
        return jnp.concatenate([x[kv * n_g:(kv + 1) * n_g]] * n_t, axis=0)

    q = [q_ref[0, kv].astype(BF16) for kv in range(n_kv)]

    @pl.when(c == 0)
    def _():
        c_new = lane_sums([lfnew_ref[0]], (jj <= ss).astype(BF16))[0]
        kpad_ref[...] = jnp.zeros_like(kpad_ref)
        vpad_ref[...] = jnp.zeros_like(vpad_ref)
        tok = lax.broadcasted_iota(jnp.int32, (grp, page), 0) // n_g
        key = lax.broadcasted_iota(jnp.int32, (grp, page), 1)
        for kv in range(n_kv):
            cq = jnp.concatenate([c_new[kv * n_g:(kv + 1) * n_g, t:t + 1] for t in range(n_t)], axis=0)
            cq_ref[kv] = cq
            kpad_ref[kv, 0:n_t, :] = knew_ref[0, :, kv * hd:(kv + 1) * hd]
            vpad_ref[kv, 0:n_t, :] = vnew_ref[0, :, kv * hd:(kv + 1) * hd]
            s = lax.dot_general(q[kv], kpad_ref[kv].astype(BF16), NT_DIMS, preferred_element_type=F32) * scale
            s = s + (cq - group_rows(c_new, kv))
            s = jnp.where(key <= tok, s, -jnp.inf)
            m = jnp.max(s, axis=1, keepdims=True)
            p = jnp.exp(s - m)
            m_ref[kv] = m
            l_ref[kv] = jnp.sum(p, axis=1, keepdims=True)
            acc_ref[kv] = jnp.dot(p.astype(BF16), vpad_ref[kv].astype(BF16), preferred_element_type=F32)
        carry_ref[...] = jnp.zeros_like(carry_ref)

    lfs = [lf_refs[i][0, 0] for i in range(pages)]
    local = lane_sums(lfs, (jj > ss).astype(BF16))
    carry = carry_ref[...]
    rs = []
    for i in range(pages):
        rs.append(local[i] + carry)
        carry = carry + (local[i][:, 0:1] + lfs[i][:, 0:1])
    carry_ref[...] = carry
    def pair(refs, i, kv):
        return jnp.concatenate([refs[i + e][0, 0, pl.ds(kv, page, stride=n_kv), :].astype(BF16) for e in range(2)],
                               axis=0)

    raw = []
    for kv in range(n_kv):
        both = [lax.dot_general(q[kv], pair(k_refs, i, kv), NT_DIMS, preferred_element_type=F32)
                for i in range(0, pages, 2)]
        raw.append([b[:, e * page:(e + 1) * page] for b in both for e in range(2)])
    probs, alphas = [], []
    for kv in range(n_kv):
        cq = cq_ref[kv]
        logits = [raw[kv][i] * scale + (cq + group_rows(rs[i], kv)) for i in range(pages)]
        m_old = m_ref[kv]
        tile_max = logits[0]
        for s in logits[1:]:
            tile_max = jnp.maximum(tile_max, s)
        m_new = jnp.maximum(m_old, jnp.max(tile_max, axis=1, keepdims=True))
        alpha = jnp.exp(m_old - m_new)
        ps = [jnp.exp(s - m_new) for s in logits]
        l_ref[kv] = alpha * l_ref[kv] + jnp.sum(sum(ps), axis=1, keepdims=True)
        m_ref[kv] = m_new
        probs.append([p.astype(BF16) for p in ps])
        alphas.append(alpha)
    for kv in range(n_kv):
        pv = sum(jnp.dot(jnp.concatenate(probs[kv][i:i + 2], axis=1), pair(v_refs, i, kv),
                         preferred_element_type=F32) for i in range(0, pages, 2))
        acc_ref[kv] = alphas[kv] * acc_ref[kv] + pv

    @pl.when(c == pl.num_programs(1) - 1)
    def _():
        for kv in range(n_kv):
            o_ref[0, kv] = acc_ref[kv] / l_ref[kv]


def fox_decode(q_rows, qkv_s, lf_new_t, cache_k, cache_v, cache_lf_t, page_table, layer, *, n_heads, n_kv, hd, pages):
    bsz, n_t = qkv_s.shape[0], qkv_s.shape[1]
    n_pages = page_table.shape[1]
    page = cache_lf_t.shape[3]
    width = n_kv * hd
    grp = n_t * (n_heads // n_kv)
    assert n_pages % pages == 0 and page == LANES and cache_k.shape[2] == page * n_kv
    n_chunks = n_pages // pages

    def page_map(i):
        return lambda b, c, pt: (layer, pt[b, n_pages - 1 - (c * pages + i)], 0, 0)

    kv_col = n_heads * hd // width
    in_specs = [
        pl.BlockSpec((1, n_kv, grp, hd), lambda b, c, pt: (b, 0, 0, 0)),
        pl.BlockSpec((1, n_t, width), lambda b, c, pt: (b, 0, kv_col)),
        pl.BlockSpec((1, n_t, width), lambda b, c, pt: (b, 0, kv_col + 1)),
        pl.BlockSpec((1, n_heads, page), lambda b, c, pt: (b, 0, 0)),
    ]
    in_specs += [pl.BlockSpec((1, 1, page * n_kv, hd), page_map(i)) for i in range(pages)]
    in_specs += [pl.BlockSpec((1, 1, page * n_kv, hd), page_map(i)) for i in range(pages)]
    in_specs += [pl.BlockSpec((1, 1, n_heads, page), page_map(i)) for i in range(pages)]
    grid_spec = pltpu.PrefetchScalarGridSpec(
        num_scalar_prefetch=1,
        grid=(bsz, n_chunks),
        in_specs=in_specs,
        out_specs=pl.BlockSpec((1, n_kv, grp, hd), lambda b, c, pt: (b, 0, 0, 0)),
        scratch_shapes=[
            pltpu.VMEM((n_kv, grp, 1), F32),
            pltpu.VMEM((n_kv, grp, 1), F32),
            pltpu.VMEM((n_kv, grp, 1), F32),
            pltpu.VMEM((n_kv, grp, hd), F32),
            pltpu.VMEM((n_heads, 1), F32),
            pltpu.VMEM((n_kv, page, hd), F32),
            pltpu.VMEM((n_kv, page, hd), F32),
        ],
    )
    return pl.pallas_call(
        functools.partial(_fox_decode_kernel, pages=pages, n_kv=n_kv, n_heads=n_heads, hd=hd, n_t=n_t),
        grid_spec=grid_spec,
        out_shape=jax.ShapeDtypeStruct((bsz, n_kv, grp, hd), F32),
        compiler_params=_params("parallel", "arbitrary"),
        name="fox_decode",
    )(page_table, q_rows, qkv_s, qkv_s, lf_new_t, *([cache_k] * pages), *([cache_v] * pages), *([cache_lf_t] * pages))


def _rope_tables(pos, hd, reps):
    rot = hd // 4
    half = rot // 2
    inv_freq = ROPE_THETA ** (-jnp.arange(half, dtype=F32) / half)
    ang = pos.astype(F32)[:, None] * inv_freq[None, :]
    cos, sin = jnp.cos(ang), jnp.sin(ang)
    n = pos.shape[0]
    ones = jnp.ones((n, hd - rot), F32)
    zeros_h = jnp.zeros((n, half), F32)
    zeros_r = jnp.zeros((n, hd - rot), F32)
    c = jnp.concatenate([cos, cos, ones], axis=1)
    s_up = jnp.concatenate([-sin, zeros_h, zeros_r], axis=1)
    s_dn = jnp.concatenate([zeros_h, sin, zeros_r], axis=1)
    return tuple(jnp.tile(t, (1, reps)) for t in (c, s_up, s_dn))


def _rope(x, c, s_up, s_dn, half):
    w = x.shape[1]
    pieces = []
    for j in range(w // LANES):
        xj = x[:, j * LANES:(j + 1) * LANES]
        up = pltpu.roll(xj, LANES - half, 1)
        dn = pltpu.roll(xj, half, 1)
        pieces.append(xj * c + up * s_up + dn * s_dn)
    return pieces[0] if len(pieces) == 1 else jnp.concatenate(pieces, axis=1)


def _swa_prompt_kernel(q_ref, kc_ref, kp_ref, vc_ref, vp_ref, cc_ref, su_ref, sd_ref, pc_ref, pu_ref, pd_ref,
                       sink_ref, o_ref, klast_ref, *, n_heads, n_kv, hd, blk):
    qb = pl.program_id(1)
    half = hd // 8
    n_g = n_heads // n_kv
    per_tile = LANES // hd
    assert per_tile == 2 and n_g % per_tile == 0
    nk = 2 * blk
    q = _rope(q_ref[...], cc_ref[...], su_ref[...], sd_ref[...], half) * (hd ** -0.5 * LOG2E)
    kc = _rope(kc_ref[...], cc_ref[...], su_ref[...], sd_ref[...], half)
    kp = _rope(kp_ref[...], pc_ref[...], pu_ref[...], pd_ref[...], half)
    k = jnp.concatenate([kp, kc], axis=0)
    v = jnp.concatenate([vp_ref[...], vc_ref[...]], axis=0)
    lane = lax.broadcasted_iota(jnp.int32, (nk, LANES), 1)
    d_row = lax.broadcasted_iota(jnp.int32, (LANES, nk), 0)
    out_row = lax.broadcasted_iota(jnp.int32, (LANES, blk), 0)

    def spread_keys(kv):
        tile = k[:, (kv // per_tile) * LANES:(kv // per_tile + 1) * LANES]
        swapped = pltpu.roll(tile, hd, 1)
        low, high = (tile, swapped) if kv % per_tile == 0 else (swapped, tile)
        return jnp.concatenate([jnp.where(lane < hd, low, 0.0), jnp.where(lane >= hd, high, 0.0)],
                               axis=0).astype(BF16)

    def spread_values_t(kv):
        tile_t = v[:, (kv // per_tile) * LANES:(kv // per_tile + 1) * LANES].T
        swapped_t = jnp.concatenate([tile_t[hd:], tile_t[:hd]], axis=0)
        low, high = (tile_t, swapped_t) if kv % per_tile == 0 else (swapped_t, tile_t)
        return jnp.concatenate([jnp.where(d_row < hd, low, 0.0), jnp.where(d_row >= hd, high, 0.0)],
                               axis=1).astype(BF16)

    j = lax.broadcasted_iota(jnp.int32, (nk, blk), 0)
    t = lax.broadcasted_iota(jnp.int32, (nk, blk), 1)
    first_key = jnp.where(qb > 0, 0, blk)
    band = (j > t) & (j <= t + blk) & (j >= first_key)
    pairs = n_g // per_tile
    n_tiles = n_kv * pairs
    k2 = [spread_keys(kv) for kv in range(n_kv)]
    raw = [lax.dot_general(k2[i // pairs], q[:, i * LANES:(i + 1) * LANES].astype(BF16), NT_DIMS,
                           preferred_element_type=F32) for i in range(n_tiles)]
    weights, scales = [], []
    for i in range(n_tiles):
        es, invs = [], []
        for e in range(per_tile):
            h = per_tile * i + e
            se = jnp.where(band, raw[i][e * nk:(e + 1) * nk], -jnp.inf)
            sink = sink_ref[0:1, h:h + 1] * LOG2E
            m = jnp.maximum(jnp.max(se, axis=0, keepdims=True), sink)
            ee = jnp.exp2(se - m)
            es.append(ee)
            invs.append(1.0 / (jnp.sum(ee, axis=0, keepdims=True) + jnp.exp2(sink - m)))
        weights.append(jnp.concatenate(es, axis=0).astype(BF16))
        scales.append(jnp.where(out_row < hd, invs[0], invs[1]))
    v2t = [spread_values_t(kv) for kv in range(n_kv)]
    for i in range(n_tiles):
        out_t = jnp.dot(v2t[i // pairs], weights[i], preferred_element_type=F32) * scales[i]
        o_ref[:, i * LANES:(i + 1) * LANES] = out_t.T.astype(o_ref.dtype)

    @pl.when(qb == pl.num_programs(1) - 1)
    def _():
        klast_ref[0] = kc


def swa_prompt(qkv, sinks, *, batch, seq, n_heads, n_kv, hd, blk):
    assert blk == WINDOW and (n_kv * hd) % LANES == 0
    nb = seq // blk
    qw, kw = n_heads * hd, n_kv * hd
    kcol = qw // kw
    pos = jnp.arange(seq)
    reps = LANES // hd
    cur = _rope_tables(pos, hd, reps)
    prev = _rope_tables(pos - blk, hd, reps)

    def cur_row(b, qb):
        return b * nb + qb

    def prev_row(b, qb):
        return b * nb + jnp.maximum(qb - 1, 0)

    tab_cur = pl.BlockSpec((blk, LANES), lambda b, qb: (qb, 0))
    tab_prev = pl.BlockSpec((blk, LANES), lambda b, qb: (qb, 0))
    return pl.pallas_call(
        functools.partial(_swa_prompt_kernel, n_heads=n_heads, n_kv=n_kv, hd=hd, blk=blk),
        grid=(batch, nb),
        in_specs=[
            pl.BlockSpec((blk, qw), lambda b, qb: (cur_row(b, qb), 0)),
            pl.BlockSpec((blk, kw), lambda b, qb: (cur_row(b, qb), kcol)),
            pl.BlockSpec((blk, kw), lambda b, qb: (prev_row(b, qb), kcol)),
            pl.BlockSpec((blk, kw), lambda b, qb: (cur_row(b, qb), kcol + 1)),
            pl.BlockSpec((blk, kw), lambda b, qb: (prev_row(b, qb), kcol + 1)),
            tab_cur, tab_cur, tab_cur, tab_prev, tab_prev, tab_prev,
            pl.BlockSpec((1, n_heads), lambda b, qb: (0, 0)),
        ],
        out_specs=[
            pl.BlockSpec((blk, qw), lambda b, qb: (cur_row(b, qb), 0)),
            pl.BlockSpec((1, blk, kw), lambda b, qb: (b, 0, 0)),
        ],
        out_shape=[
            jax.ShapeDtypeStruct((qkv.shape[0], qw), BF16),
            jax.ShapeDtypeStruct((batch, blk, kw), F32),
        ],
        compiler_params=_params("parallel", "arbitrary"),
        name="swa_prompt",
    )(qkv, qkv, qkv, qkv, qkv, *cur, *prev, sinks.reshape(1, n_heads))


def _swa_sample_kernel(q_ref, knew_ref, vnew_ref, bk_ref, bv_ref, qc_ref, qu_ref, qd_ref, kc_ref, ku_ref, kd_ref,
                       sink_ref, o_ref, ok_ref, ov_ref, kall_ref, vall_ref, *, n_heads, n_kv, hd, n_t):
    scale = hd ** -0.5
    half = hd // 8
    n_g = n_heads // n_kv
    n_rows = n_t * n_heads
    nbuf = bk_ref.shape[1]
    width = n_kv * hd
    n_all = kall_ref.shape[0]
    knew = _rope(knew_ref[0], kc_ref[...], ku_ref[...], kd_ref[...], half)
    kall_ref[...] = jnp.zeros_like(kall_ref)
    vall_ref[...] = jnp.zeros_like(vall_ref)
    kall_ref[0:nbuf, :] = bk_ref[0]
    vall_ref[0:nbuf, :] = bv_ref[0]
    kall_ref[nbuf:nbuf + n_t, :] = knew
    vall_ref[nbuf:nbuf + n_t, :] = vnew_ref[0]
    ok_ref[0] = kall_ref[n_t:n_t + nbuf, :]
    ov_ref[0] = vall_ref[n_t:n_t + nbuf, :]
    q = q_ref[0]
    up = jnp.concatenate([q[:, half:], q[:, :half]], axis=1)
    dn = jnp.concatenate([q[:, hd - half:], q[:, :hd - half]], axis=1)
    q = q * qc_ref[...] + up * qu_ref[...] + dn * qd_ref[...]
    qt = jnp.concatenate([q] * n_kv, axis=1)
    row_kv = (lax.broadcasted_iota(jnp.int32, (n_rows, width), 0) % n_heads) // n_g
    lane_kv = lax.broadcasted_iota(jnp.int32, (n_rows, width), 1) // hd
    qbd = jnp.where(row_kv == lane_kv, qt, 0.0).astype(BF16)
    s = lax.dot_general(qbd, kall_ref[...].astype(BF16), NT_DIMS, preferred_element_type=F32) * scale
    tok = lax.broadcasted_iota(jnp.int32, (n_rows, n_all), 0) // n_heads
    key = lax.broadcasted_iota(jnp.int32, (n_rows, n_all), 1)
    band = ((key < nbuf) & (tok + nbuf - key < WINDOW)) | ((key >= nbuf) & (key - nbuf <= tok))
    s = jnp.where(band, s, -jnp.inf)
    sink = sink_ref[...]
    m = jnp.maximum(jnp.max(s, axis=1, keepdims=True), sink)
    e = jnp.exp(s - m)
    p = e / (jnp.sum(e, axis=1, keepdims=True) + jnp.exp(sink - m))
    out = jnp.dot(p.astype(BF16), vall_ref[...].astype(BF16), preferred_element_type=F32)
    row_kv = (lax.broadcasted_iota(jnp.int32, (n_rows, hd), 0) % n_heads) // n_g
    res = jnp.zeros((n_rows, hd), F32)
    for kv in range(n_kv):
        res = jnp.where(row_kv == kv, out[:, kv * hd:(kv + 1) * hd], res)
    o_ref[0] = res


def swa_sample(q_rows, qkv_s, buf_k, buf_v, sinks, past, *, n_heads, n_kv, hd):
    bsz, n_t = qkv_s.shape[0], qkv_s.shape[1]
    nbuf = buf_k.shape[1]
    assert nbuf == WINDOW and n_t <= 8
    width = n_kv * hd
    n_rows = n_t * n_heads
    n_all = nbuf + LANES
    pos = past + jnp.arange(n_t)
    qtab = tuple(jnp.repeat(t, n_heads, axis=0) for t in _rope_tables(pos, hd, 1))
    ktab = _rope_tables(pos, hd, LANES // hd)
    sink_rows = jnp.tile(sinks.astype(F32), n_t).reshape(n_rows, 1)
    kcol = n_heads * hd // width
    full2 = lambda shape: pl.BlockSpec(shape, lambda b: (0, 0))
    return pl.pallas_call(
        functools.partial(_swa_sample_kernel, n_heads=n_heads, n_kv=n_kv, hd=hd, n_t=n_t),
        grid=(bsz,),
        in_specs=[
            pl.BlockSpec((1, n_rows, hd), lambda b: (b, 0, 0)),
            pl.BlockSpec((1, n_t, width), lambda b: (b, 0, kcol)),
            pl.BlockSpec((1, n_t, width), lambda b: (b, 0, kcol + 1)),
            pl.BlockSpec((1, nbuf, width), lambda b: (b, 0, 0)),
            pl.BlockSpec((1, nbuf, width), lambda b: (b, 0, 0)),
            full2((n_rows, hd)), full2((n_rows, hd)), full2((n_rows, hd)),
            full2((n_t, LANES)), full2((n_t, LANES)), full2((n_t, LANES)),
            full2((n_rows, 1)),
        ],
        out_specs=[
            pl.BlockSpec((1, n_rows, hd), lambda b: (b, 0, 0)),
            pl.BlockSpec((1, nbuf, width), lambda b: (b, 0, 0)),
            pl.BlockSpec((1, nbuf, width), lambda b: (b, 0, 0)),
        ],
        out_shape=[
            jax.ShapeDtypeStruct((bsz, n_rows, hd), F32),
            jax.ShapeDtypeStruct((bsz, nbuf, width), F32),
            jax.ShapeDtypeStruct((bsz, nbuf, width), F32),
        ],
        scratch_shapes=[pltpu.VMEM((n_all, width), F32), pltpu.VMEM((n_all, width), F32)],
        compiler_params=_params("parallel"),
        name="swa_sample",
    )(q_rows, qkv_s, qkv_s, buf_k, buf_v, *qtab, *ktab, sink_rows)


def _rglru_kernel(gate_ref, u_ref, cb_ref, h0_ref, cw_ref, cbias_ref, wa_ref, ba_ref, wx_ref, bx_ref, lam_ref,
                  y_ref, hlast_ref, cout_ref, uext_ref, a_ref, d_ref, h_ref, *, n_t, n_seq, chunk):
    tc = pl.program_id(1)
    rows = n_t * n_seq
    tail = (cw_ref.shape[0] - 1) * n_seq
    head = uext_ref.shape[0] - rows
    n_blocks, cb = wa_ref.shape[0], wa_ref.shape[1]

    @pl.when(tc == 0)
    def _():
        uext_ref[head - tail:head, :] = cb_ref[0]
        h_ref[...] = h0_ref[0]

    @pl.when(tc > 0)
    def _():
        uext_ref[head - tail:head, :] = uext_ref[head + rows - tail:head + rows, :]

    uext_ref[head:head + rows, :] = u_ref[...]
    log_sig_lam = jax.nn.log_sigmoid(lam_ref[...])
    for c0 in range(0, rows, chunk):
        n = min(chunk, rows - c0)
        xc = cbias_ref[...]
        for i in range(cw_ref.shape[0]):
            xc = xc + uext_ref[head - tail + i * n_seq + c0:head - tail + i * n_seq + c0 + n, :] * cw_ref[i:i + 1, :]
        xb = xc.astype(BF16)
        r = jnp.concatenate([jnp.dot(xb[:, j * cb:(j + 1) * cb], wa_ref[j], preferred_element_type=F32)
                             for j in range(n_blocks)], axis=1)
        ig = jnp.concatenate([jnp.dot(xb[:, j * cb:(j + 1) * cb], wx_ref[j], preferred_element_type=F32)
                              for j in range(n_blocks)], axis=1)
        r = jax.nn.sigmoid(r + ba_ref[...])
        ig = jax.nn.sigmoid(ig + bx_ref[...])
        log_a = LRU_C * r * log_sig_lam
        a_ref[c0:c0 + n, :] = jnp.exp(log_a)
        d_ref[c0:c0 + n, :] = jnp.sqrt(-_expm1(2.0 * log_a)) * (ig * xc)

    def step(t, h):
        sl = pl.ds(pl.multiple_of(t * n_seq, n_seq), n_seq)
        h = a_ref[sl, :] * h + d_ref[sl, :]
        d_ref[sl, :] = h
        return h

    h_last = lax.fori_loop(0, n_t, step, h_ref[...], unroll=min(8, n_t))
    h_ref[...] = h_last
    for c0 in range(0, rows, chunk):
        n = min(chunk, rows - c0)
        y_ref[c0:c0 + n, :] = (d_ref[c0:c0 + n, :] * gate_ref[c0:c0 + n, :]).astype(y_ref.dtype)

    @pl.when(tc == pl.num_programs(1) - 1)
    def _():
        hlast_ref[0] = h_last
        cout_ref[0] = uext_ref[head + rows - tail:head + rows, :]


def rglru(gu, conv_buf, h0, conv_w, conv_b, w_a, b_a, w_x, b_x, lam, *, n_groups, n_t, n_seq, t_chunk, d_rnn):
    assert n_t % t_chunk == 0
    n_tc = n_t // t_chunk
    rows = t_chunk * n_seq
    tail = conv_buf.shape[1]
    assert tail <= rows
    head = -(-tail // 8) * 8
    chunk = min(rows, 256)
    vec = lambda a: a.reshape(1, d_rnn)
    const2 = lambda g, t: (0, 0)
    const3 = lambda g, t: (0, 0, 0)
    per_group = lambda g, t: (g, 0, 0)
    return pl.pallas_call(
        functools.partial(_rglru_kernel, n_t=t_chunk, n_seq=n_seq, chunk=chunk),
        grid=(n_groups, n_tc),
        in_specs=[
            pl.BlockSpec((rows, d_rnn), lambda g, t: (g * n_tc + t, 0)),
            pl.BlockSpec((rows, d_rnn), lambda g, t: (g * n_tc + t, 1)),
            pl.BlockSpec((1, tail, d_rnn), per_group),
            pl.BlockSpec((1, n_seq, d_rnn), per_group),
            pl.BlockSpec(conv_w.shape, const2),
            pl.BlockSpec((1, d_rnn), const2),
            pl.BlockSpec(w_a.shape, const3),
            pl.BlockSpec((1, d_rnn), const2),
            pl.BlockSpec(w_x.shape, const3),
            pl.BlockSpec((1, d_rnn), const2),
            pl.BlockSpec((1, d_rnn), const2),
        ],
        out_specs=[
            pl.BlockSpec((rows, d_rnn), lambda g, t: (g * n_tc + t, 0)),
            pl.BlockSpec((1, n_seq, d_rnn), per_group),
            pl.BlockSpec((1, tail, d_rnn), per_group),
        ],
        out_shape=[
            jax.ShapeDtypeStruct((gu.shape[0], d_rnn), BF16),
            jax.ShapeDtypeStruct((n_groups, n_seq, d_rnn), F32),
            jax.ShapeDtypeStruct((n_groups, tail, d_rnn), F32),
        ],
        scratch_shapes=[
            pltpu.VMEM((head + rows, d_rnn), F32),
            pltpu.VMEM((rows, d_rnn), F32),
            pltpu.VMEM((rows, d_rnn), F32),
            pltpu.VMEM((n_seq, d_rnn), F32),
        ],
        compiler_params=_params("parallel", "arbitrary"),
        name="rglru",
    )(gu, gu, conv_buf, h0, conv_w, vec(conv_b), w_a, vec(b_a), w_x, vec(b_x), vec(lam))


TM = 640
TN_MAX = 1280
TF = 1024


def _proj_tile(n):
    return max(t for t in range(LANES, TN_MAX + 1, LANES) if n % t == 0)
FOX_TQ = 512
FOX_BLK = 256
FOX_PAGES = 16
LRU_T_CHUNK = 256


def kernel(x_prompt, x_sample, cache_fox_k, cache_fox_v, cache_fox_logf, cache_swa_k, cache_swa_v, state_lru_h, state_lru_conv, page_table, norm_mix_pre, norm_mix_post, norm_mlp_pre, norm_mlp_post, mlp_w_up, mlp_w_down, fox_w_qkv, fox_w_f, fox_b_f, fox_w_o, swa_w_qkv, swa_b_qkv, swa_sinks, swa_w_o, swa_b_o, lru_w_gate, lru_b_gate, lru_w_in, lru_b_in, lru_conv_w, lru_conv_b, lru_w_a, lru_b_a, lru_w_x, lru_b_x, lru_lambda, lru_w_out, lru_b_out):
    bp, seq, d = x_prompt.shape
    bs, n_t, _ = x_sample.shape
    mp, ms = bp * seq, bs * n_t
    depth = norm_mix_pre.shape[0]
    fox_heads = fox_w_f.shape[2]
    fox_kv, fox_hd = cache_fox_k.shape[3], cache_fox_k.shape[4]
    swa_heads = swa_sinks.shape[1]
    swa_kv, swa_hd = cache_swa_k.shape[3], cache_swa_k.shape[4]
    d_rnn = lru_w_gate.shape[2]
    conv_w = lru_conv_w.shape[1]
    past = page_table.shape[1] * cache_fox_k.shape[2]
    fox_qw, fox_kw = fox_heads * fox_hd, fox_kv * fox_hd
    swa_qw, swa_kw = swa_heads * swa_hd, swa_kv * swa_hd

    x = jnp.concatenate([x_prompt.reshape(mp, d), x_sample.reshape(ms, d)], axis=0)
    n_layers_fox, pool, page = cache_fox_k.shape[:3]
    ck = cache_fox_k.reshape(n_layers_fox, pool, page * fox_kv, fox_hd)
    cv = cache_fox_v.reshape(n_layers_fox, pool, page * fox_kv, fox_hd)
    clf_t = jnp.swapaxes(cache_fox_logf, 2, 3)

    w_up_all, w_down_all = mlp_w_up.astype(BF16), mlp_w_down.astype(BF16)

    fkp, fvp, flp, fks, fvs, fls = [], [], [], [], [], []
    skp, svp, sks, svs = [], [], [], []
    lhp, lcp, lhs, lcs = [], [], [], []
    for i in range(depth):
        kind, j = i % 3, i // 3
        if kind == 0:
            qkv, logf, k_rows, v_rows = norm_matmul(x, norm_mix_pre[i], fox_w_qkv[j].astype(BF16),
                                                    gate=(fox_w_f[j].astype(BF16), fox_b_f[j]),
                                                    kv_heads=(fox_kv, fox_hd), tm=TM,
                                                    tn=_proj_tile(fox_w_qkv.shape[2]))
            fkp.append(k_rows[:mp * fox_kv].reshape(bp, seq, fox_kv, fox_hd))
            fvp.append(v_rows[:mp * fox_kv].reshape(bp, seq, fox_kv, fox_hd))
            flp.append(logf[:mp].reshape(bp, seq, fox_heads))
            qkv_s = qkv[mp:].reshape(bs, n_t, -1)
            lf_s = logf[mp:].reshape(bs, n_t, fox_heads)
            fks.append(k_rows[mp * fox_kv:].reshape(bs, n_t, fox_kv, fox_hd))
            fvs.append(v_rows[mp * fox_kv:].reshape(bs, n_t, fox_kv, fox_hd))
            fls.append(lf_s)
            qx, kx = fox_cumsum(logf[:mp].reshape(bp, seq, fox_heads), n_kv=fox_kv, chunk=FOX_BLK)
            att_p = fox_flash(qkv, qx, kx, batch=bp, seq=seq, n_heads=fox_heads, n_kv=fox_kv, hd=fox_hd,
                              tq=FOX_TQ, blk=FOX_BLK)
            fox_g = fox_heads // fox_kv
            q_rows = qkv_s[:, :, :fox_qw].reshape(bs, n_t, fox_kv, fox_g, fox_hd).transpose(0, 2, 1, 3, 4)
            q_rows = q_rows.reshape(bs, fox_kv, n_t * fox_g, fox_hd)
            lf_new_t = jnp.pad(jnp.swapaxes(lf_s, 1, 2), ((0, 0), (0, 0), (0, page - n_t)))
            att_s = fox_decode(q_rows, qkv_s, lf_new_t, ck, cv, clf_t, page_table, j,
                               n_heads=fox_heads, n_kv=fox_kv, hd=fox_hd, pages=FOX_PAGES)
            att_s = att_s.reshape(bs, fox_kv, n_t, fox_g, fox_hd).transpose(0, 2, 1, 3, 4).reshape(ms, fox_qw)
            att = lax.dynamic_update_slice(att_p, att_s.astype(BF16), (mp, 0))
            x = matmul_norm_residual(att, fox_w_o[j].astype(BF16), None, norm_mix_post[i], x, tm=TM)
        elif kind == 1:
            qkv = norm_matmul(x, norm_mix_pre[i], swa_w_qkv[j].astype(BF16), swa_b_qkv[j], tm=TM,
                              tn=_proj_tile(swa_w_qkv.shape[2]))
            att_p, k_last = swa_prompt(qkv, swa_sinks[j], batch=bp, seq=seq, n_heads=swa_heads, n_kv=swa_kv,
                                       hd=swa_hd, blk=WINDOW)
            keep = min(WINDOW, seq)
            skp.append(k_last.reshape(bp, keep, swa_kv, swa_hd))
            v_last = jnp.stack([qkv[(b + 1) * seq - keep:(b + 1) * seq, swa_qw + swa_kw:] for b in range(bp)])
            svp.append(v_last.reshape(bp, keep, swa_kv, swa_hd))
            qkv_s = qkv[mp:].reshape(bs, n_t, -1)
            q_rows = qkv_s[:, :, :swa_qw].reshape(bs, n_t * swa_heads, swa_hd)
            nbuf = cache_swa_k.shape[2]
            att_s, nk, nv = swa_sample(q_rows, qkv_s, cache_swa_k[j].reshape(bs, nbuf, swa_kw),
                                       cache_swa_v[j].reshape(bs, nbuf, swa_kw), swa_sinks[j], past,
                                       n_heads=swa_heads, n_kv=swa_kv, hd=swa_hd)
            sks.append(nk.reshape(bs, nbuf, swa_kv, swa_hd))
            svs.append(nv.reshape(bs, nbuf, swa_kv, swa_hd))
            att = lax.dynamic_update_slice(att_p, att_s.reshape(ms, swa_qw).astype(BF16), (mp, 0))
            x = matmul_norm_residual(att, swa_w_o[j].astype(BF16), swa_b_o[j], norm_mix_post[i], x, tm=TM)
        else:
            w_gu = jnp.concatenate([lru_w_gate[j], lru_w_in[j]], axis=1).astype(BF16)
            b_gu = jnp.concatenate([lru_b_gate[j], lru_b_in[j]])
            gu = norm_matmul(x, norm_mix_pre[i], w_gu, b_gu, n_gelu_cols=d_rnn, tm=TM, tn=_proj_tile(d_rnn))
            lw = (lru_conv_w[j], lru_conv_b[j], lru_w_a[j].astype(BF16), lru_b_a[j], lru_w_x[j].astype(BF16),
                  lru_b_x[j], lru_lambda[j])
            y_p, h_p, c_p = rglru(gu, jnp.zeros((bp, conv_w - 1, d_rnn), F32), jnp.zeros((bp, 1, d_rnn), F32),
                                  *lw, n_groups=bp, n_t=seq, n_seq=1, t_chunk=LRU_T_CHUNK, d_rnn=d_rnn)
            lhp.append(h_p.reshape(bp, d_rnn))
            lcp.append(c_p)
            gu_s = gu[mp:].reshape(bs, n_t, -1).swapaxes(0, 1).reshape(ms, -1)
            cb_s = state_lru_conv[j].swapaxes(0, 1).reshape(1, (conv_w - 1) * bs, d_rnn)
            y_s, h_s, c_s = rglru(gu_s, cb_s, state_lru_h[j].reshape(1, bs, d_rnn), *lw,
                                  n_groups=1, n_t=n_t, n_seq=bs, t_chunk=n_t, d_rnn=d_rnn)
            lhs.append(h_s.reshape(bs, d_rnn))
            lcs.append(c_s.reshape(conv_w - 1, bs, d_rnn).swapaxes(0, 1))
            y_s = y_s.reshape(n_t, bs, d_rnn).swapaxes(0, 1).reshape(ms, d_rnn)
            y = lax.dynamic_update_slice(y_p, y_s, (mp, 0))
            x = matmul_norm_residual(y, lru_w_out[j].astype(BF16), lru_b_out[j], norm_mix_post[i], x, tm=TM)
        x = mlp_sublayer(x, norm_mlp_pre[i], w_up_all, w_down_all, norm_mlp_post[i], i, tm=TM, tf=TF)
    return (x[:mp].reshape(bp, seq, d), x[mp:].reshape(bs, n_t, d),
            jnp.stack(fkp), jnp.stack(fvp), jnp.stack(flp),
            jnp.stack(fks), jnp.stack(fvs), jnp.stack(fls),
            jnp.stack(skp), jnp.stack(svp), jnp.stack(sks), jnp.stack(svs),
            jnp.stack(lhp), jnp.stack(lcp), jnp.stack(lhs), jnp.stack(lcs))
```

```python
import functools

import jax
import jax.numpy as jnp
from jax import lax
from jax.experimental import pallas as pl
from jax.experimental.pallas import tpu as pltpu

F32 = jnp.float32
BF16 = jnp.bfloat16

RMS_EPS = 1e-6
WINDOW = 128
ROPE_THETA = 500000.0
LRU_C = 8.0
LOG2E = 1.4426950408889634

V7X_VMEM_LIMIT_BYTES = 56 * 1024 * 1024
LANES = 128

NT_DIMS = (((1,), (1,)), ((), ()))


def _params(*sem):
    return pltpu.CompilerParams(dimension_semantics=sem, vmem_limit_bytes=V7X_VMEM_LIMIT_BYTES)


def _rms(x, g):
    ms = jnp.mean(x * x, axis=-1, keepdims=True)
    return x * lax.rsqrt(ms + RMS_EPS) * g


def _expm1(x):
    u = jnp.exp(x)
    um1 = u - 1.0
    safe = jnp.where(u == 1.0, 1.0, jnp.log(u))
    return jnp.where(u == 1.0, x, jnp.where(um1 == -1.0, -1.0, um1 * x / safe))


def _split3(x):
    hi = x.astype(BF16)
    r1 = x - hi.astype(F32)
    mid = r1.astype(BF16)
    lo = (r1 - mid.astype(F32)).astype(BF16)
    return hi, mid, lo


def _norm_mm_kernel(*refs, n_gelu_tiles, has_bias, has_gate, kv_rows):
    it = iter(refs)
    x_ref, g_ref, w_ref = next(it), next(it), next(it)
    b_ref = next(it) if has_bias else None
    wf_ref, bf_ref = (next(it), next(it)) if has_gate else (None, None)
    o_ref = next(it)
    lf_ref = next(it) if has_gate else None
    krows_ref, vrows_ref = (next(it), next(it)) if kv_rows else (None, None)
    xn_ref = next(it)
    j = pl.program_id(1)

    @pl.when(j == 0)
    def _():
        xn = _rms(x_ref[...], g_ref[...]).astype(BF16)
        xn_ref[...] = xn
        if has_gate:
            z = jnp.dot(xn, wf_ref[...], preferred_element_type=F32) + bf_ref[...]
            lf_ref[...] = jax.nn.log_sigmoid(z)

    acc = jnp.dot(xn_ref[...], w_ref[...], preferred_element_type=F32)
    if has_bias:
        acc = acc + b_ref[...]
    if kv_rows:
        n_kv, hd, off = kv_rows
        tm = o_ref.shape[0]

        @pl.when(j == pl.num_programs(1) - 1)
        def _():
            for kv in range(n_kv):
                krows_ref[pl.ds(kv, tm, stride=n_kv), :] = acc[:, off + kv * hd:off + (kv + 1) * hd]
                vrows_ref[pl.ds(kv, tm, stride=n_kv), :] = acc[:, off + (n_kv + kv) * hd:off + (n_kv + kv + 1) * hd]

    if n_gelu_tiles == 0:
        o_ref[...] = acc.astype(o_ref.dtype)
    else:
        @pl.when(j < n_gelu_tiles)
        def _():
            o_ref[...] = jax.nn.gelu(acc).astype(o_ref.dtype)

        @pl.when(j >= n_gelu_tiles)
        def _():
            o_ref[...] = acc.astype(o_ref.dtype)


def norm_matmul(x, g, w, b=None, *, gate=None, kv_heads=None, n_gelu_cols=0, tm, tn):
    m, d = x.shape
    n = w.shape[1]
    assert m % tm == 0 and n % tn == 0 and n_gelu_cols % tn == 0
    has_bias, has_gate = b is not None, gate is not None
    kv_rows = None
    if kv_heads is not None:
        n_kv, hd = kv_heads
        off = tn - 2 * n_kv * hd
        assert off >= 0 and hd % LANES == 0
        kv_rows = (n_kv, hd, off)
    args = [x, g.reshape(1, d), w]
    in_specs = [
        pl.BlockSpec((tm, d), lambda i, j: (i, 0)),
        pl.BlockSpec((1, d), lambda i, j: (0, 0)),
        pl.BlockSpec((d, tn), lambda i, j: (0, j)),
    ]
    if has_bias:
        args.append(b.reshape(1, n))
        in_specs.append(pl.BlockSpec((1, tn), lambda i, j: (0, j)))
    out_shape = [jax.ShapeDtypeStruct((m, n), F32)]
    out_specs = [pl.BlockSpec((tm, tn), lambda i, j: (i, j))]
    if has_gate:
        w_f, b_f = gate
        h = w_f.shape[1]
        args += [w_f, b_f.reshape(1, h)]
        in_specs += [pl.BlockSpec((d, h), lambda i, j: (0, 0)), pl.BlockSpec((1, h), lambda i, j: (0, 0))]
        out_shape.append(jax.ShapeDtypeStruct((m, h), F32))
        out_specs.append(pl.BlockSpec((tm, h), lambda i, j: (i, 0)))
    if kv_rows:
        out_shape += [jax.ShapeDtypeStruct((m * n_kv, hd), F32)] * 2
        out_specs += [pl.BlockSpec((tm * n_kv, hd), lambda i, j: (i, 0))] * 2
    outs = pl.pallas_call(
        functools.partial(_norm_mm_kernel, n_gelu_tiles=n_gelu_cols // tn, has_bias=has_bias, has_gate=has_gate,
                          kv_rows=kv_rows),
        grid=(m // tm, n // tn),
        in_specs=in_specs,
        out_specs=out_specs,
        out_shape=out_shape,
        scratch_shapes=[pltpu.VMEM((tm, d), BF16)],
        compiler_params=_params("parallel", "arbitrary"),
        name="norm_matmul",
    )(*args)
    return outs if len(outs) > 1 else outs[0]


def _mm_norm_res_kernel(*refs, has_bias):
    it = iter(refs)
    a_ref, w_ref = next(it), next(it)
    b_ref = next(it) if has_bias else None
    g_ref, x_ref, o_ref = next(it), next(it), next(it)
    mix = jnp.dot(a_ref[...], w_ref[...], preferred_element_type=F32)
    if has_bias:
        mix = mix + b_ref[...]
    o_ref[...] = x_ref[...] + _rms(mix, g_ref[...])


def matmul_norm_residual(a, w, b, g, x, *, tm):
    m, k = a.shape
    d = w.shape[1]
    assert m % tm == 0
    has_bias = b is not None
    args = [a, w]
    in_specs = [pl.BlockSpec((tm, k), lambda i: (i, 0)), pl.BlockSpec((k, d), lambda i: (0, 0))]
    if has_bias:
        args.append(b.reshape(1, d))
        in_specs.append(pl.BlockSpec((1, d), lambda i: (0, 0)))
    args += [g.reshape(1, d), x]
    in_specs += [pl.BlockSpec((1, d), lambda i: (0, 0)), pl.BlockSpec((tm, d), lambda i: (i, 0))]
    return pl.pallas_call(
        functools.partial(_mm_norm_res_kernel, has_bias=has_bias),
        grid=(m // tm,),
        in_specs=in_specs,
        out_specs=pl.BlockSpec((tm, d), lambda i: (i, 0)),
        out_shape=jax.ShapeDtypeStruct((m, d), F32),
        compiler_params=_params("parallel"),
        name="matmul_norm_residual",
    )(*args)


def _mlp_kernel(x_ref, gpre_ref, wup_ref, wdn_ref, gpost_ref, o_ref, xn_ref, acc_ref):
    f = pl.program_id(1)

    @pl.when(f == 0)
    def _():
        xn_ref[...] = _rms(x_ref[...], gpre_ref[...]).astype(BF16)
        acc_ref[...] = jnp.zeros_like(acc_ref)

    h = jnp.dot(xn_ref[...], wup_ref[0], preferred_element_type=F32)
    h = jnp.square(jnp.maximum(h, 0.0)).astype(BF16)
    acc_ref[...] += jnp.dot(h, wdn_ref[0], preferred_element_type=F32)

    @pl.when(f == pl.num_programs(1) - 1)
    def _():
        o_ref[...] = x_ref[...] + _rms(acc_ref[...], gpost_ref[...])


def mlp_sublayer(x, g_pre, w_up, w_down, g_post, layer, *, tm, tf):
    m, d = x.shape
    dff = w_up.shape[2]
    assert m % tm == 0 and dff % tf == 0
    return pl.pallas_call(
        _mlp_kernel,
        grid=(m // tm, dff // tf),
        in_specs=[
            pl.BlockSpec((tm, d), lambda i, f: (i, 0)),
            pl.BlockSpec((1, d), lambda i, f: (0, 0)),
            pl.BlockSpec((1, d, tf), lambda i, f: (layer, 0, f)),
            pl.BlockSpec((1, tf, d), lambda i, f: (layer, f, 0)),
            pl.BlockSpec((1, d), lambda i, f: (0, 0)),
        ],
        out_specs=pl.BlockSpec((tm, d), lambda i, f: (i, 0)),
        out_shape=jax.ShapeDtypeStruct((m, d), F32),
        scratch_shapes=[pltpu.VMEM((tm, d), BF16), pltpu.VMEM((tm, d), F32)],
        compiler_params=_params("parallel", "arbitrary"),
        name="mlp_sublayer",
    )(x, g_pre.reshape(1, d), w_up, w_down, g_post.reshape(1, d))


N_PARTS = 3


def _fox_cumsum_kernel(lf_ref, qx_ref, kx_ref, *, chunk, n_kv, n_g):
    s, h = lf_ref.shape[1], lf_ref.shape[2]
    row = lax.broadcasted_iota(jnp.int32, (chunk, chunk), 0)
    col = lax.broadcasted_iota(jnp.int32, (chunk, chunk), 1)
    lower = (row >= col).astype(BF16)

    def placement(n_tiles, lane_of, head_of):
        r = lax.broadcasted_iota(jnp.int32, (h, n_tiles * LANES), 0)
        c = lax.broadcasted_iota(jnp.int32, (h, n_tiles * LANES), 1)
        tile, lane = c // LANES, c % LANES
        mats = []
        for p in range(N_PARTS):
            hit = jnp.zeros((h, n_tiles * LANES), jnp.bool_)
            for g in range(n_g):
                hit = hit | ((r == head_of(tile, g)) & (lane == lane_of(g, p)))
            mats.append(hit)
        return mats, tile, lane

    q_hit, q_tile, q_lane = placement(h, lambda g, p: p, lambda tile, g: jnp.where(tile % n_g == g, tile, -1))
    q_mats = [m.astype(BF16) for m in q_hit]
    q_gl = N_PARTS * (1 + q_tile[0:1] % n_g)
    q_ones = ((q_lane[0:1] >= q_gl) & (q_lane[0:1] < q_gl + N_PARTS)).astype(F32)
    k_hit, _, k_lane = placement(n_kv, lambda g, p: N_PARTS * (1 + g) + p, lambda tile, g: tile * n_g + g)
    k_mats = [(-m.astype(F32)).astype(BF16) for m in k_hit]
    k_ones = (k_lane[0:1] < N_PARTS).astype(F32)

    carry = jnp.zeros((1, h), F32)
    for c in range(s // chunk):
        rows = slice(c * chunk, (c + 1) * chunk)
        hi, mid, lo = _split3(lf_ref[0, rows, :])
        cs = (jnp.dot(lower, hi, preferred_element_type=F32)
              + jnp.dot(lower, mid, preferred_element_type=F32)
              + jnp.dot(lower, lo, preferred_element_type=F32)) + carry
        carry = cs[chunk - 1:chunk, :]
        parts = _split3(cs * LOG2E)
        qx = q_ones + sum(jnp.dot(parts[p], q_mats[p], preferred_element_type=F32) for p in range(N_PARTS))
        kx = k_ones + sum(jnp.dot(parts[p], k_mats[p], preferred_element_type=F32) for p in range(N_PARTS))
        for hh in range(h):
            qx_ref[0, hh, rows, :] = qx[:, hh * LANES:(hh + 1) * LANES].astype(BF16)
        for kv in range(n_kv):
            kx_ref[0, kv, rows, :] = kx[:, kv * LANES:(kv + 1) * LANES].astype(BF16)


def fox_cumsum(logf, *, n_kv, chunk):
    b, s, h = logf.shape
    n_g = h // n_kv
    assert N_PARTS * (1 + n_g) <= LANES
    return pl.pallas_call(
        functools.partial(_fox_cumsum_kernel, chunk=chunk, n_kv=n_kv, n_g=n_g),
        grid=(b,),
        in_specs=[pl.BlockSpec((1, s, h), lambda i: (i, 0, 0))],
        out_specs=[
            pl.BlockSpec((1, h, s, LANES), lambda i: (i, 0, 0, 0)),
            pl.BlockSpec((1, n_kv, s, LANES), lambda i: (i, 0, 0, 0)),
        ],
        out_shape=[
            jax.ShapeDtypeStruct((b, h, s, LANES), BF16),
            jax.ShapeDtypeStruct((b, n_kv, s, LANES), BF16),
        ],
        compiler_params=_params("parallel"),
        name="fox_cumsum",
    )(logf)


def _lane_tiles(x):
    return [x[:, j * LANES:(j + 1) * LANES] for j in range(x.shape[1] // LANES)]


def _fox_flash_kernel(q_ref, k_ref, v_ref, qx_ref, kx_ref, o_ref, ka_ref, vb_ref, s_ref, m_ref, l_ref, acc_ref,
                      *, tq, blk, n_g, hd):
    qi = pl.program_id(2)
    per_q = tq // blk

    @pl.when(qi == 0)
    def _():
        ka_ref[:, :hd] = k_ref[...].astype(BF16)
        ka_ref[:, hd:] = kx_ref[0, 0]
        vb_ref[...] = v_ref[...].astype(BF16)

    q_scale = hd ** -0.5 * LOG2E
    qa = [jnp.concatenate([(q_ref[:, g * hd:(g + 1) * hd] * q_scale).astype(BF16), qx_ref[0, g]], axis=1)
          for g in range(n_g)]
    m_ref[...] = jnp.full_like(m_ref, -jnp.inf)

    def logits_pass(kc, diag):
        ka = ka_ref[pl.ds(pl.multiple_of(kc * blk, blk), blk), :]
        if diag is not None:
            row = lax.broadcasted_iota(jnp.int32, (tq, blk), 0)
            col = lax.broadcasted_iota(jnp.int32, (tq, blk), 1) + diag * blk
            causal = row >= col
        for g in range(n_g):
            s = lax.dot_general(qa[g], ka, NT_DIMS, preferred_element_type=F32)
            if diag is not None:
                s = jnp.where(causal, s, -jnp.inf)
            s_ref[g, kc] = s
            m = m_ref[g]
            for t in _lane_tiles(s):
                m = jnp.maximum(m, t)
            m_ref[g] = m

    def body1(kc, carry):
        logits_pass(kc, None)
        return carry

    lax.fori_loop(0, per_q * qi, body1, 0)
    for e in range(per_q):
        logits_pass(per_q * qi + e, e)
    for g in range(n_g):
        m_ref[g] = jnp.broadcast_to(jnp.max(m_ref[g], axis=1, keepdims=True), (tq, LANES))
    l_ref[...] = jnp.zeros_like(l_ref)
    acc_ref[...] = jnp.zeros_like(acc_ref)

    def body2(kc, carry):
        vb = vb_ref[pl.ds(pl.multiple_of(kc * blk, blk), blk), :]
        for g in range(n_g):
            m = m_ref[g]
            p = [jnp.exp2(t - m) for t in _lane_tiles(s_ref[g, kc])]
            l_ref[g] += sum(p)
            acc_ref[g] += jnp.dot(jnp.concatenate(p, axis=1).astype(BF16), vb, preferred_element_type=F32)
        return carry

    lax.fori_loop(0, per_q * (qi + 1), body2, 0)
    for g in range(n_g):
        inv = 1.0 / jnp.sum(l_ref[g], axis=1, keepdims=True)
        o_ref[:, g * hd:(g + 1) * hd] = (acc_ref[g] * inv).astype(o_ref.dtype)


def fox_flash(qkv, qx, kx, *, batch, seq, n_heads, n_kv, hd, tq, blk):
    assert hd == LANES and tq % blk == 0 and seq % tq == 0
    n_g = n_heads // n_kv
    nq = seq // tq
    gw = n_g * hd
    return pl.pallas_call(
        functools.partial(_fox_flash_kernel, tq=tq, blk=blk, n_g=n_g, hd=hd),
        grid=(batch, n_kv, nq),
        in_specs=[
            pl.BlockSpec((tq, gw), lambda b, kv, qi: (b * nq + qi, kv)),
            pl.BlockSpec((seq, hd), lambda b, kv, qi: (b, n_heads + kv)),
            pl.BlockSpec((seq, hd), lambda b, kv, qi: (b, n_heads + n_kv + kv)),
            pl.BlockSpec((1, n_g, tq, LANES), lambda b, kv, qi: (b, kv, qi, 0)),
            pl.BlockSpec((1, 1, seq, LANES), lambda b, kv, qi: (b, kv, 0, 0)),
        ],
        out_specs=pl.BlockSpec((tq, gw), lambda b, kv, qi: (b * nq + qi, kv)),
        out_shape=jax.ShapeDtypeStruct((qkv.shape[0], n_heads * hd), BF16),
        scratch_shapes=[
            pltpu.VMEM((seq, hd + LANES), BF16),
            pltpu.VMEM((seq, hd), BF16),
            pltpu.VMEM((n_g, seq // blk, tq, blk), F32),
            pltpu.VMEM((n_g, tq, LANES), F32),
            pltpu.VMEM((n_g, tq, LANES), F32),
            pltpu.VMEM((n_g, tq, hd), F32),
        ],
        compiler_params=_params("parallel", "parallel", "arbitrary"),
        name="fox_flash",
    )(qkv, qkv, qkv, qx, kx)


def _fox_decode_kernel(pt_ref, q_ref, knew_ref, vnew_ref, lfnew_ref, *rest, pages, n_kv, n_heads, hd, n_t):
    del pt_ref
    k_refs, v_refs, lf_refs = rest[:pages], rest[pages:2 * pages], rest[2 * pages:3 * pages]
    o_ref, cq_ref, m_ref, l_ref, acc_ref, carry_ref, kpad_ref, vpad_ref = rest[3 * pages:]
    c = pl.program_id(1)
    scale = hd ** -0.5
    page = lf_refs[0].shape[3]
    n_g = n_heads // n_kv
    grp = n_t * n_g
    jj = lax.broadcasted_iota(jnp.int32, (page, page), 0)
    ss = lax.broadcasted_iota(jnp.int32, (page, page), 1)

    def lane_sums(xs, tri):
        stacked = jnp.concatenate([part for x in xs for part in _split3(x)], axis=0)
        out = jnp.dot(stacked, tri, preferred_element_type=F32)
        n = N_PARTS * n_heads
        return [out[i * n:i * n + n_heads] + out[i * n + n_heads:i * n + 2 * n_heads]
                + out[i * n + 2 * n_heads:(i + 1) * n] for i in range(len(xs))]

    def group_rows(x, kv):
        return jnp.concatenate([x[kv * n_g:(kv + 1) * n_g]] * n_t, axis=0)

    q = [q_ref[0, kv].astype(BF16) for kv in range(n_kv)]

    @pl.when(c == 0)
    def _():
        c_new = lane_sums([lfnew_ref[0]], (jj <= ss).astype(BF16))[0]
        kpad_ref[...] = jnp.zeros_like(kpad_ref)
        vpad_ref[...] = jnp.zeros_like(vpad_ref)
        tok = lax.broadcasted_iota(jnp.int32, (grp, page), 0) // n_g
        key = lax.broadcasted_iota(jnp.int32, (grp, page), 1)
        for kv in range(n_kv):
            cq = jnp.concatenate([c_new[kv * n_g:(kv + 1) * n_g, t:t + 1] for t in range(n_t)], axis=0)
            cq_ref[kv] = cq
            kpad_ref[kv, 0:n_t, :] = knew_ref[0, :, kv * hd:(kv + 1) * hd]
            vpad_ref[kv, 0:n_t, :] = vnew_ref[0, :, kv * hd:(kv + 1) * hd]
            s = lax.dot_general(q[kv], kpad_ref[kv].astype(BF16), NT_DIMS, preferred_element_type=F32) * scale
            s = s + (cq - group_rows(c_new, kv))
            s = jnp.where(key <= tok, s, -jnp.inf)
            m = jnp.max(s, axis=1, keepdims=True)
            p = jnp.exp(s - m)
            m_ref[kv] = m
            l_ref[kv] = jnp.sum(p, axis=1, keepdims=True)
            acc_ref[kv] = jnp.dot(p.astype(BF16), vpad_ref[kv].astype(BF16), preferred_element_type=F32)
        carry_ref[...] = jnp.zeros_like(carry_ref)

    lfs = [lf_refs[i][0, 0] for i in range(pages)]
    local = lane_sums(lfs, (jj > ss).astype(BF16))
    carry = carry_ref[...]
    rs = []
    for i in range(pages):
        rs.append(local[i] + carry)
        carry = carry + (local[i][:, 0:1] + lfs[i][:, 0:1])
    carry_ref[...] = carry
    def pair(refs, i, kv):
        return jnp.concatenate([refs[i + e][0, 0, pl.ds(kv, page, stride=n_kv), :].astype(BF16) for e in range(2)],
                               axis=0)

    raw = []
    for kv in range(n_kv):
        both = [lax.dot_general(q[kv], pair(k_refs, i, kv), NT_DIMS, preferred_element_type=F32)
                for i in range(0, pages, 2)]
        raw.append([b[:, e * page:(e + 1) * page] for b in both for e in range(2)])
    probs, alphas = [], []
    for kv in range(n_kv):
        cq = cq_ref[kv]
        logits = [raw[kv][i] * scale + (cq + group_rows(rs[i], kv)) for i in range(pages)]
        m_old = m_ref[kv]
        tile_max = logits[0]
        for s in logits[1:]:
            tile_max = jnp.maximum(tile_max, s)
        m_new = jnp.maximum(m_old, jnp.max(tile_max, axis=1, keepdims=True))
        alpha = jnp.exp(m_old - m_new)
        ps = [jnp.exp(s - m_new) for s in logits]
        l_ref[kv] = alpha * l_ref[kv] + jnp.sum(sum(ps), axis=1, keepdims=True)
        m_ref[kv] = m_new
        probs.append([p.astype(BF16) for p in ps])
        alphas.append(alpha)
    for kv in range(n_kv):
        pv = sum(jnp.dot(jnp.concatenate(probs[kv][i:i + 2], axis=1), pair(v_refs, i, kv),
                         preferred_element_type=F32) for i in range(0, pages, 2))
        acc_ref[kv] = alphas[kv] * acc_ref[kv] + pv

    @pl.when(c == pl.num_programs(1) - 1)
    def _():
        for kv in range(n_kv):
            o_ref[0, kv] = acc_ref[kv] / l_ref[kv]


def fox_decode(q_rows, qkv_s, lf_new_t, cache_k, cache_v, cache_lf_t, page_table, layer, *, n_heads, n_kv, hd, pages):
    bsz, n_t = qkv_s.shape[0], qkv_s.shape[1]
    n_pages = page_table.shape[1]
    page = cache_lf_t.shape[3]
    width = n_kv * hd
    grp = n_t * (n_heads // n_kv)
    assert n_pages % pages == 0 and page == LANES and cache_k.shape[2] == page * n_kv
    n_chunks = n_pages // pages

    def page_map(i):
        return lambda b, c, pt: (layer, pt[b, n_pages - 1 - (c * pages + i)], 0, 0)

    kv_col = n_heads * hd // width
    in_specs = [
        pl.BlockSpec((1, n_kv, grp, hd), lambda b, c, pt: (b, 0, 0, 0)),
        pl.BlockSpec((1, n_t, width), lambda b, c, pt: (b, 0, kv_col)),
        pl.BlockSpec((1, n_t, width), lambda b, c, pt: (b, 0, kv_col + 1)),
        pl.BlockSpec((1, n_heads, page), lambda b, c, pt: (b, 0, 0)),
    ]
    in_specs += [pl.BlockSpec((1, 1, page * n_kv, hd), page_map(i)) for i in range(pages)]
    in_specs += [pl.BlockSpec((1, 1, page * n_kv, hd), page_map(i)) for i in range(pages)]
    in_specs += [pl.BlockSpec((1, 1, n_heads, page), page_map(i)) for i in range(pages)]
    grid_spec = pltpu.PrefetchScalarGridSpec(
        num_scalar_prefetch=1,
        grid=(bsz, n_chunks),
        in_specs=in_specs,
        out_specs=pl.BlockSpec((1, n_kv, grp, hd), lambda b, c, pt: (b, 0, 0, 0)),
        scratch_shapes=[
            pltpu.VMEM((n_kv, grp, 1), F32),
            pltpu.VMEM((n_kv, grp, 1), F32),
            pltpu.VMEM((n_kv, grp, 1), F32),
            pltpu.VMEM((n_kv, grp, hd), F32),
            pltpu.VMEM((n_heads, 1), F32),
            pltpu.VMEM((n_kv, page, hd), F32),
            pltpu.VMEM((n_kv, page, hd), F32),
        ],
    )
    return pl.pallas_call(
        functools.partial(_fox_decode_kernel, pages=pages, n_kv=n_kv, n_heads=n_heads, hd=hd, n_t=n_t),
        grid_spec=grid_spec,
        out_shape=jax.ShapeDtypeStruct((bsz, n_kv, grp, hd), F32),
        compiler_params=_params("parallel", "arbitrary"),
        name="fox_decode",
    )(page_table, q_rows, qkv_s, qkv_s, lf_new_t, *([cache_k] * pages), *([cache_v] * pages), *([cache_lf_t] * pages))


def _rope_tables(pos, hd, reps):
    rot = hd // 4
    half = rot // 2
    inv_freq = ROPE_THETA ** (-jnp.arange(half, dtype=F32) / half)
    ang = pos.astype(F32)[:, None] * inv_freq[None, :]
    cos, sin = jnp.cos(ang), jnp.sin(ang)
    n = pos.shape[0]
    ones = jnp.ones((n, hd - rot), F32)
    zeros_h = jnp.zeros((n, half), F32)
    zeros_r = jnp.zeros((n, hd - rot), F32)
    c = jnp.concatenate([cos, cos, ones], axis=1)
    s_up = jnp.concatenate([-sin, zeros_h, zeros_r], axis=1)
    s_dn = jnp.concatenate([zeros_h, sin, zeros_r], axis=1)
    return tuple(jnp.tile(t, (1, reps)) for t in (c, s_up, s_dn))


def _rope(x, c, s_up, s_dn, half):
    w = x.shape[1]
    pieces = []
    for j in range(w // LANES):
        xj = x[:, j * LANES:(j + 1) * LANES]
        up = pltpu.roll(xj, LANES - half, 1)
        dn = pltpu.roll(xj, half, 1)
        pieces.append(xj * c + up * s_up + dn * s_dn)
    return pieces[0] if len(pieces) == 1 else jnp.concatenate(pieces, axis=1)


def _swa_prompt_kernel(q_ref, kc_ref, kp_ref, vc_ref, vp_ref, cc_ref, su_ref, sd_ref, pc_ref, pu_ref, pd_ref,
                       sink_ref, o_ref, klast_ref, *, n_heads, n_kv, hd, blk):
    qb = pl.program_id(1)
    half = hd // 8
    n_g = n_heads // n_kv
    per_tile = LANES // hd
    assert per_tile == 2 and n_g % per_tile == 0
    nk = 2 * blk
    q = _rope(q_ref[...], cc_ref[...], su_ref[...], sd_ref[...], half) * (hd ** -0.5 * LOG2E)
    kc = _rope(kc_ref[...], cc_ref[...], su_ref[...], sd_ref[...], half)
    kp = _rope(kp_ref[...], pc_ref[...], pu_ref[...], pd_ref[...], half)
    k = jnp.concatenate([kp, kc], axis=0)
    v = jnp.concatenate([vp_ref[...], vc_ref[...]], axis=0)
    lane = lax.broadcasted_iota(jnp.int32, (nk, LANES), 1)
    d_row = lax.broadcasted_iota(jnp.int32, (LANES, nk), 0)
    out_row = lax.broadcasted_iota(jnp.int32, (LANES, blk), 0)

    def spread_keys(kv):
        tile = k[:, (kv // per_tile) * LANES:(kv // per_tile + 1) * LANES]
        swapped = pltpu.roll(tile, hd, 1)
        low, high = (tile, swapped) if kv % per_tile == 0 else (swapped, tile)
        return jnp.concatenate([jnp.where(lane < hd, low, 0.0), jnp.where(lane >= hd, high, 0.0)],
                               axis=0).astype(BF16)

    def spread_values_t(kv):
        tile_t = v[:, (kv // per_tile) * LANES:(kv // per_tile + 1) * LANES].T
        swapped_t = jnp.concatenate([tile_t[hd:], tile_t[:hd]], axis=0)
        low, high = (tile_t, swapped_t) if kv % per_tile == 0 else (swapped_t, tile_t)
        return jnp.concatenate([jnp.where(d_row < hd, low, 0.0), jnp.where(d_row >= hd, high, 0.0)],
                               axis=1).astype(BF16)

    j = lax.broadcasted_iota(jnp.int32, (nk, blk), 0)
    t = lax.broadcasted_iota(jnp.int32, (nk, blk), 1)
    first_key = jnp.where(qb > 0, 0, blk)
    band = (j > t) & (j <= t + blk) & (j >= first_key)
    pairs = n_g // per_tile
    n_tiles = n_kv * pairs
    k2 = [spread_keys(kv) for kv in range(n_kv)]
    raw = [lax.dot_general(k2[i // pairs], q[:, i * LANES:(i + 1) * LANES].astype(BF16), NT_DIMS,
                           preferred_element_type=F32) for i in range(n_tiles)]
    weights, scales = [], []
    for i in range(n_tiles):
        es, invs = [], []
        for e in range(per_tile):
            h = per_tile * i + e
            se = jnp.where(band, raw[i][e * nk:(e + 1) * nk], -jnp.inf)
            sink = sink_ref[0:1, h:h + 1] * LOG2E
            m = jnp.maximum(jnp.max(se, axis=0, keepdims=True), sink)
            ee = jnp.exp2(se - m)
            es.append(ee)
            invs.append(1.0 / (jnp.sum(ee, axis=0, keepdims=True) + jnp.exp2(sink - m)))
        weights.append(jnp.concatenate(es, axis=0).astype(BF16))
        scales.append(jnp.where(out_row < hd, invs[0], invs[1]))
    v2t = [spread_values_t(kv) for kv in range(n_kv)]
    for i in range(n_tiles):
        out_t = jnp.dot(v2t[i // pairs], weights[i], preferred_element_type=F32) * scales[i]
        o_ref[:, i * LANES:(i + 1) * LANES] = out_t.T.astype(o_ref.dtype)

    @pl.when(qb == pl.num_programs(1) - 1)
    def _():
        klast_ref[0] = kc


def swa_prompt(qkv, sinks, *, batch, seq, n_heads, n_kv, hd, blk):
    assert blk == WINDOW and (n_kv * hd) % LANES == 0
    nb = seq // blk
    qw, kw = n_heads * hd, n_kv * hd
    kcol = qw // kw
    pos = jnp.arange(seq)
    reps = LANES // hd
    cur = _rope_tables(pos, hd, reps)
    prev = _rope_tables(pos - blk, hd, reps)

    def cur_row(b, qb):
        return b * nb + qb

    def prev_row(b, qb):
        return b * nb + jnp.maximum(qb - 1, 0)

    tab_cur = pl.BlockSpec((blk, LANES), lambda b, qb: (qb, 0))
    tab_prev = pl.BlockSpec((blk, LANES), lambda b, qb: (qb, 0))
    return pl.pallas_call(
        functools.partial(_swa_prompt_kernel, n_heads=n_heads, n_kv=n_kv, hd=hd, blk=blk),
        grid=(batch, nb),
        in_specs=[
            pl.BlockSpec((blk, qw), lambda b, qb: (cur_row(b, qb), 0)),
            pl.BlockSpec((blk, kw), lambda b, qb: (cur_row(b, qb), kcol)),
            pl.BlockSpec((blk, kw), lambda b, qb: (prev_row(b, qb), kcol)),
            pl.BlockSpec((blk, kw), lambda b, qb: (cur_row(b, qb), kcol + 1)),
            pl.BlockSpec((blk, kw), lambda b, qb: (prev_row(b, qb), kcol + 1)),
            tab_cur, tab_cur, tab_cur, tab_prev, tab_prev, tab_prev,
            pl.BlockSpec((1, n_heads), lambda b, qb: (0, 0)),
        ],
        out_specs=[
            pl.BlockSpec((blk, qw), lambda b, qb: (cur_row(b, qb), 0)),
            pl.BlockSpec((1, blk, kw), lambda b, qb: (b, 0, 0)),
        ],
        out_shape=[
            jax.ShapeDtypeStruct((qkv.shape[0], qw), BF16),
            jax.ShapeDtypeStruct((batch, blk, kw), F32),
        ],
        compiler_params=_params("parallel", "arbitrary"),
        name="swa_prompt",
    )(qkv, qkv, qkv, qkv, qkv, *cur, *prev, sinks.reshape(1, n_heads))


def _swa_sample_kernel(q_ref, knew_ref, vnew_ref, bk_ref, bv_ref, qc_ref, qu_ref, qd_ref, kc_ref, ku_ref, kd_ref,
                       sink_ref, o_ref, ok_ref, ov_ref, kall_ref, vall_ref, *, n_heads, n_kv, hd, n_t):
    scale = hd ** -0.5
    half = hd // 8
    n_g = n_heads // n_kv
    n_rows = n_t * n_heads
    nbuf = bk_ref.shape[1]
    width = n_kv * hd
    n_all = kall_ref.shape[0]
    knew = _rope(knew_ref[0], kc_ref[...], ku_ref[...], kd_ref[...], half)
    kall_ref[...] = jnp.zeros_like(kall_ref)
    vall_ref[...] = jnp.zeros_like(vall_ref)
    kall_ref[0:nbuf, :] = bk_ref[0]
    vall_ref[0:nbuf, :] = bv_ref[0]
    kall_ref[nbuf:nbuf + n_t, :] = knew
    vall_ref[nbuf:nbuf + n_t, :] = vnew_ref[0]
    ok_ref[0] = kall_ref[n_t:n_t + nbuf, :]
    ov_ref[0] = vall_ref[n_t:n_t + nbuf, :]
    q = q_ref[0]
    up = jnp.concatenate([q[:, half:], q[:, :half]], axis=1)
    dn = jnp.concatenate([q[:, hd - half:], q[:, :hd - half]], axis=1)
    q = q * qc_ref[...] + up * qu_ref[...] + dn * qd_ref[...]
    qt = jnp.concatenate([q] * n_kv, axis=1)
    row_kv = (lax.broadcasted_iota(jnp.int32, (n_rows, width), 0) % n_heads) // n_g
    lane_kv = lax.broadcasted_iota(jnp.int32, (n_rows, width), 1) // hd
    qbd = jnp.where(row_kv == lane_kv, qt, 0.0).astype(BF16)
    s = lax.dot_general(qbd, kall_ref[...].astype(BF16), NT_DIMS, preferred_element_type=F32) * scale
    tok = lax.broadcasted_iota(jnp.int32, (n_rows, n_all), 0) // n_heads
    key = lax.broadcasted_iota(jnp.int32, (n_rows, n_all), 1)
    band = ((key < nbuf) & (tok + nbuf - key < WINDOW)) | ((key >= nbuf) & (key - nbuf <= tok))
    s = jnp.where(band, s, -jnp.inf)
    sink = sink_ref[...]
    m = jnp.maximum(jnp.max(s, axis=1, keepdims=True), sink)
    e = jnp.exp(s - m)
    p = e / (jnp.sum(e, axis=1, keepdims=True) + jnp.exp(sink - m))
    out = jnp.dot(p.astype(BF16), vall_ref[...].astype(BF16), preferred_element_type=F32)
    row_kv = (lax.broadcasted_iota(jnp.int32, (n_rows, hd), 0) % n_heads) // n_g
    res = jnp.zeros((n_rows, hd), F32)
    for kv in range(n_kv):
        res = jnp.where(row_kv == kv, out[:, kv * hd:(kv + 1) * hd], res)
    o_ref[0] = res


def swa_sample(q_rows, qkv_s, buf_k, buf_v, sinks, past, *, n_heads, n_kv, hd):
    bsz, n_t = qkv_s.shape[0], qkv_s.shape[1]
    nbuf = buf_k.shape[1]
    assert nbuf == WINDOW and n_t <= 8
    width = n_kv * hd
    n_rows = n_t * n_heads
    n_all = nbuf + LANES
    pos = past + jnp.arange(n_t)
    qtab = tuple(jnp.repeat(t, n_heads, axis=0) for t in _rope_tables(pos, hd, 1))
    ktab = _rope_tables(pos, hd, LANES // hd)
    sink_rows = jnp.tile(sinks.astype(F32), n_t).reshape(n_rows, 1)
    kcol = n_heads * hd // width
    full2 = lambda shape: pl.BlockSpec(shape, lambda b: (0, 0))
    return pl.pallas_call(
        functools.partial(_swa_sample_kernel, n_heads=n_heads, n_kv=n_kv, hd=hd, n_t=n_t),
        grid=(bsz,),
        in_specs=[
            pl.BlockSpec((1, n_rows, hd), lambda b: (b, 0, 0)),
            pl.BlockSpec((1, n_t, width), lambda b: (b, 0, kcol)),
            pl.BlockSpec((1, n_t, width), lambda b: (b, 0, kcol + 1)),
            pl.BlockSpec((1, nbuf, width), lambda b: (b, 0, 0)),
            pl.BlockSpec((1, nbuf, width), lambda b: (b, 0, 0)),
            full2((n_rows, hd)), full2((n_rows, hd)), full2((n_rows, hd)),
            full2((n_t, LANES)), full2((n_t, LANES)), full2((n_t, LANES)),
            full2((n_rows, 1)),
        ],
        out_specs=[
            pl.BlockSpec((1, n_rows, hd), lambda b: (b, 0, 0)),
            pl.BlockSpec((1, nbuf, width), lambda b: (b, 0, 0)),
            pl.BlockSpec((1, nbuf, width), lambda b: (b, 0, 0)),
        ],
        out_shape=[
            jax.ShapeDtypeStruct((bsz, n_rows, hd), F32),
            jax.ShapeDtypeStruct((bsz, nbuf, width), F32),
            jax.ShapeDtypeStruct((bsz, nbuf, width), F32),
        ],
        scratch_shapes=[pltpu.VMEM((n_all, width), F32), pltpu.VMEM((n_all, width), F32)],
        compiler_params=_params("parallel"),
        name="swa_sample",
    )(q_rows, qkv_s, qkv_s, buf_k, buf_v, *qtab, *ktab, sink_rows)


def _rglru_kernel(gate_ref, u_ref, cb_ref, h0_ref, cw_ref, cbias_ref, wa_ref, ba_ref, wx_ref, bx_ref, lam_ref,
                  y_ref, hlast_ref, cout_ref, uext_ref, a_ref, d_ref, h_ref, *, n_t, n_seq, chunk):
    tc = pl.program_id(1)
    rows = n_t * n_seq
    tail = (cw_ref.shape[0] - 1) * n_seq
    head = uext_ref.shape[0] - rows
    n_blocks, cb = wa_ref.shape[0], wa_ref.shape[1]

    @pl.when(tc == 0)
    def _():
        uext_ref[head - tail:head, :] = cb_ref[0]
        h_ref[...] = h0_ref[0]

    @pl.when(tc > 0)
    def _():
        uext_ref[head - tail:head, :] = uext_ref[head + rows - tail:head + rows, :]

    uext_ref[head:head + rows, :] = u_ref[...]
    log_sig_lam = jax.nn.log_sigmoid(lam_ref[...])
    for c0 in range(0, rows, chunk):
        n = min(chunk, rows - c0)
        xc = cbias_ref[...]
        for i in range(cw_ref.shape[0]):
            xc = xc + uext_ref[head - tail + i * n_seq + c0:head - tail + i * n_seq + c0 + n, :] * cw_ref[i:i + 1, :]
        xb = xc.astype(BF16)
        r = jnp.concatenate([jnp.dot(xb[:, j * cb:(j + 1) * cb], wa_ref[j], preferred_element_type=F32)
                             for j in range(n_blocks)], axis=1)
        ig = jnp.concatenate([jnp.dot(xb[:, j * cb:(j + 1) * cb], wx_ref[j], preferred_element_type=F32)
                              for j in range(n_blocks)], axis=1)
        r = jax.nn.sigmoid(r + ba_ref[...])
        ig = jax.nn.sigmoid(ig + bx_ref[...])
        log_a = LRU_C * r * log_sig_lam
        a_ref[c0:c0 + n, :] = jnp.exp(log_a)
        d_ref[c0:c0 + n, :] = jnp.sqrt(-_expm1(2.0 * log_a)) * (ig * xc)

    def step(t, h):
        sl = pl.ds(pl.multiple_of(t * n_seq, n_seq), n_seq)
        h = a_ref[sl, :] * h + d_ref[sl, :]
        d_ref[sl, :] = h
        return h

    h_last = lax.fori_loop(0, n_t, step, h_ref[...], unroll=min(8, n_t))
    h_ref[...] = h_last
    for c0 in range(0, rows, chunk):
        n = min(chunk, rows - c0)
        y_ref[c0:c0 + n, :] = (d_ref[c0:c0 + n, :] * gate_ref[c0:c0 + n, :]).astype(y_ref.dtype)

    @pl.when(tc == pl.num_programs(1) - 1)
    def _():
        hlast_ref[0] = h_last
        cout_ref[0] = uext_ref[head + rows - tail:head + rows, :]


def rglru(gu, conv_buf, h0, conv_w, conv_b, w_a, b_a, w_x, b_x, lam, *, n_groups, n_t, n_seq, t_chunk, d_rnn):
    assert n_t % t_chunk == 0
    n_tc = n_t // t_chunk
    rows = t_chunk * n_seq
    tail = conv_buf.shape[1]
    assert tail <= rows
    head = -(-tail // 8) * 8
    chunk = min(rows, 256)
    vec = lambda a: a.reshape(1, d_rnn)
    const2 = lambda g, t: (0, 0)
    const3 = lambda g, t: (0, 0, 0)
    per_group = lambda g, t: (g, 0, 0)
    return pl.pallas_call(
        functools.partial(_rglru_kernel, n_t=t_chunk, n_seq=n_seq, chunk=chunk),
        grid=(n_groups, n_tc),
        in_specs=[
            pl.BlockSpec((rows, d_rnn), lambda g, t: (g * n_tc + t, 0)),
            pl.BlockSpec((rows, d_rnn), lambda g, t: (g * n_tc + t, 1)),
            pl.BlockSpec((1, tail, d_rnn), per_group),
            pl.BlockSpec((1, n_seq, d_rnn), per_group),
            pl.BlockSpec(conv_w.shape, const2),
            pl.BlockSpec((1, d_rnn), const2),
            pl.BlockSpec(w_a.shape, const3),
            pl.BlockSpec((1, d_rnn), const2),
            pl.BlockSpec(w_x.shape, const3),
            pl.BlockSpec((1, d_rnn), const2),
            pl.BlockSpec((1, d_rnn), const2),
        ],
        out_specs=[
            pl.BlockSpec((rows, d_rnn), lambda g, t: (g * n_tc + t, 0)),
            pl.BlockSpec((1, n_seq, d_rnn), per_group),
            pl.BlockSpec((1, tail, d_rnn), per_group),
        ],
        out_shape=[
            jax.ShapeDtypeStruct((gu.shape[0], d_rnn), BF16),
            jax.ShapeDtypeStruct((n_groups, n_seq, d_rnn), F32),
            jax.ShapeDtypeStruct((n_groups, tail, d_rnn), F32),
        ],
        scratch_shapes=[
            pltpu.VMEM((head + rows, d_rnn), F32),
            pltpu.VMEM((rows, d_rnn), F32),
            pltpu.VMEM((rows, d_rnn), F32),
            pltpu.VMEM((n_seq, d_rnn), F32),
        ],
        compiler_params=_params("parallel", "arbitrary"),
        name="rglru",
    )(gu, gu, conv_buf, h0, conv_w, vec(conv_b), w_a, vec(b_a), w_x, vec(b_x), vec(lam))


TM = 640
TN_MAX = 1280
TF = 1024


def _proj_tile(n):
    return max(t for t in range(LANES, TN_MAX + 1, LANES) if n % t == 0)
FOX_TQ = 512
FOX_BLK = 256
FOX_PAGES = 16
LRU_T_CHUNK = 256


def kernel(x_prompt, x_sample, cache_fox_k, cache_fox_v, cache_fox_logf, cache_swa_k, cache_swa_v, state_lru_h, state_lru_conv, page_table, norm_mix_pre, norm_mix_post, norm_mlp_pre, norm_mlp_post, mlp_w_up, mlp_w_down, fox_w_qkv, fox_w_f, fox_b_f, fox_w_o, swa_w_qkv, swa_b_qkv, swa_sinks, swa_w_o, swa_b_o, lru_w_gate, lru_b_gate, lru_w_in, lru_b_in, lru_conv_w, lru_conv_b, lru_w_a, lru_b_a, lru_w_x, lru_b_x, lru_lambda, lru_w_out, lru_b_out):
    bp, seq, d = x_prompt.shape
    bs, n_t, _ = x_sample.shape
    mp, ms = bp * seq, bs * n_t
    depth = norm_mix_pre.shape[0]
    fox_heads = fox_w_f.shape[2]
    fox_kv, fox_hd = cache_fox_k.shape[3], cache_fox_k.shape[4]
    swa_heads = swa_sinks.shape[1]
    swa_kv, swa_hd = cache_swa_k.shape[3], cache_swa_k.shape[4]
    d_rnn = lru_w_gate.shape[2]
    conv_w = lru_conv_w.shape[1]
    past = page_table.shape[1] * cache_fox_k.shape[2]
    fox_qw, fox_kw = fox_heads * fox_hd, fox_kv * fox_hd
    swa_qw, swa_kw = swa_heads * swa_hd, swa_kv * swa_hd

    x = jnp.concatenate([x_prompt.reshape(mp, d), x_sample.reshape(ms, d)], axis=0)
    n_layers_fox, pool, page = cache_fox_k.shape[:3]
    ck = cache_fox_k.reshape(n_layers_fox, pool, page * fox_kv, fox_hd)
    cv = cache_fox_v.reshape(n_layers_fox, pool, page * fox_kv, fox_hd)
    clf_t = jnp.swapaxes(cache_fox_logf, 2, 3)

    w_up_all, w_down_all = mlp_w_up.astype(BF16), mlp_w_down.astype(BF16)

    fkp, fvp, flp, fks, fvs, fls = [], [], [], [], [], []
    skp, svp, sks, svs = [], [], [], []
    lhp, lcp, lhs, lcs = [], [], [], []
    for i in range(depth):
        kind, j = i % 3, i // 3
        if kind == 0:
            qkv, logf, k_rows, v_rows = norm_matmul(x, norm_mix_pre[i], fox_w_qkv[j].astype(BF16),
                                                    gate=(fox_w_f[j].astype(BF16), fox_b_f[j]),
                                                    kv_heads=(fox_kv, fox_hd), tm=TM,
                                                    tn=_proj_tile(fox_w_qkv.shape[2]))
            fkp.append(k_rows[:mp * fox_kv].reshape(bp, seq, fox_kv, fox_hd))
            fvp.append(v_rows[:mp * fox_kv].reshape(bp, seq, fox_kv, fox_hd))
            flp.append(logf[:mp].reshape(bp, seq, fox_heads))
            qkv_s = qkv[mp:].reshape(bs, n_t, -1)
            lf_s = logf[mp:].reshape(bs, n_t, fox_heads)
            fks.append(k_rows[mp * fox_kv:].reshape(bs, n_t, fox_kv, fox_hd))
            fvs.append(v_rows[mp * fox_kv:].reshape(bs, n_t, fox_kv, fox_hd))
            fls.append(lf_s)
            qx, kx = fox_cumsum(logf[:mp].reshape(bp, seq, fox_heads), n_kv=fox_kv, chunk=FOX_BLK)
            att_p = fox_flash(qkv, qx, kx, batch=bp, seq=seq, n_heads=fox_heads, n_kv=fox_kv, hd=fox_hd,
                              tq=FOX_TQ, blk=FOX_TQ)
            fox_g = fox_heads // fox_kv
            q_rows = qkv_s[:, :, :fox_qw].reshape(bs, n_t, fox_kv, fox_g, fox_hd).transpose(0, 2, 1, 3, 4)
            q_rows = q_rows.reshape(bs, fox_kv, n_t * fox_g, fox_hd)
            lf_new_t = jnp.pad(jnp.swapaxes(lf_s, 1, 2), ((0, 0), (0, 0), (0, page - n_t)))
            att_s = fox_decode(q_rows, qkv_s, lf_new_t, ck, cv, clf_t, page_table, j,
                               n_heads=fox_heads, n_kv=fox_kv, hd=fox_hd, pages=FOX_PAGES)
            att_s = att_s.reshape(bs, fox_kv, n_t, fox_g, fox_hd).transpose(0, 2, 1, 3, 4).reshape(ms, fox_qw)
            att = lax.dynamic_update_slice(att_p, att_s.astype(BF16), (mp, 0))
            x = matmul_norm_residual(att, fox_w_o[j].astype(BF16), None, norm_mix_post[i], x, tm=TM)
        elif kind == 1:
            qkv = norm_matmul(x, norm_mix_pre[i], swa_w_qkv[j].astype(BF16), swa_b_qkv[j], tm=TM,
                              tn=_proj_tile(swa_w_qkv.shape[2]))
            att_p, k_last = swa_prompt(qkv, swa_sinks[j], batch=bp, seq=seq, n_heads=swa_heads, n_kv=swa_kv,
                                       hd=swa_hd, blk=WINDOW)
            keep = min(WINDOW, seq)
            skp.append(k_last.reshape(bp, keep, swa_kv, swa_hd))
            v_last = jnp.stack([qkv[(b + 1) * seq - keep:(b + 1) * seq, swa_qw + swa_kw:] for b in range(bp)])
            svp.append(v_last.reshape(bp, keep, swa_kv, swa_hd))
            qkv_s = qkv[mp:].reshape(bs, n_t, -1)
            q_rows = qkv_s[:, :, :swa_qw].reshape(bs, n_t * swa_heads, swa_hd)
            nbuf = cache_swa_k.shape[2]
            att_s, nk, nv = swa_sample(q_rows, qkv_s, cache_swa_k[j].reshape(bs, nbuf, swa_kw),
                                       cache_swa_v[j].reshape(bs, nbuf, swa_kw), swa_sinks[j], past,
                                       n_heads=swa_heads, n_kv=swa_kv, hd=swa_hd)
            sks.append(nk.reshape(bs, nbuf, swa_kv, swa_hd))
            svs.append(nv.reshape(bs, nbuf, swa_kv, swa_hd))
            att = lax.dynamic_update_slice(att_p, att_s.reshape(ms, swa_qw).astype(BF16), (mp, 0))
            x = matmul_norm_residual(att, swa_w_o[j].astype(BF16), swa_b_o[j], norm_mix_post[i], x, tm=TM)
        else:
            w_gu = jnp.concatenate([lru_w_gate[j], lru_w_in[j]], axis=1).astype(BF16)
            b_gu = jnp.concatenate([lru_b_gate[j], lru_b_in[j]])
            gu = norm_matmul(x, norm_mix_pre[i], w_gu, b_gu, n_gelu_cols=d_rnn, tm=TM, tn=_proj_tile(d_rnn))
            lw = (lru_conv_w[j], lru_conv_b[j], lru_w_a[j].astype(BF16), lru_b_a[j], lru_w_x[j].astype(BF16),
                  lru_b_x[j], lru_lambda[j])
            y_p, h_p, c_p = rglru(gu, jnp.zeros((bp, conv_w - 1, d_rnn), F32), jnp.zeros((bp, 1, d_rnn), F32),
                                  *lw, n_groups=bp, n_t=seq, n_seq=1, t_chunk=LRU_T_CHUNK, d_rnn=d_rnn)
            lhp.append(h_p.reshape(bp, d_rnn))
            lcp.append(c_p)
            gu_s = gu[mp:].reshape(bs, n_t, -1).swapaxes(0, 1).reshape(ms, -1)
            cb_s = state_lru_conv[j].swapaxes(0, 1).reshape(1, (conv_w - 1) * bs, d_rnn)
            y_s, h_s, c_s = rglru(gu_s, cb_s, state_lru_h[j].reshape(1, bs, d_rnn), *lw,
                                  n_groups=1, n_t=n_t, n_seq=bs, t_chunk=n_t, d_rnn=d_rnn)
            lhs.append(h_s.reshape(bs, d_rnn))
            lcs.append(c_s.reshape(conv_w - 1, bs, d_rnn).swapaxes(0, 1))
            y_s = y_s.reshape(n_t, bs, d_rnn).swapaxes(0, 1).reshape(ms, d_rnn)
            y = lax.dynamic_update_slice(y_p, y_s, (mp, 0))
            x = matmul_norm_residual(y, lru_w_out[j].astype(BF16), lru_b_out[j], norm_mix_post[i], x, tm=TM)
        x = mlp_sublayer(x, norm_mlp_pre[i], w_up_all, w_down_all, norm_mlp_post[i], i, tm=TM, tf=TF)
    return (x[:mp].reshape(bp, seq, d), x[mp:].reshape(bs, n_t, d),
            jnp.stack(fkp), jnp.stack(fvp), jnp.stack(flp),
            jnp.stack(fks), jnp.stack(fvs), jnp.stack(fls),
            jnp.stack(skp), jnp.stack(svp), jnp.stack(sks), jnp.stack(svs),
            jnp.stack(lhp), jnp.stack(lcp), jnp.stack(lhs), jnp.stack(lcs))
```
